```python
import jax, jax.numpy as jnp
from jax import lax
import numpy as np

D_MODEL = 1024
BATCH = 2
SEQ = 16384
DEPTH = 2

N_MIXERS = 2
NORM_EPS = 1e-6
MLA_HEADS = 8
MLA_Q_LORA = 384
MLA_KV_LORA = 256
MLA_NOPE = 128
MLA_ROPE = 64
MLA_V = 128
ROPE_THETA = 10000.0
ATTN_QBLOCK = 128
HG_HEADS = 8
HG_DK = D_MODEL // HG_HEADS
HG_DV = D_MODEL // HG_HEADS
HG_CHUNK = 64
PEER_HEADS = 8
PEER_NKEYS = 128
PEER_EXPERTS = PEER_NKEYS * PEER_NKEYS
PEER_QDIM = 256
PEER_HALF = PEER_QDIM // 2
PEER_TOPK = 16
PEER_TBLOCK = 128

kernel_name = 'hybrid_mla_hgrn2_peer'


def _rmsnorm(x, g):
    xf = x.astype(jnp.float32)
    y = xf * lax.rsqrt(jnp.mean(xf * xf, axis=-1, keepdims=True) + NORM_EPS)
    return (y * g.astype(jnp.float32)).astype(x.dtype)


def _rope_tables(positions):
    inv = ROPE_THETA ** (-jnp.arange(0, MLA_ROPE, 2, dtype=jnp.float32) / MLA_ROPE)
    ang = positions.astype(jnp.float32)[..., None] * inv
    return jnp.cos(ang), jnp.sin(ang)


def _rope(x, cos, sin):
    x1, x2 = jnp.split(x.astype(jnp.float32), 2, axis=-1)
    return jnp.concatenate([x1 * cos - x2 * sin, x2 * cos + x1 * sin], axis=-1).astype(x.dtype)


def _mla(h, cos, sin, w_in, q_norm, w_q_up, kv_norm, w_kv_up, w_out):
    B, S, _ = h.shape
    z = h @ w_in
    c_q = z[..., :MLA_Q_LORA]
    c_kv = z[..., MLA_Q_LORA:MLA_Q_LORA + MLA_KV_LORA]
    k_rope = _rope(z[..., MLA_Q_LORA + MLA_KV_LORA:], cos, sin)
    q = (_rmsnorm(c_q, q_norm) @ w_q_up).reshape(B, S, MLA_HEADS, MLA_NOPE + MLA_ROPE)
    q_nope = q[..., :MLA_NOPE]
    q_rope = _rope(q[..., MLA_NOPE:], cos[:, :, None], sin[:, :, None])
    kv = (_rmsnorm(c_kv, kv_norm) @ w_kv_up).reshape(B, S, MLA_HEADS, MLA_NOPE + MLA_V)
    k_nope = kv[..., :MLA_NOPE]
    v = kv[..., MLA_NOPE:]
    scale = (MLA_NOPE + MLA_ROPE) ** -0.5
    nb = S // ATTN_QBLOCK
    qn_b = (q_nope * scale).reshape(B, nb, ATTN_QBLOCK, MLA_HEADS, MLA_NOPE).swapaxes(0, 1)
    qr_b = (q_rope * scale).reshape(B, nb, ATTN_QBLOCK, MLA_HEADS, MLA_ROPE).swapaxes(0, 1)
    starts = jnp.arange(nb, dtype=jnp.int32) * ATTN_QBLOCK
    kpos = jnp.arange(S, dtype=jnp.int32)

    def block(args):
        qn, qr, start = args
        s = (jnp.einsum('bqhd,bkhd->bhqk', qn, k_nope)
             + jnp.einsum('bqhr,bkr->bhqk', qr, k_rope)).astype(jnp.float32)
        qpos = start + jnp.arange(ATTN_QBLOCK, dtype=jnp.int32)
        mask = kpos[None, :] <= qpos[:, None]
        s = jnp.where(mask, s, -jnp.inf)
        p = jax.nn.softmax(s, axis=-1).astype(v.dtype)
        return jnp.einsum('bhqk,bkhd->bqhd', p, v)

    o = lax.map(block, (qn_b, qr_b, starts))
    o = o.swapaxes(0, 1).reshape(B, S, MLA_HEADS * MLA_V)
    return o @ w_out


def _hgrn2(h, lb, w_in, out_norm, w_out):
    B, S, _ = h.shape
    F = HG_HEADS * HG_DK
    E = HG_HEADS * HG_DV
    z = h @ w_in
    q_pre = z[..., :F]
    f_pre = z[..., F:2 * F]
    i_in = z[..., 2 * F:2 * F + E]
    g_pre = z[..., 2 * F + E:]
    f = lb + (1.0 - lb) * jax.nn.sigmoid(f_pre.astype(jnp.float32))
    log_f = jnp.log(f)
    k = 1.0 - f
    q = jax.nn.silu(q_pre.astype(jnp.float32))
    v = i_in.astype(jnp.float32)
    nc = S // HG_CHUNK

    def chunks(t, d):
        return t.reshape(B, nc, HG_CHUNK, HG_HEADS, d).transpose(1, 0, 3, 2, 4)

    causal = jnp.tril(jnp.ones((HG_CHUNK, HG_CHUNK), dtype=bool))[:, :, None]

    def step(state, inp):
        qc, kc, vc, lfc = inp
        b = jnp.cumsum(lfc, axis=2)
        o_inter = jnp.einsum('bhtd,bhde->bhte', qc * jnp.exp(b), state)
        diff = b[:, :, :, None, :] - b[:, :, None, :, :]
        decay = jnp.where(causal, jnp.exp(jnp.where(causal, diff, 0.0)), 0.0)
        a = jnp.einsum('bhtd,bhsd,bhtsd->bhts', qc, kc, decay)
        o_intra = jnp.einsum('bhts,bhse->bhte', a, vc)
        b_last = b[:, :, -1:, :]
        state = (jnp.exp(b_last[:, :, 0, :])[..., None] * state
                 + jnp.einsum('bhsd,bhse->bhde', kc * jnp.exp(b_last - b), vc))
        return state, o_inter + o_intra

    state0 = jnp.zeros((B, HG_HEADS, HG_DK, HG_DV), jnp.float32)
    _, o = lax.scan(step, state0, (chunks(q, HG_DK), chunks(k, HG_DK), chunks(v, HG_DV), chunks(log_f, HG_DK)))
    o = o.transpose(1, 0, 3, 2, 4).reshape(B, S, HG_HEADS, HG_DV)
    o = o * lax.rsqrt(jnp.mean(o * o, axis=-1, keepdims=True) + NORM_EPS)
    o = o.reshape(B, S, E) * out_norm.astype(jnp.float32) * jax.nn.silu(g_pre.astype(jnp.float32))
    return o.astype(h.dtype) @ w_out


def _peer(h, w_q, sub_keys, u, v):
    B, S, D = h.shape
    xb = h.reshape(B * S // PEER_TBLOCK, PEER_TBLOCK, D)

    def block(xt):
        q = (xt @ w_q).reshape(PEER_TBLOCK, PEER_HEADS, 2, PEER_HALF)
        s = jnp.einsum('thpd,hpnd->thpn', q, sub_keys).astype(jnp.float32)
        s1, i1 = lax.top_k(s[:, :, 0], PEER_TOPK)
        s2, i2 = lax.top_k(s[:, :, 1], PEER_TOPK)
        cand = (s1[..., :, None] + s2[..., None, :]).reshape(PEER_TBLOCK, PEER_HEADS, PEER_TOPK * PEER_TOPK)
        cidx = (i1[..., :, None] * PEER_NKEYS + i2[..., None, :]).reshape(PEER_TBLOCK, PEER_HEADS, PEER_TOPK * PEER_TOPK)
        top, pos = lax.top_k(cand, PEER_TOPK)
        eidx = jnp.take_along_axis(cidx, pos, axis=-1)
        gates = jax.nn.softmax(top, axis=-1)
        act = jax.nn.gelu(jnp.einsum('td,thkd->thk', xt, u[eidx]).astype(jnp.float32), approximate=False)
        return jnp.einsum('thk,thkd->td', (gates * act).astype(xt.dtype), v[eidx])

    return lax.map(block, xb).reshape(B, S, D)


def setup_inputs(seed: int = 0) -> dict:
    key = jax.random.key(seed)
    ks = jax.random.split(key, 20)
    n_a = (DEPTH + N_MIXERS - 1) // N_MIXERS
    n_b = DEPTH // N_MIXERS
    F = HG_HEADS * HG_DK
    E = HG_HEADS * HG_DV

    def nrm(k, shape, scale):
        return jax.random.normal(k, shape, jnp.float32) * scale

    def gain(k, shape):
        return 1.0 + 0.02 * jax.random.normal(k, shape, jnp.float32)

    return {
        'x': nrm(ks[0], (BATCH, SEQ, D_MODEL), 1.0),
        'positions': jnp.broadcast_to(jnp.arange(SEQ, dtype=jnp.int32), (BATCH, SEQ)),
        'ln_mix': gain(ks[1], (DEPTH, D_MODEL)),
        'ln_ffn': gain(ks[2], (DEPTH, D_MODEL)),
        'ln_final': gain(ks[3], (D_MODEL,)),
        'mla_w_in': nrm(ks[4], (n_a, D_MODEL, MLA_Q_LORA + MLA_KV_LORA + MLA_ROPE), D_MODEL ** -0.5),
        'mla_q_norm': gain(ks[5], (n_a, MLA_Q_LORA)),
        'mla_w_q_up': nrm(ks[6], (n_a, MLA_Q_LORA, MLA_HEADS * (MLA_NOPE + MLA_ROPE)), MLA_Q_LORA ** -0.5),
        'mla_kv_norm': gain(ks[7], (n_a, MLA_KV_LORA)),
        'mla_w_kv_up': nrm(ks[8], (n_a, MLA_KV_LORA, MLA_HEADS * (MLA_NOPE + MLA_V)), MLA_KV_LORA ** -0.5),
        'mla_w_out': nrm(ks[9], (n_a, MLA_HEADS * MLA_V, D_MODEL), (MLA_HEADS * MLA_V) ** -0.5),
        'hg_w_in': nrm(ks[10], (n_b, D_MODEL, 2 * F + 2 * E), D_MODEL ** -0.5),
        'hg_lb': nrm(ks[11], (DEPTH, F), 0.5),
        'hg_out_norm': gain(ks[12], (n_b, E)),
        'hg_w_out': nrm(ks[13], (n_b, E, D_MODEL), E ** -0.5),
        'peer_w_q': nrm(ks[14], (DEPTH, D_MODEL, PEER_HEADS * PEER_QDIM), D_MODEL ** -0.5),
        'peer_sub_keys': nrm(ks[15], (DEPTH, PEER_HEADS, 2, PEER_NKEYS, PEER_HALF), PEER_HALF ** -0.5),
        'peer_u': nrm(ks[16], (DEPTH, PEER_EXPERTS, D_MODEL), D_MODEL ** -0.5),
        'peer_v': nrm(ks[17], (DEPTH, PEER_EXPERTS, D_MODEL), (PEER_HEADS * PEER_TOPK) ** -0.5),
    }


def reference(x, positions, ln_mix, ln_ffn, ln_final, mla_w_in, mla_q_norm, mla_w_q_up, mla_kv_norm,
              mla_w_kv_up, mla_w_out, hg_w_in, hg_lb, hg_out_norm, hg_w_out, peer_w_q, peer_sub_keys,
              peer_u, peer_v):
    cos, sin = _rope_tables(positions)
    p = jax.nn.softmax(hg_lb.astype(jnp.float32), axis=0)
    lb_all = jnp.cumsum(p, axis=0) - p[0:1]
    h = x
    for i in range(DEPTH):
        j = i // N_MIXERS
        hn = _rmsnorm(h, ln_mix[i])
        if i % N_MIXERS == 0:
            h = h + _mla(hn, cos, sin, mla_w_in[j], mla_q_norm[j], mla_w_q_up[j], mla_kv_norm[j],
                         mla_w_kv_up[j], mla_w_out[j])
        else:
            h = h + _hgrn2(hn, lb_all[i], hg_w_in[j], hg_out_norm[j], hg_w_out[j])
        h = h + _peer(_rmsnorm(h, ln_ffn[i]), peer_w_q[i], peer_sub_keys[i], peer_u[i], peer_v[i])
    return _rmsnorm(h, ln_final)
```

```python
import functools

import jax
import jax.numpy as jnp
from jax import lax
from jax.experimental import pallas as pl
from jax.experimental.pallas import tpu as pltpu
from jax.experimental.pallas import tpu_sc as plsc

D_MODEL = 1024
NORM_EPS = 1e-6
MLA_HEADS = 8
MLA_Q_LORA = 384
MLA_KV_LORA = 256
MLA_NOPE = 128
MLA_ROPE = 64
MLA_V = 128
ROPE_THETA = 10000.0
HG_HEADS = 8
HG_DK = 128
HG_DV = 128
PEER_HEADS = 8
PEER_NKEYS = 128
PEER_HALF = 128
PEER_TOPK = 16
PEER_SLOTS = PEER_HEADS * PEER_TOPK

LANES = 128
SC_CORES = 2
SC_SUBCORES = 16
VMEM_LIMIT = 48 * 1024 * 1024

ROW_WORDS = D_MODEL // 2
TOKEN_BLOCK = 256
ATTN_BLOCK = 512
HG_BLOCK = 256
HG_CHUNK = 64
HG_SUB = 16
COMBINE_TOKENS = 16
GATHER_WINDOW = 64
GATHER_TOKENS = 2048
NEG_INF = float("-inf")


def _rms(x, g):
    return x * lax.rsqrt(jnp.mean(x * x, axis=-1, keepdims=True) + NORM_EPS) * g


def _params(*sem):
    return pltpu.CompilerParams(dimension_semantics=sem, vmem_limit_bytes=VMEM_LIMIT)


def _pack_kernel(t_ref, o_ref):
    t = t_ref[...]
    lo = pltpu.bitcast(t[:, :ROW_WORDS].astype(jnp.bfloat16).astype(jnp.float32), jnp.uint32)
    hi = pltpu.bitcast(t[:, ROW_WORDS:].astype(jnp.bfloat16).astype(jnp.float32), jnp.uint32)
    o_ref[...] = (lo >> 16) | (hi & jnp.uint32(0xFFFF0000))


def _pack_table(tab):
    e, d = tab.shape
    rows = 512
    return pl.pallas_call(
        _pack_kernel,
        grid=(e // rows,),
        in_specs=[pl.BlockSpec((rows, d), lambda i: (i, 0))],
        out_specs=pl.BlockSpec((rows, d // 2), lambda i: (i, 0)),
        out_shape=jax.ShapeDtypeStruct((e, d // 2), jnp.uint32),
        compiler_params=_params("parallel"),
        name="pack_table",
    )(tab)


def _unpack_words(w):
    lo = pltpu.bitcast(w << 16, jnp.float32)
    hi = pltpu.bitcast(w & jnp.uint32(0xFFFF0000), jnp.float32)
    return lo, hi


def _rope_kernel(pos_ref, inv_ref, sign_ref, c_ref, s_ref):
    ang = pos_ref[...].astype(jnp.float32) * inv_ref[...]
    c_ref[...] = jnp.cos(ang)
    s_ref[...] = jnp.sin(ang) * sign_ref[...]


def _rope_tables(positions):
    n = positions.size
    lane = jnp.arange(LANES)
    inv = ROPE_THETA ** (-(2 * (lane % (MLA_ROPE // 2))).astype(jnp.float32) / MLA_ROPE)
    sign = jnp.where((lane % MLA_ROPE) < MLA_ROPE // 2, -1.0, 1.0).astype(jnp.float32)
    tb = 1024
    out = jax.ShapeDtypeStruct((n, LANES), jnp.float32)
    return pl.pallas_call(
        _rope_kernel,
        grid=(n // tb,),
        in_specs=[pl.BlockSpec((tb, 1), lambda i: (i, 0)),
                  pl.BlockSpec((1, LANES), lambda i: (0, 0)),
                  pl.BlockSpec((1, LANES), lambda i: (0, 0))],
        out_specs=[pl.BlockSpec((tb, LANES), lambda i: (i, 0))] * 2,
        out_shape=[out, out],
        compiler_params=_params("parallel"),
        name="rope_tables",
    )(positions.reshape(n, 1), inv.reshape(1, LANES), sign.reshape(1, LANES))


def _mla_proj_kernel(x_ref, g_ref, c_ref, s_ref, win_ref, qn_ref, wq_ref, kvn_ref, wkv_ref,
                     q_out, k_out, v_out):
    hn = _rms(x_ref[...], g_ref[...]).astype(jnp.bfloat16)
    z = jnp.dot(hn, win_ref[...], preferred_element_type=jnp.float32)
    c = c_ref[...]
    s = s_ref[...]
    o_kv = MLA_Q_LORA
    o_kr = MLA_Q_LORA + MLA_KV_LORA
    k_rope = (z[:, o_kr:o_kr + LANES] * c + z[:, o_kr + LANES:o_kr + 2 * LANES] * s).astype(jnp.bfloat16)
    cq = _rms(z[:, :MLA_Q_LORA], qn_ref[...]).astype(jnp.bfloat16)
    q = jnp.dot(cq, wq_ref[...], preferred_element_type=jnp.float32)
    ckv = _rms(z[:, o_kv:o_kr], kvn_ref[...]).astype(jnp.bfloat16)
    kv = jnp.dot(ckv, wkv_ref[...], preferred_element_type=jnp.float32)
    scale = (MLA_NOPE + MLA_ROPE) ** -0.5
    hw = MLA_HEADS * LANES
    for h in range(MLA_HEADS):
        sl = slice(h * LANES, (h + 1) * LANES)
        q_out[h, :, :LANES] = (q[:, sl] * scale).astype(jnp.bfloat16)
        qr = q[:, hw + h * LANES:hw + (h + 1) * LANES] * c + q[:, 2 * hw + h * LANES:2 * hw + (h + 1) * LANES] * s
        q_out[h, :, LANES:] = (qr * scale).astype(jnp.bfloat16)
        k_out[h, :, :LANES] = kv[:, sl].astype(jnp.bfloat16)
        k_out[h, :, LANES:] = k_rope
        v_out[h] = kv[:, hw + h * LANES:hw + (h + 1) * LANES].astype(jnp.bfloat16)


def _swap_halves(w):
    half = w.shape[-1] // 2
    return jnp.concatenate([w[..., half:], w[..., :half]], axis=-1)


def _pad_lanes(w):
    return jnp.pad(w, [(0, 0)] * (w.ndim - 1) + [(0, LANES - w.shape[-1])])


def _mla_weights(w_in, w_q_up, w_kv_up):
    o_kr = MLA_Q_LORA + MLA_KV_LORA
    w_kr = w_in[:, o_kr:]
    win = jnp.concatenate([w_in[:, :o_kr], _pad_lanes(w_kr), _pad_lanes(_swap_halves(w_kr))], axis=1)
    wq = w_q_up.reshape(MLA_Q_LORA, MLA_HEADS, MLA_NOPE + MLA_ROPE)
    wq_n = wq[:, :, :MLA_NOPE].reshape(MLA_Q_LORA, -1)
    wq_r = _pad_lanes(wq[:, :, MLA_NOPE:]).reshape(MLA_Q_LORA, -1)
    wq_rs = _pad_lanes(_swap_halves(wq[:, :, MLA_NOPE:])).reshape(MLA_Q_LORA, -1)
    wqp = jnp.concatenate([wq_n, wq_r, wq_rs], axis=1)
    wkv = w_kv_up.reshape(MLA_KV_LORA, MLA_HEADS, MLA_NOPE + MLA_V)
    wkvp = jnp.concatenate([wkv[:, :, :MLA_NOPE].reshape(MLA_KV_LORA, -1),
                            wkv[:, :, MLA_NOPE:].reshape(MLA_KV_LORA, -1)], axis=1)
    return win.astype(jnp.bfloat16), wqp.astype(jnp.bfloat16), wkvp.astype(jnp.bfloat16)


def _mla_proj(h, g, cos_t, sin_t, win, q_norm, wq, kv_norm, wkv):
    n = h.shape[0]
    tb = TOKEN_BLOCK
    full = lambda a: pl.BlockSpec(a.shape, lambda i: (0,) * a.ndim)
    g = g.reshape(1, -1)
    q_norm = q_norm.reshape(1, -1)
    kv_norm = kv_norm.reshape(1, -1)
    qk_t = jax.ShapeDtypeStruct((MLA_HEADS, n, 2 * LANES), jnp.bfloat16)
    v_t = jax.ShapeDtypeStruct((MLA_HEADS, n, MLA_V), jnp.bfloat16)
    return pl.pallas_call(
        _mla_proj_kernel,
        grid=(n // tb,),
        in_specs=[pl.BlockSpec((tb, D_MODEL), lambda i: (i, 0)), full(g),
                  pl.BlockSpec((tb, LANES), lambda i: (i, 0)), pl.BlockSpec((tb, LANES), lambda i: (i, 0)),
                  full(win), full(q_norm), full(wq), full(kv_norm), full(wkv)],
        out_specs=[pl.BlockSpec((MLA_HEADS, tb, 2 * LANES), lambda i: (0, i, 0)),
                   pl.BlockSpec((MLA_HEADS, tb, 2 * LANES), lambda i: (0, i, 0)),
                   pl.BlockSpec((MLA_HEADS, tb, MLA_V), lambda i: (0, i, 0))],
        out_shape=[qk_t, qk_t, v_t],
        compiler_params=_params("parallel"),
        name="mla_proj",
    )(h, g, cos_t, sin_t, win, q_norm, wq, kv_norm, wkv)


def _attn_kernel(qi_ref, ki_ref, q_ref, k_ref, v_ref, o_ref, m_sc, l_sc, acc_sc):
    step = pl.program_id(2)
    qi = qi_ref[step]
    ki = ki_ref[step]

    @pl.when(ki == 0)
    def _():
        m_sc[...] = jnp.full(m_sc.shape, NEG_INF, jnp.float32)
        l_sc[...] = jnp.zeros(l_sc.shape, jnp.float32)
        acc_sc[...] = jnp.zeros(acc_sc.shape, jnp.float32)

    def update(masked):
        sc = lax.dot_general(q_ref[...], k_ref[...], (((1,), (1,)), ((), ())),
                             preferred_element_type=jnp.float32)
        if masked:
            row = lax.broadcasted_iota(jnp.int32, sc.shape, 0)
            col = lax.broadcasted_iota(jnp.int32, sc.shape, 1)
            sc = jnp.where(col <= row, sc, NEG_INF)
        m_prev = m_sc[...]
        m_next = jnp.maximum(m_prev, jnp.max(sc, axis=1, keepdims=True))
        p = jnp.exp(sc - m_next[:, :1])
        alpha = jnp.exp(m_prev - m_next)
        l_sc[...] = alpha * l_sc[...] + jnp.sum(p, axis=1, keepdims=True)
        acc_sc[...] = alpha * acc_sc[...] + jnp.dot(p.astype(jnp.bfloat16), v_ref[...],
                                                    preferred_element_type=jnp.float32)
        m_sc[...] = m_next

    @pl.when(ki < qi)
    def _():
        update(False)

    @pl.when(ki == qi)
    def _():
        update(True)
        o_ref[...] = (acc_sc[...] / l_sc[...]).astype(o_ref.dtype)


def _attention(q, k, v, batch):
    n = q.shape[1]
    seq = n // batch
    blk = min(ATTN_BLOCK, seq)
    nb = seq // blk
    qi = jnp.array([i for i in range(nb) for _ in range(i + 1)], jnp.int32)
    ki = jnp.array([j for i in range(nb) for j in range(i + 1)], jnp.int32)
    grid_spec = pltpu.PrefetchScalarGridSpec(
        num_scalar_prefetch=2,
        grid=(batch, MLA_HEADS, qi.shape[0]),
        in_specs=[pl.BlockSpec((None, blk, 2 * LANES), lambda b, h, s, qi, ki: (h, b * nb + qi[s], 0)),
                  pl.BlockSpec((None, blk, 2 * LANES), lambda b, h, s, qi, ki: (h, b * nb + ki[s], 0)),
                  pl.BlockSpec((None, blk, MLA_V), lambda b, h, s, qi, ki: (h, b * nb + ki[s], 0))],
        out_specs=pl.BlockSpec((blk, MLA_V), lambda b, h, s, qi, ki: (b * nb + qi[s], h)),
        scratch_shapes=[pltpu.VMEM((blk, LANES), jnp.float32), pltpu.VMEM((blk, LANES), jnp.float32),
                        pltpu.VMEM((blk, MLA_V), jnp.float32)],
    )
    return pl.pallas_call(
        _attn_kernel,
        grid_spec=grid_spec,
        out_shape=jax.ShapeDtypeStruct((n, MLA_HEADS * MLA_V), jnp.bfloat16),
        compiler_params=_params("parallel", "parallel", "arbitrary"),
        name="mla_attention",
    )(qi, ki, q, k, v)


def _top_rows(vals, ids, count, out_rows):
    t = vals.shape[1]
    big = jnp.int32(2 ** 30)
    orow = lax.broadcasted_iota(jnp.int32, (out_rows, t), 0)

    def body(r, carry):
        cur, ov, oi = carry
        m = jnp.max(cur, axis=0, keepdims=True)
        pick = jnp.min(jnp.where(cur == m, ids, big), axis=0, keepdims=True)
        cur = jnp.where(ids == pick, NEG_INF, cur)
        ov = jnp.where(orow == r, m, ov)
        oi = jnp.where(orow == r, pick, oi)
        return cur, ov, oi

    init = (vals, jnp.zeros((out_rows, t), jnp.float32), jnp.zeros((out_rows, t), jnp.int32))
    _, ov, oi = lax.fori_loop(0, count, body, init)
    return ov, oi


_ROW_SLABS = [(0, 0, 16), (1, 0, 8)] + [(a, 0, 8) for a in range(2, 8)]
_COL_SLAB = (8, 16, 0)
_PAIR_ROWS = sum(hi - lo for _, lo, hi in _ROW_SLABS) + (_COL_SLAB[1] - _COL_SLAB[0])


def _route_kernel(o_ref, h_ref, wo_ref, g_ref, wqt_ref, keys_ref, pos_ref,
                  hn_out, xn_out, eid_out, gate_out, qt_sc, v_sc, i_sc):
    tb = h_ref.shape[0]
    hnew = h_ref[...] + jnp.dot(o_ref[...], wo_ref[...], preferred_element_type=jnp.float32)
    hn_out[...] = hnew
    xn = _rms(hnew, g_ref[...])
    xn_out[...] = xn
    qt_sc[...] = lax.dot_general(wqt_ref[...], xn.astype(jnp.bfloat16), (((1,), (1,)), ((), ())),
                                 preferred_element_type=jnp.float32).astype(jnp.bfloat16)
    key_ids = lax.broadcasted_iota(jnp.int32, (PEER_NKEYS, tb), 0)

    def group(g, carry):
        row0 = pl.multiple_of(g * PEER_HALF, PEER_HALF)
        st = jnp.dot(keys_ref[g], qt_sc[pl.ds(row0, PEER_HALF), :], preferred_element_type=jnp.float32)
        tv, ti = _top_rows(st, key_ids, PEER_TOPK, PEER_TOPK)
        out0 = pl.multiple_of(g * PEER_TOPK, PEER_TOPK)
        v_sc[pl.ds(out0, PEER_TOPK), :] = tv
        i_sc[pl.ds(out0, PEER_TOPK), :] = ti
        return carry

    lax.fori_loop(0, 2 * PEER_HEADS, group, 0)

    pos = pos_ref[...]

    def head(hd, carry):
        base = pl.multiple_of(hd * 2 * PEER_TOPK, 2 * PEER_TOPK)
        v1 = v_sc[pl.ds(base, PEER_TOPK), :]
        i1 = i_sc[pl.ds(base, PEER_TOPK), :]
        v2 = v_sc[pl.ds(base + PEER_TOPK, PEER_TOPK), :]
        i2 = i_sc[pl.ds(base + PEER_TOPK, PEER_TOPK), :]
        cv, ce = [], []
        for a, lo, hi in _ROW_SLABS:
            cv.append(v1[a:a + 1, :] + v2[lo:hi, :])
            ce.append(i1[a:a + 1, :] * PEER_NKEYS + i2[lo:hi, :])
        a_lo, a_hi, b = _COL_SLAB
        cv.append(v1[a_lo:a_hi, :] + v2[b:b + 1, :])
        ce.append(i1[a_lo:a_hi, :] * PEER_NKEYS + i2[b:b + 1, :])
        cv = jnp.concatenate(cv, axis=0)
        ce = jnp.concatenate(ce, axis=0)
        tv, tp = _top_rows(cv, jnp.broadcast_to(pos, cv.shape), PEER_TOPK, PEER_TOPK)
        te = jnp.zeros((PEER_TOPK, tb), jnp.int32)
        orow = lax.broadcasted_iota(jnp.int32, (PEER_TOPK, tb), 0)
        for r in range(PEER_TOPK):
            e_r = jnp.sum(jnp.where(pos == tp[r:r + 1, :], ce, 0), axis=0, keepdims=True)
            te = jnp.where(orow == r, e_r, te)
        ex = jnp.exp(tv - tv[0:1, :])
        gates = ex / jnp.sum(ex, axis=0, keepdims=True)
        out0 = pl.multiple_of(hd * PEER_TOPK, PEER_TOPK)
        v_sc[pl.ds(out0, PEER_TOPK), :] = gates
        i_sc[pl.ds(out0, PEER_TOPK), :] = te
        return carry

    lax.fori_loop(0, PEER_HEADS, head, 0)

    eid_out[...] = jnp.transpose(i_sc[:PEER_SLOTS, :].astype(jnp.float32)).astype(jnp.int32)
    gates_all = v_sc[:PEER_SLOTS, :]
    for c in range(tb // COMBINE_TOKENS):
        gate_out[c] = gates_all[:, c * COMBINE_TOKENS:(c + 1) * COMBINE_TOKENS]


def _pair_positions():
    pos = [a * PEER_TOPK + b for a, lo, hi in _ROW_SLABS for b in range(lo, hi)]
    a_lo, a_hi, b = _COL_SLAB
    pos += [a * PEER_TOPK + b for a in range(a_lo, a_hi)]
    return jnp.array(pos, jnp.int32).reshape(_PAIR_ROWS, 1)


def _route(o, h, w_out, g, wqt, keys):
    n = h.shape[0]
    tb = TOKEN_BLOCK
    full = lambda a: pl.BlockSpec(a.shape, lambda i: (0,) * a.ndim)
    g = g.reshape(1, -1)
    pos = _pair_positions()
    row = pl.BlockSpec((tb, D_MODEL), lambda i: (i, 0))
    f32 = jnp.float32
    return pl.pallas_call(
        _route_kernel,
        grid=(n // tb,),
        in_specs=[row, row, full(w_out), full(g), full(wqt), full(keys), full(pos)],
        out_specs=[row, row, pl.BlockSpec((tb, PEER_SLOTS), lambda i: (i, 0)),
                   pl.BlockSpec((tb // COMBINE_TOKENS, PEER_SLOTS, COMBINE_TOKENS), lambda i: (i, 0, 0))],
        out_shape=[jax.ShapeDtypeStruct((n, D_MODEL), f32), jax.ShapeDtypeStruct((n, D_MODEL), f32),
                   jax.ShapeDtypeStruct((n, PEER_SLOTS), jnp.int32),
                   jax.ShapeDtypeStruct((n // COMBINE_TOKENS, PEER_SLOTS, COMBINE_TOKENS), f32)],
        scratch_shapes=[pltpu.VMEM((2 * PEER_HEADS * PEER_HALF, tb), jnp.bfloat16),
                        pltpu.VMEM((2 * PEER_HEADS * PEER_TOPK, tb), f32),
                        pltpu.VMEM((2 * PEER_HEADS * PEER_TOPK, tb), jnp.int32)],
        compiler_params=_params("parallel"),
        name="peer_route",
    )(o, h, w_out, g, wqt, keys, pos)


def _sc_gather(tab_u, tab_v, idx):
    p = idx.shape[0]
    w = GATHER_WINDOW
    per_worker = p // (SC_CORES * SC_SUBCORES)
    mesh = plsc.VectorSubcoreMesh(core_axis_name="c", subcore_axis_name="s")
    out_t = jax.ShapeDtypeStruct((p, ROW_WORDS), jnp.uint32)

    @functools.partial(
        pl.kernel, out_type=(out_t, out_t), mesh=mesh,
        scratch_types=[pltpu.VMEM((w,), jnp.int32),
                       pltpu.VMEM((w, ROW_WORDS), jnp.uint32),
                       pltpu.VMEM((w, ROW_WORDS), jnp.uint32),
                       pltpu.SemaphoreType.DMA, pltpu.SemaphoreType.DMA],
        name="peer_gather")
    def gather(u_hbm, v_hbm, i_hbm, ou_hbm, ov_hbm, idx_v, rows_u, rows_v, sem_u, sem_v):
        wid = lax.axis_index("s") * SC_CORES + lax.axis_index("c")
        base = wid * per_worker

        @pl.loop(0, per_worker // w)
        def _(it):
            off = pl.multiple_of(base + it * w, w)
            pltpu.sync_copy(i_hbm.at[pl.ds(off, w)], idx_v)
            cu = pltpu.async_copy(u_hbm.at[idx_v], rows_u, sem_u)
            cv = pltpu.async_copy(v_hbm.at[idx_v], rows_v, sem_v)
            cu.wait()
            pltpu.sync_copy(rows_u, ou_hbm.at[pl.ds(off, w)])
            cv.wait()
            pltpu.sync_copy(rows_v, ov_hbm.at[pl.ds(off, w)])

    return gather(tab_u, tab_v, idx)


def _combine_kernel(gu_ref, gv_ref, x_ref, gate_ref, h_ref, gf_ref, o_ref, *, final_norm):
    tk = x_ref.shape[0]
    for t in range(tk):
        rows = slice(t * PEER_SLOTS, (t + 1) * PEER_SLOTS)
        ulo, uhi = _unpack_words(gu_ref[rows, :])
        x = x_ref[t:t + 1, :]
        act = jnp.sum(ulo * x[:, :ROW_WORDS] + uhi * x[:, ROW_WORDS:], axis=1, keepdims=True)
        gelu = 0.5 * act * (1.0 + lax.erf(act * (2.0 ** -0.5)))
        w = gate_ref[:, t:t + 1] * gelu
        vlo, vhi = _unpack_words(gv_ref[rows, :])
        y_lo = jnp.sum(vlo * w, axis=0, keepdims=True)
        y_hi = jnp.sum(vhi * w, axis=0, keepdims=True)
        o_ref[t:t + 1, :ROW_WORDS] = h_ref[t:t + 1, :ROW_WORDS] + y_lo
        o_ref[t:t + 1, ROW_WORDS:] = h_ref[t:t + 1, ROW_WORDS:] + y_hi
    if final_norm:
        o_ref[...] = _rms(o_ref[...], gf_ref[...])


def _combine(gu, gv, xn, gates, h, g_final, final_norm):
    t = xn.shape[0]
    tk = COMBINE_TOKENS
    g_final = g_final.reshape(1, -1)
    return pl.pallas_call(
        functools.partial(_combine_kernel, final_norm=final_norm),
        grid=(t // tk,),
        in_specs=[pl.BlockSpec((tk * PEER_SLOTS, ROW_WORDS), lambda i: (i, 0)),
                  pl.BlockSpec((tk * PEER_SLOTS, ROW_WORDS), lambda i: (i, 0)),
                  pl.BlockSpec((tk, D_MODEL), lambda i: (i, 0)),
                  pl.BlockSpec((None, PEER_SLOTS, tk), lambda i: (i, 0, 0)),
                  pl.BlockSpec((tk, D_MODEL), lambda i: (i, 0)),
                  pl.BlockSpec((1, D_MODEL), lambda i: (0, 0))],
        out_specs=pl.BlockSpec((tk, D_MODEL), lambda i: (i, 0)),
        out_shape=jax.ShapeDtypeStruct((t, D_MODEL), jnp.float32),
        compiler_params=_params("parallel"),
        name="peer_combine",
    )(gu, gv, xn, gates, h, g_final)


def _peer(xn, eid, gates, h, tab_u, tab_v, g_final, final_norm):
    n = xn.shape[0]
    tc = min(GATHER_TOKENS, n)
    gpc = tc // COMBINE_TOKENS
    outs = []
    for c in range(n // tc):
        tok = slice(c * tc, (c + 1) * tc)
        gu, gv = _sc_gather(tab_u, tab_v, eid[tok].reshape(tc * PEER_SLOTS))
        outs.append(_combine(gu, gv, xn[tok], gates[c * gpc:(c + 1) * gpc], h[tok], g_final, final_norm))
    return jnp.concatenate(outs, axis=0)


_N_SUB = HG_CHUNK // HG_SUB
_OFF_PAIRS = [(i, j) for i in range(_N_SUB) for j in range(i)]


def _cum_matrix():
    t = jnp.arange(HG_CHUNK)[:, None]
    r = jnp.arange(HG_CHUNK)[None, :]
    sub = t // HG_SUB
    incl = r <= t
    before = r < sub * HG_SUB
    end = r < (sub + 1) * HG_SUB
    return jnp.concatenate([incl, before, end], axis=0).astype(jnp.float32)


def _hgrn_kernel(h_ref, g_ref, w_ref, lb_ref, on_ref, cum_ref, o_ref,
                 z_sc, st_sc, b_sc, k_sc, q_sc, a_sc, lb_sc):
    @pl.when(pl.program_id(1) == 0)
    def _():
        st_sc[...] = jnp.zeros(st_sc.shape, jnp.float32)

    tb = h_ref.shape[0]
    hn = _rms(h_ref[...], g_ref[...]).astype(jnp.bfloat16)
    z_sc[...] = jnp.dot(hn, w_ref[...], preferred_element_type=jnp.float32)
    lbr = lb_ref[...]
    mx = jnp.max(lbr, axis=0, keepdims=True)
    ex = jnp.exp(lbr - mx)
    prob = ex / jnp.sum(ex, axis=0, keepdims=True)
    lb_sc[...] = jnp.broadcast_to((prob[0:1, :] + prob[1:2, :]) - prob[0:1, :], lb_sc.shape)
    wf = HG_HEADS * HG_DK
    sub_row = lax.broadcasted_iota(jnp.int32, (HG_SUB, HG_DK), 0)
    lane64 = lax.broadcasted_iota(jnp.int32, (HG_SUB, HG_CHUNK), 1)

    def head_chunk(idx, carry):
        hd = idx // (tb // HG_CHUNK)
        ch = idx % (tb // HG_CHUNK)
        r0 = pl.multiple_of(ch * HG_CHUNK, HG_CHUNK)
        c0 = pl.multiple_of(hd * HG_DK, HG_DK)
        rows = pl.ds(r0, HG_CHUNK)
        qp = z_sc[rows, pl.ds(c0, HG_DK)]
        fp = z_sc[rows, pl.ds(pl.multiple_of(wf + c0, HG_DK), HG_DK)]
        v = z_sc[rows, pl.ds(pl.multiple_of(2 * wf + c0, HG_DK), HG_DK)]
        gp = z_sc[rows, pl.ds(pl.multiple_of(2 * wf + HG_HEADS * HG_DV + c0, HG_DK), HG_DK)]
        lb = lb_sc[0:1, pl.ds(c0, HG_DK)]
        f = lb + (1.0 - lb) * jax.nn.sigmoid(fp)
        lf = jnp.log(f)
        k = 1.0 - f
        q = qp * jax.nn.sigmoid(qp)
        cums = jnp.dot(cum_ref[...], lf, precision=lax.Precision.HIGHEST, preferred_element_type=jnp.float32)
        b = cums[:HG_CHUNK]
        b_start = cums[HG_CHUNK:2 * HG_CHUNK]
        b_end = cums[2 * HG_CHUNK:]
        q_hat = q * jnp.exp(b - b_start)
        k_hat = k * jnp.exp(b_end - b)
        b_sc[...] = b
        k_sc[...] = k
        q_sc[...] = q
        st = st_sc[hd]
        o_inter = lax.dot_general((q_hat * jnp.exp(b_start)).astype(jnp.bfloat16), st.astype(jnp.bfloat16),
                                  (((1,), (1,)), ((), ())), preferred_element_type=jnp.float32)
        stacked = []
        for (i, j) in _OFF_PAIRS:
            d_ij = jnp.exp(b_start[i * HG_SUB:i * HG_SUB + 1, :] - b_end[j * HG_SUB:j * HG_SUB + 1, :])
            stacked.append(q_hat[i * HG_SUB:(i + 1) * HG_SUB, :] * d_ij)
        stacked = jnp.concatenate(stacked, axis=0).astype(jnp.bfloat16)
        off = lax.dot_general(stacked, k_hat.astype(jnp.bfloat16), (((1,), (1,)), ((), ())),
                              preferred_element_type=jnp.float32)
        for i in range(_N_SUB):
            blk = jnp.zeros((HG_SUB, HG_CHUNK), jnp.float32)
            for p, (pi, pj) in enumerate(_OFF_PAIRS):
                if pi == i:
                    in_j = (lane64 >= pj * HG_SUB) & (lane64 < (pj + 1) * HG_SUB)
                    blk = jnp.where(in_j, off[p * HG_SUB:(p + 1) * HG_SUB, :], blk)
            a_sc[i * HG_SUB:(i + 1) * HG_SUB, :] = blk

        def diag(n, c2):
            i = n // HG_SUB
            s = n % HG_SUB
            blk_rows = pl.ds(pl.multiple_of(i * HG_SUB, HG_SUB), HG_SUB)
            b_blk = b_sc[blk_rows, :]
            q_blk = q_sc[blk_rows, :]
            b_s = b_sc[pl.ds(n, 1), :]
            k_s = k_sc[pl.ds(n, 1), :]
            causal = sub_row >= s
            e = jnp.exp(jnp.where(causal, b_blk - b_s, 0.0))
            col = jnp.sum(jnp.where(causal, q_blk * k_s * e, 0.0), axis=1, keepdims=True)
            a_sc[blk_rows, :] = jnp.where(lane64 == n, col, a_sc[blk_rows, :])
            return c2

        lax.fori_loop(0, HG_CHUNK, diag, 0)
        o = o_inter + jnp.dot(a_sc[...].astype(jnp.bfloat16), v.astype(jnp.bfloat16),
                              preferred_element_type=jnp.float32)
        b_last = b[HG_CHUNK - 1:HG_CHUNK, :]
        k_til = (k_hat * jnp.exp(b_last - b_end)).astype(jnp.bfloat16)
        upd = lax.dot_general(v.astype(jnp.bfloat16), k_til, (((0,), (0,)), ((), ())),
                              preferred_element_type=jnp.float32)
        st_sc[hd] = st * jnp.exp(b_last) + upd
        o = o * lax.rsqrt(jnp.mean(o * o, axis=-1, keepdims=True) + NORM_EPS)
        o = o * on_ref[0:1, pl.ds(c0, HG_DK)] * (gp * jax.nn.sigmoid(gp))
        o_ref[rows, pl.ds(c0, HG_DK)] = o.astype(o_ref.dtype)
        return carry

    lax.fori_loop(0, HG_HEADS * (tb // HG_CHUNK), head_chunk, 0)


def _hgrn(h, g, w, lb_raw, out_norm, batch):
    n = h.shape[0]
    seq = n // batch
    tb = min(HG_BLOCK, seq)
    nb = seq // tb
    g = g.reshape(1, -1)
    out_norm = out_norm.reshape(1, -1)
    cum = _cum_matrix()
    full = lambda a: pl.BlockSpec(a.shape, lambda b, i: (0,) * a.ndim)
    f32 = jnp.float32
    return pl.pallas_call(
        _hgrn_kernel,
        grid=(batch, nb),
        in_specs=[pl.BlockSpec((tb, D_MODEL), lambda b, i: (b * nb + i, 0)), full(g), full(w),
                  full(lb_raw), full(out_norm), full(cum)],
        out_specs=pl.BlockSpec((tb, HG_HEADS * HG_DV), lambda b, i: (b * nb + i, 0)),
        out_shape=jax.ShapeDtypeStruct((n, HG_HEADS * HG_DV), jnp.bfloat16),
        scratch_shapes=[pltpu.VMEM((tb, w.shape[1]), f32),
                        pltpu.VMEM((HG_HEADS, HG_DV, HG_DK), f32),
                        pltpu.VMEM((HG_CHUNK, HG_DK), f32), pltpu.VMEM((HG_CHUNK, HG_DK), f32),
                        pltpu.VMEM((HG_CHUNK, HG_DK), f32), pltpu.VMEM((HG_CHUNK, HG_CHUNK), f32),
                        pltpu.VMEM((8, HG_HEADS * HG_DK), f32)],
        compiler_params=_params("arbitrary", "arbitrary"),
        name="hgrn2",
    )(h, g, w, lb_raw, out_norm, cum)


def kernel(x, positions, ln_mix, ln_ffn, ln_final, mla_w_in, mla_q_norm, mla_w_q_up, mla_kv_norm,
           mla_w_kv_up, mla_w_out, hg_w_in, hg_lb, hg_out_norm, hg_w_out, peer_w_q, peer_sub_keys,
           peer_u, peer_v):
    batch, seq, d = x.shape
    n = batch * seq
    bf16 = jnp.bfloat16
    h = x.reshape(n, d)

    def route_weights(i):
        keys = peer_sub_keys[i].reshape(2 * PEER_HEADS, PEER_NKEYS, PEER_HALF).astype(bf16)
        return peer_w_q[i].T.astype(bf16), keys

    tables = [(_pack_table(peer_u[i]), _pack_table(peer_v[i])) for i in range(2)]

    cos_t, sin_t = _rope_tables(positions)
    win, wq, wkv = _mla_weights(mla_w_in[0], mla_w_q_up[0], mla_w_kv_up[0])
    q, k, v = _mla_proj(h, ln_mix[0], cos_t, sin_t, win, mla_q_norm[0], wq, mla_kv_norm[0], wkv)
    o = _attention(q, k, v, batch)
    wqt, keys = route_weights(0)
    h, xn, eid, gates = _route(o, h, mla_w_out[0].astype(bf16), ln_ffn[0], wqt, keys)
    h = _peer(xn, eid, gates, h, tables[0][0], tables[0][1], ln_final, False)

    o = _hgrn(h, ln_mix[1], hg_w_in[0].astype(bf16), hg_lb, hg_out_norm[0], batch)
    wqt, keys = route_weights(1)
    h, xn, eid, gates = _route(o, h, hg_w_out[0].astype(bf16), ln_ffn[1], wqt, keys)
    out = _peer(xn, eid, gates, h, tables[1][0], tables[1][1], ln_final, True)
    return out.reshape(batch, seq, d)
```

```python
import functools

import jax
import jax.numpy as jnp
from jax import lax
from jax.experimental import pallas as pl
from jax.experimental.pallas import tpu as pltpu
from jax.experimental.pallas import tpu_sc as plsc

D_MODEL = 1024
NORM_EPS = 1e-6
MLA_HEADS = 8
MLA_Q_LORA = 384
MLA_KV_LORA = 256
MLA_NOPE = 128
MLA_ROPE = 64
MLA_V = 128
ROPE_THETA = 10000.0
HG_HEADS = 8
HG_DK = 128
HG_DV = 128
PEER_HEADS = 8
PEER_NKEYS = 128
PEER_HALF = 128
PEER_TOPK = 16
PEER_SLOTS = PEER_HEADS * PEER_TOPK

LANES = 128
SC_CORES = 2
SC_SUBCORES = 16
VMEM_LIMIT = 48 * 1024 * 1024

ROW_WORDS = D_MODEL // 2
TOKEN_BLOCK = 256
ATTN_BLOCK = 1024
HG_BLOCK = 256
HG_CHUNK = 64
HG_SUB = 16
COMBINE_TOKENS = 16
GATHER_WINDOW = 32
GATHER_SLOTS = 3
GATHER_TOKENS = 2048
NEG_INF = float("-inf")
MASKED_LOG_DECAY = -1e30


def _rms(x, g):
    return x * lax.rsqrt(jnp.mean(x * x, axis=-1, keepdims=True) + NORM_EPS) * g


def _params(*sem):
    return pltpu.CompilerParams(dimension_semantics=sem, vmem_limit_bytes=VMEM_LIMIT)


def _pack_kernel(t_ref, o_ref):
    t = t_ref[...]
    lo = pltpu.bitcast(t[:, :ROW_WORDS].astype(jnp.bfloat16).astype(jnp.float32), jnp.uint32)
    hi = pltpu.bitcast(t[:, ROW_WORDS:].astype(jnp.bfloat16).astype(jnp.float32), jnp.uint32)
    o_ref[...] = (lo >> 16) | (hi & jnp.uint32(0xFFFF0000))


def _pack_table(tab):
    e, d = tab.shape
    rows = 512
    return pl.pallas_call(
        _pack_kernel,
        grid=(e // rows,),
        in_specs=[pl.BlockSpec((rows, d), lambda i: (i, 0))],
        out_specs=pl.BlockSpec((rows, d // 2), lambda i: (i, 0)),
        out_shape=jax.ShapeDtypeStruct((e, d // 2), jnp.uint32),
        compiler_params=_params("parallel"),
        name="pack_table",
    )(tab)


def _unpack_words(w):
    lo = pltpu.bitcast(w << 16, jnp.float32)
    hi = pltpu.bitcast(w & jnp.uint32(0xFFFF0000), jnp.float32)
    return lo, hi


def _rope_kernel(pos_ref, inv_ref, sign_ref, c_ref, s_ref):
    ang = pos_ref[...].astype(jnp.float32) * inv_ref[...]
    c_ref[...] = jnp.cos(ang)
    s_ref[...] = jnp.sin(ang) * sign_ref[...]


def _rope_tables(positions):
    n = positions.size
    lane = jnp.arange(LANES)
    inv = ROPE_THETA ** (-(2 * (lane % (MLA_ROPE // 2))).astype(jnp.float32) / MLA_ROPE)
    sign = jnp.where((lane % MLA_ROPE) < MLA_ROPE // 2, -1.0, 1.0).astype(jnp.float32)
    tb = 1024
    out = jax.ShapeDtypeStruct((n, LANES), jnp.float32)
    return pl.pallas_call(
        _rope_kernel,
        grid=(n // tb,),
        in_specs=[pl.BlockSpec((tb, 1), lambda i: (i, 0)),
                  pl.BlockSpec((1, LANES), lambda i: (0, 0)),
                  pl.BlockSpec((1, LANES), lambda i: (0, 0))],
        out_specs=[pl.BlockSpec((tb, LANES), lambda i: (i, 0))] * 2,
        out_shape=[out, out],
        compiler_params=_params("parallel"),
        name="rope_tables",
    )(positions.reshape(n, 1), inv.reshape(1, LANES), sign.reshape(1, LANES))


def _mla_proj_kernel(x_ref, g_ref, c_ref, s_ref, win_ref, qn_ref, wq_ref, kvn_ref, wkv_ref,
                     q_out, k_out, v_out):
    hn = _rms(x_ref[...], g_ref[...]).astype(jnp.bfloat16)
    z = jnp.dot(hn, win_ref[...], preferred_element_type=jnp.float32)
    c = c_ref[...]
    s = s_ref[...]
    o_kv = MLA_Q_LORA
    o_kr = MLA_Q_LORA + MLA_KV_LORA
    k_rope = (z[:, o_kr:o_kr + LANES] * c + z[:, o_kr + LANES:o_kr + 2 * LANES] * s).astype(jnp.bfloat16)
    cq = _rms(z[:, :MLA_Q_LORA], qn_ref[...]).astype(jnp.bfloat16)
    q = jnp.dot(cq, wq_ref[...], preferred_element_type=jnp.float32)
    ckv = _rms(z[:, o_kv:o_kr], kvn_ref[...]).astype(jnp.bfloat16)
    kv = jnp.dot(ckv, wkv_ref[...], preferred_element_type=jnp.float32)
    scale = (MLA_NOPE + MLA_ROPE) ** -0.5
    hw = MLA_HEADS * LANES
    for h in range(MLA_HEADS):
        sl = slice(h * LANES, (h + 1) * LANES)
        q_out[h, :, :LANES] = (q[:, sl] * scale).astype(jnp.bfloat16)
        qr = q[:, hw + h * LANES:hw + (h + 1) * LANES] * c + q[:, 2 * hw + h * LANES:2 * hw + (h + 1) * LANES] * s
        q_out[h, :, LANES:] = (qr * scale).astype(jnp.bfloat16)
        k_out[h, :, :LANES] = kv[:, sl].astype(jnp.bfloat16)
        k_out[h, :, LANES:] = k_rope
        v_out[h] = kv[:, hw + h * LANES:hw + (h + 1) * LANES].astype(jnp.bfloat16)


def _swap_halves(w):
    half = w.shape[-1] // 2
    return jnp.concatenate([w[..., half:], w[..., :half]], axis=-1)


def _pad_lanes(w):
    return jnp.pad(w, [(0, 0)] * (w.ndim - 1) + [(0, LANES - w.shape[-1])])


def _mla_weights(w_in, w_q_up, w_kv_up):
    o_kr = MLA_Q_LORA + MLA_KV_LORA
    w_kr = w_in[:, o_kr:]
    win = jnp.concatenate([w_in[:, :o_kr], _pad_lanes(w_kr), _pad_lanes(_swap_halves(w_kr))], axis=1)
    wq = w_q_up.reshape(MLA_Q_LORA, MLA_HEADS, MLA_NOPE + MLA_ROPE)
    wq_n = wq[:, :, :MLA_NOPE].reshape(MLA_Q_LORA, -1)
    wq_r = _pad_lanes(wq[:, :, MLA_NOPE:]).reshape(MLA_Q_LORA, -1)
    wq_rs = _pad_lanes(_swap_halves(wq[:, :, MLA_NOPE:])).reshape(MLA_Q_LORA, -1)
    wqp = jnp.concatenate([wq_n, wq_r, wq_rs], axis=1)
    wkv = w_kv_up.reshape(MLA_KV_LORA, MLA_HEADS, MLA_NOPE + MLA_V)
    wkvp = jnp.concatenate([wkv[:, :, :MLA_NOPE].reshape(MLA_KV_LORA, -1),
                            wkv[:, :, MLA_NOPE:].reshape(MLA_KV_LORA, -1)], axis=1)
    return win.astype(jnp.bfloat16), wqp.astype(jnp.bfloat16), wkvp.astype(jnp.bfloat16)


def _mla_proj(h, g, cos_t, sin_t, win, q_norm, wq, kv_norm, wkv):
    n = h.shape[0]
    tb = TOKEN_BLOCK
    full = lambda a: pl.BlockSpec(a.shape, lambda i: (0,) * a.ndim)
    g = g.reshape(1, -1)
    q_norm = q_norm.reshape(1, -1)
    kv_norm = kv_norm.reshape(1, -1)
    qk_t = jax.ShapeDtypeStruct((MLA_HEADS, n, 2 * LANES), jnp.bfloat16)
    v_t = jax.ShapeDtypeStruct((MLA_HEADS, n, MLA_V), jnp.bfloat16)
    return pl.pallas_call(
        _mla_proj_kernel,
        grid=(n // tb,),
        in_specs=[pl.BlockSpec((tb, D_MODEL), lambda i: (i, 0)), full(g),
                  pl.BlockSpec((tb, LANES), lambda i: (i, 0)), pl.BlockSpec((tb, LANES), lambda i: (i, 0)),
                  full(win), full(q_norm), full(wq), full(kv_norm), full(wkv)],
        out_specs=[pl.BlockSpec((MLA_HEADS, tb, 2 * LANES), lambda i: (0, i, 0)),
                   pl.BlockSpec((MLA_HEADS, tb, 2 * LANES), lambda i: (0, i, 0)),
                   pl.BlockSpec((MLA_HEADS, tb, MLA_V), lambda i: (0, i, 0))],
        out_shape=[qk_t, qk_t, v_t],
        compiler_params=_params("parallel"),
        name="mla_proj",
    )(h, g, cos_t, sin_t, win, q_norm, wq, kv_norm, wkv)


def _attn_kernel(qi_ref, ki_ref, q_ref, k_ref, v_ref, o_ref, m_sc, l_sc, acc_sc):
    step = pl.program_id(2)
    qi = qi_ref[step]
    ki = ki_ref[step]

    @pl.when(ki == 0)
    def _():
        m_sc[...] = jnp.full(m_sc.shape, NEG_INF, jnp.float32)
        l_sc[...] = jnp.zeros(l_sc.shape, jnp.float32)
        acc_sc[...] = jnp.zeros(acc_sc.shape, jnp.float32)

    def update(masked):
        sc = lax.dot_general(q_ref[...], k_ref[...], (((1,), (1,)), ((), ())),
                             preferred_element_type=jnp.float32)
        if masked:
            row = lax.broadcasted_iota(jnp.int32, sc.shape, 0)
            col = lax.broadcasted_iota(jnp.int32, sc.shape, 1)
            sc = jnp.where(col <= row, sc, NEG_INF)
        m_prev = m_sc[...]
        m_next = jnp.maximum(m_prev, jnp.max(sc, axis=1, keepdims=True))
        p = jnp.exp(sc - m_next[:, :1])
        alpha = jnp.exp(m_prev - m_next)
        l_sc[...] = alpha * l_sc[...] + jnp.sum(p, axis=1, keepdims=True)
        acc_sc[...] = alpha * acc_sc[...] + jnp.dot(p.astype(jnp.bfloat16), v_ref[...],
                                                    preferred_element_type=jnp.float32)
        m_sc[...] = m_next

    @pl.when(ki < qi)
    def _():
        update(False)

    @pl.when(ki == qi)
    def _():
        update(True)
        o_ref[...] = (acc_sc[...] / l_sc[...]).astype(o_ref.dtype)


def _attention(q, k, v, batch):
    n = q.shape[1]
    seq = n // batch
    blk = min(ATTN_BLOCK, seq)
    nb = seq // blk
    qi = jnp.array([i for i in range(nb) for _ in range(i + 1)], jnp.int32)
    ki = jnp.array([j for i in range(nb) for j in range(i + 1)], jnp.int32)
    grid_spec = pltpu.PrefetchScalarGridSpec(
        num_scalar_prefetch=2,
        grid=(batch, MLA_HEADS, qi.shape[0]),
        in_specs=[pl.BlockSpec((None, blk, 2 * LANES), lambda b, h, s, qi, ki: (h, b * nb + qi[s], 0)),
                  pl.BlockSpec((None, blk, 2 * LANES), lambda b, h, s, qi, ki: (h, b * nb + ki[s], 0)),
                  pl.BlockSpec((None, blk, MLA_V), lambda b, h, s, qi, ki: (h, b * nb + ki[s], 0))],
        out_specs=pl.BlockSpec((blk, MLA_V), lambda b, h, s, qi, ki: (b * nb + qi[s], h)),
        scratch_shapes=[pltpu.VMEM((blk, LANES), jnp.float32), pltpu.VMEM((blk, LANES), jnp.float32),
                        pltpu.VMEM((blk, MLA_V), jnp.float32)],
    )
    return pl.pallas_call(
        _attn_kernel,
        grid_spec=grid_spec,
        out_shape=jax.ShapeDtypeStruct((n, MLA_HEADS * MLA_V), jnp.bfloat16),
        compiler_params=_params("parallel", "parallel", "arbitrary"),
        name="mla_attention",
    )(qi, ki, q, k, v)


def _top_rows(vals, ids, count, out_rows):
    t = vals.shape[1]
    big = jnp.int32(2 ** 30)
    orow = lax.broadcasted_iota(jnp.int32, (out_rows, t), 0)

    def body(r, carry):
        cur, ov, oi = carry
        m = jnp.max(cur, axis=0, keepdims=True)
        pick = jnp.min(jnp.where(cur == m, ids, big), axis=0, keepdims=True)
        cur = jnp.where(ids == pick, NEG_INF, cur)
        ov = jnp.where(orow == r, m, ov)
        oi = jnp.where(orow == r, pick, oi)
        return cur, ov, oi

    init = (vals, jnp.zeros((out_rows, t), jnp.float32), jnp.zeros((out_rows, t), jnp.int32))
    _, ov, oi = lax.fori_loop(0, count, body, init)
    return ov, oi


_ROW_SLABS = [(0, 0, 16), (1, 0, 8)] + [(a, 0, 8) for a in range(2, 8)]
_COL_SLAB = (8, 16, 0)
_PAIR_ROWS = sum(hi - lo for _, lo, hi in _ROW_SLABS) + (_COL_SLAB[1] - _COL_SLAB[0])


def _route_kernel(o_ref, h_ref, wo_ref, g_ref, wqt_ref, keys_ref, pos_ref,
                  hn_out, xn_out, eid_out, gate_out, qt_sc, v_sc, i_sc):
    tb = h_ref.shape[0]
    hnew = h_ref[...] + jnp.dot(o_ref[...], wo_ref[...], preferred_element_type=jnp.float32)
    hn_out[...] = hnew
    xn = _rms(hnew, g_ref[...])
    xn_out[...] = xn
    qt_sc[...] = lax.dot_general(wqt_ref[...], xn.astype(jnp.bfloat16), (((1,), (1,)), ((), ())),
                                 preferred_element_type=jnp.float32).astype(jnp.bfloat16)
    key_ids = lax.broadcasted_iota(jnp.int32, (PEER_NKEYS, tb), 0)

    def group(g, carry):
        row0 = pl.multiple_of(g * PEER_HALF, PEER_HALF)
        st = jnp.dot(keys_ref[g], qt_sc[pl.ds(row0, PEER_HALF), :], preferred_element_type=jnp.float32)
        tv, ti = _top_rows(st, key_ids, PEER_TOPK, PEER_TOPK)
        out0 = pl.multiple_of(g * PEER_TOPK, PEER_TOPK)
        v_sc[pl.ds(out0, PEER_TOPK), :] = tv
        i_sc[pl.ds(out0, PEER_TOPK), :] = ti
        return carry

    lax.fori_loop(0, 2 * PEER_HEADS, group, 0)

    pos = pos_ref[...]

    def head(hd, carry):
        base = pl.multiple_of(hd * 2 * PEER_TOPK, 2 * PEER_TOPK)
        v1 = v_sc[pl.ds(base, PEER_TOPK), :]
        i1 = i_sc[pl.ds(base, PEER_TOPK), :]
        v2 = v_sc[pl.ds(base + PEER_TOPK, PEER_TOPK), :]
        i2 = i_sc[pl.ds(base + PEER_TOPK, PEER_TOPK), :]
        cv, ce = [], []
        for a, lo, hi in _ROW_SLABS:
            cv.append(v1[a:a + 1, :] + v2[lo:hi, :])
            ce.append(i1[a:a + 1, :] * PEER_NKEYS + i2[lo:hi, :])
        a_lo, a_hi, b = _COL_SLAB
        cv.append(v1[a_lo:a_hi, :] + v2[b:b + 1, :])
        ce.append(i1[a_lo:a_hi, :] * PEER_NKEYS + i2[b:b + 1, :])
        cv = jnp.concatenate(cv, axis=0)
        ce = jnp.concatenate(ce, axis=0)
        tv, tp = _top_rows(cv, jnp.broadcast_to(pos, cv.shape), PEER_TOPK, PEER_TOPK)
        te = jnp.zeros((PEER_TOPK, tb), jnp.int32)
        orow = lax.broadcasted_iota(jnp.int32, (PEER_TOPK, tb), 0)
        for r in range(PEER_TOPK):
            e_r = jnp.sum(jnp.where(pos == tp[r:r + 1, :], ce, 0), axis=0, keepdims=True)
            te = jnp.where(orow == r, e_r, te)
        ex = jnp.exp(tv - tv[0:1, :])
        gates = ex / jnp.sum(ex, axis=0, keepdims=True)
        out0 = pl.multiple_of(hd * PEER_TOPK, PEER_TOPK)
        v_sc[pl.ds(out0, PEER_TOPK), :] = gates
        i_sc[pl.ds(out0, PEER_TOPK), :] = te
        return carry

    lax.fori_loop(0, PEER_HEADS, head, 0)

    eid_out[...] = jnp.transpose(i_sc[:PEER_SLOTS, :].astype(jnp.float32)).astype(jnp.int32)
    gates_all = v_sc[:PEER_SLOTS, :]
    for c in range(tb // COMBINE_TOKENS):
        gate_out[c] = gates_all[:, c * COMBINE_TOKENS:(c + 1) * COMBINE_TOKENS]


def _pair_positions():
    pos = [a * PEER_TOPK + b for a, lo, hi in _ROW_SLABS for b in range(lo, hi)]
    a_lo, a_hi, b = _COL_SLAB
    pos += [a * PEER_TOPK + b for a in range(a_lo, a_hi)]
    return jnp.array(pos, jnp.int32).reshape(_PAIR_ROWS, 1)


def _route(o, h, w_out, g, wqt, keys):
    n = h.shape[0]
    tb = TOKEN_BLOCK
    full = lambda a: pl.BlockSpec(a.shape, lambda i: (0,) * a.ndim)
    g = g.reshape(1, -1)
    pos = _pair_positions()
    row = pl.BlockSpec((tb, D_MODEL), lambda i: (i, 0))
    f32 = jnp.float32
    return pl.pallas_call(
        _route_kernel,
        grid=(n // tb,),
        in_specs=[row, row, full(w_out), full(g), full(wqt), full(keys), full(pos)],
        out_specs=[row, row, pl.BlockSpec((tb, PEER_SLOTS), lambda i: (i, 0)),
                   pl.BlockSpec((tb // COMBINE_TOKENS, PEER_SLOTS, COMBINE_TOKENS), lambda i: (i, 0, 0))],
        out_shape=[jax.ShapeDtypeStruct((n, D_MODEL), f32), jax.ShapeDtypeStruct((n, D_MODEL), f32),
                   jax.ShapeDtypeStruct((n, PEER_SLOTS), jnp.int32),
                   jax.ShapeDtypeStruct((n // COMBINE_TOKENS, PEER_SLOTS, COMBINE_TOKENS), f32)],
        scratch_shapes=[pltpu.VMEM((2 * PEER_HEADS * PEER_HALF, tb), jnp.bfloat16),
                        pltpu.VMEM((2 * PEER_HEADS * PEER_TOPK, tb), f32),
                        pltpu.VMEM((2 * PEER_HEADS * PEER_TOPK, tb), jnp.int32)],
        compiler_params=_params("parallel"),
        name="peer_route",
    )(o, h, w_out, g, wqt, keys, pos)


def _sc_gather(tab_u, tab_v, idx):
    p = idx.shape[0]
    w = GATHER_WINDOW
    slots = GATHER_SLOTS
    per_worker = p // (SC_CORES * SC_SUBCORES)
    n_win = per_worker // w
    mesh = plsc.VectorSubcoreMesh(core_axis_name="c", subcore_axis_name="s")
    out_t = jax.ShapeDtypeStruct((p, ROW_WORDS), jnp.uint32)

    @functools.partial(
        pl.kernel, out_type=(out_t, out_t), mesh=mesh,
        scratch_types=[pltpu.VMEM((per_worker,), jnp.int32),
                       pltpu.VMEM((slots, w, ROW_WORDS), jnp.uint32),
                       pltpu.VMEM((slots, w, ROW_WORDS), jnp.uint32),
                       pltpu.SemaphoreType.DMA((slots,)), pltpu.SemaphoreType.DMA((slots,))],
        name="peer_gather")
    def gather(u_hbm, v_hbm, i_hbm, ou_hbm, ov_hbm, idx_v, rows_u, rows_v, sem_g, sem_w):
        wid = lax.axis_index("s") * SC_CORES + lax.axis_index("c")
        base = pl.multiple_of(wid * per_worker, per_worker)
        pltpu.sync_copy(i_hbm.at[pl.ds(base, per_worker)], idx_v)

        def gather_copies(win, slot):
            ix = idx_v.at[pl.ds(pl.multiple_of(win * w, w), w)]
            return (pltpu.make_async_copy(u_hbm.at[ix], rows_u.at[slot], sem_g.at[slot]),
                    pltpu.make_async_copy(v_hbm.at[ix], rows_v.at[slot], sem_g.at[slot]))

        def write_copies(win, slot):
            dst = pl.ds(pl.multiple_of(base + win * w, w), w)
            return (pltpu.make_async_copy(rows_u.at[slot], ou_hbm.at[dst], sem_w.at[slot]),
                    pltpu.make_async_copy(rows_v.at[slot], ov_hbm.at[dst], sem_w.at[slot]))

        for s in range(slots - 1):
            for c in gather_copies(s, s):
                c.start()

        @pl.loop(0, n_win, step=slots)
        def _(w0):
            for s in range(slots):
                win = w0 + s
                prev = (s - 1) % slots

                @pl.when(win < n_win)
                def _():
                    @pl.when(win >= 1)
                    def _():
                        for c in write_copies(win - 1, prev):
                            c.wait()

                    @pl.when(win + slots - 1 < n_win)
                    def _():
                        for c in gather_copies(win + slots - 1, prev):
                            c.start()

                    for c in gather_copies(win, s):
                        c.wait()
                    for c in write_copies(win, s):
                        c.start()

        for c in write_copies(n_win - 1, (n_win - 1) % slots):
            c.wait()

    return gather(tab_u, tab_v, idx)


def _combine_kernel(gu_ref, gv_ref, x_ref, gate_ref, h_ref, gf_ref, o_ref, *, final_norm):
    tk = x_ref.shape[0]
    for t in range(tk):
        rows = slice(t * PEER_SLOTS, (t + 1) * PEER_SLOTS)
        ulo, uhi = _unpack_words(gu_ref[rows, :])
        x = x_ref[t:t + 1, :]
        act = jnp.sum(ulo * x[:, :ROW_WORDS] + uhi * x[:, ROW_WORDS:], axis=1, keepdims=True)
        gelu = 0.5 * act * (1.0 + lax.erf(act * (2.0 ** -0.5)))
        w = gate_ref[:, t:t + 1] * gelu
        vlo, vhi = _unpack_words(gv_ref[rows, :])
        y_lo = jnp.sum(vlo * w, axis=0, keepdims=True)
        y_hi = jnp.sum(vhi * w, axis=0, keepdims=True)
        o_ref[t:t + 1, :ROW_WORDS] = h_ref[t:t + 1, :ROW_WORDS] + y_lo
        o_ref[t:t + 1, ROW_WORDS:] = h_ref[t:t + 1, ROW_WORDS:] + y_hi
    if final_norm:
        o_ref[...] = _rms(o_ref[...], gf_ref[...])


def _combine(gu, gv, xn, gates, h, g_final, final_norm):
    t = xn.shape[0]
    tk = COMBINE_TOKENS
    g_final = g_final.reshape(1, -1)
    return pl.pallas_call(
        functools.partial(_combine_kernel, final_norm=final_norm),
        grid=(t // tk,),
        in_specs=[pl.BlockSpec((tk * PEER_SLOTS, ROW_WORDS), lambda i: (i, 0)),
                  pl.BlockSpec((tk * PEER_SLOTS, ROW_WORDS), lambda i: (i, 0)),
                  pl.BlockSpec((tk, D_MODEL), lambda i: (i, 0)),
                  pl.BlockSpec((None, PEER_SLOTS, tk), lambda i: (i, 0, 0)),
                  pl.BlockSpec((tk, D_MODEL), lambda i: (i, 0)),
                  pl.BlockSpec((1, D_MODEL), lambda i: (0, 0))],
        out_specs=pl.BlockSpec((tk, D_MODEL), lambda i: (i, 0)),
        out_shape=jax.ShapeDtypeStruct((t, D_MODEL), jnp.float32),
        compiler_params=_params("parallel"),
        name="peer_combine",
    )(gu, gv, xn, gates, h, g_final)


def _peer(xn, eid, gates, h, tab_u, tab_v, g_final, final_norm):
    n = xn.shape[0]
    tc = min(GATHER_TOKENS, n)
    gpc = tc // COMBINE_TOKENS
    outs = []
    for c in range(n // tc):
        tok = slice(c * tc, (c + 1) * tc)
        gu, gv = _sc_gather(tab_u, tab_v, eid[tok].reshape(tc * PEER_SLOTS))
        outs.append(_combine(gu, gv, xn[tok], gates[c * gpc:(c + 1) * gpc], h[tok], g_final, final_norm))
    return jnp.concatenate(outs, axis=0)


_N_SUB = HG_CHUNK // HG_SUB
_OFF_PAIRS = [(i, j) for i in range(_N_SUB) for j in range(i)]


def _cum_matrix():
    t = jnp.arange(HG_CHUNK)[:, None]
    r = jnp.arange(HG_CHUNK)[None, :]
    sub = t // HG_SUB
    incl = r <= t
    before = r < sub * HG_SUB
    end = r < (sub + 1) * HG_SUB
    return jnp.concatenate([incl, before, end], axis=0).astype(jnp.float32)


def _hgrn_kernel(h_ref, g_ref, w_ref, lb_ref, on_ref, cum_ref, o_ref, z_sc, st_sc, lb_sc):
    @pl.when(pl.program_id(1) == 0)
    def _():
        st_sc[...] = jnp.zeros(st_sc.shape, jnp.float32)

    tb = h_ref.shape[0]
    hn = _rms(h_ref[...], g_ref[...]).astype(jnp.bfloat16)
    z_sc[...] = jnp.dot(hn, w_ref[...], preferred_element_type=jnp.float32)
    lbr = lb_ref[...]
    mx = jnp.max(lbr, axis=0, keepdims=True)
    ex = jnp.exp(lbr - mx)
    prob = ex / jnp.sum(ex, axis=0, keepdims=True)
    lb_sc[...] = jnp.broadcast_to((prob[0:1, :] + prob[1:2, :]) - prob[0:1, :], lb_sc.shape)
    wf = HG_HEADS * HG_DK
    sub_row = lax.broadcasted_iota(jnp.int32, (HG_SUB, HG_DK), 0)
    lane64 = lax.broadcasted_iota(jnp.int32, (HG_SUB, HG_CHUNK), 1)

    mxu = jnp.bfloat16
    nt = (((1,), (1,)), ((), ()))

    def head(hd, carry):
        c0 = pl.multiple_of(hd * HG_DK, HG_DK)
        lb = lb_sc[0:1, pl.ds(c0, HG_DK)]
        on = on_ref[0:1, pl.ds(c0, HG_DK)]
        st = st_sc[hd]
        for ch in range(tb // HG_CHUNK):
            rows = slice(ch * HG_CHUNK, (ch + 1) * HG_CHUNK)
            qp = z_sc[rows, pl.ds(c0, HG_DK)]
            fp = z_sc[rows, pl.ds(pl.multiple_of(wf + c0, HG_DK), HG_DK)]
            v = z_sc[rows, pl.ds(pl.multiple_of(2 * wf + c0, HG_DK), HG_DK)]
            gp = z_sc[rows, pl.ds(pl.multiple_of(2 * wf + HG_HEADS * HG_DV + c0, HG_DK), HG_DK)]
            f = lb + (1.0 - lb) * jax.nn.sigmoid(fp)
            lf = jnp.log(f)
            k = 1.0 - f
            q = qp * jax.nn.sigmoid(qp)
            cums = jnp.dot(cum_ref[...], lf, precision=lax.Precision.HIGHEST,
                           preferred_element_type=jnp.float32)
            b = cums[:HG_CHUNK]
            b_start = cums[HG_CHUNK:2 * HG_CHUNK]
            b_end = cums[2 * HG_CHUNK:]
            q_hat = q * jnp.exp(b - b_start)
            k_hat = k * jnp.exp(b_end - b)
            o_inter = lax.dot_general((q_hat * jnp.exp(b_start)).astype(mxu), st.astype(mxu), nt,
                                      preferred_element_type=jnp.float32)
            stacked = []
            for (i, j) in _OFF_PAIRS:
                d_ij = jnp.exp(b_start[i * HG_SUB:i * HG_SUB + 1, :] - b_end[j * HG_SUB:j * HG_SUB + 1, :])
                stacked.append(q_hat[i * HG_SUB:(i + 1) * HG_SUB, :] * d_ij)
            stacked = jnp.concatenate(stacked, axis=0).astype(mxu)
            off = lax.dot_general(stacked, k_hat.astype(mxu), nt,
                                  preferred_element_type=jnp.float32)
            a_rows = []
            for i in range(_N_SUB):
                blk = jnp.zeros((HG_SUB, HG_CHUNK), jnp.float32)
                for p, (pi, pj) in enumerate(_OFF_PAIRS):
                    if pi == i:
                        in_j = (lane64 >= pj * HG_SUB) & (lane64 < (pj + 1) * HG_SUB)
                        blk = jnp.where(in_j, off[p * HG_SUB:(p + 1) * HG_SUB, :], blk)
                b_blk = b[i * HG_SUB:(i + 1) * HG_SUB, :]
                q_blk = q[i * HG_SUB:(i + 1) * HG_SUB, :]
                for s in range(HG_SUB):
                    n = i * HG_SUB + s
                    e = jnp.exp(jnp.where(sub_row >= s, b_blk - b[n:n + 1, :], MASKED_LOG_DECAY))
                    col = jnp.sum(q_blk * k[n:n + 1, :] * e, axis=1, keepdims=True)
                    blk = jnp.where(lane64 == n, col, blk)
                a_rows.append(blk)
            a = jnp.concatenate(a_rows, axis=0).astype(mxu)
            o = o_inter + jnp.dot(a, v.astype(mxu), preferred_element_type=jnp.float32)
            b_last = b[HG_CHUNK - 1:HG_CHUNK, :]
            k_til = (k_hat * jnp.exp(b_last - b_end)).astype(mxu)
            upd = lax.dot_general(v.astype(mxu), k_til, (((0,), (0,)), ((), ())),
                                  preferred_element_type=jnp.float32)
            st = st * jnp.exp(b_last) + upd
            o = o * lax.rsqrt(jnp.mean(o * o, axis=-1, keepdims=True) + NORM_EPS)
            o = o * on * (gp * jax.nn.sigmoid(gp))
            o_ref[rows, pl.ds(c0, HG_DK)] = o.astype(o_ref.dtype)
        st_sc[hd] = st
        return carry

    lax.fori_loop(0, HG_HEADS, head, 0)


def _hgrn(h, g, w, lb_raw, out_norm, batch):
    n = h.shape[0]
    seq = n // batch
    tb = min(HG_BLOCK, seq)
    nb = seq // tb
    g = g.reshape(1, -1)
    out_norm = out_norm.reshape(1, -1)
    cum = _cum_matrix()
    full = lambda a: pl.BlockSpec(a.shape, lambda b, i: (0,) * a.ndim)
    f32 = jnp.float32
    return pl.pallas_call(
        _hgrn_kernel,
        grid=(batch, nb),
        in_specs=[pl.BlockSpec((tb, D_MODEL), lambda b, i: (b * nb + i, 0)), full(g), full(w),
                  full(lb_raw), full(out_norm), full(cum)],
        out_specs=pl.BlockSpec((tb, HG_HEADS * HG_DV), lambda b, i: (b * nb + i, 0)),
        out_shape=jax.ShapeDtypeStruct((n, HG_HEADS * HG_DV), jnp.bfloat16),
        scratch_shapes=[pltpu.VMEM((tb, w.shape[1]), f32),
                        pltpu.VMEM((HG_HEADS, HG_DV, HG_DK), f32),
                        pltpu.VMEM((8, HG_HEADS * HG_DK), f32)],
        compiler_params=_params("arbitrary", "arbitrary"),
        name="hgrn2",
    )(h, g, w, lb_raw, out_norm, cum)


def kernel(x, positions, ln_mix, ln_ffn, ln_final, mla_w_in, mla_q_norm, mla_w_q_up, mla_kv_norm,
           mla_w_kv_up, mla_w_out, hg_w_in, hg_lb, hg_out_norm, hg_w_out, peer_w_q, peer_sub_keys,
           peer_u, peer_v):
    batch, seq, d = x.shape
    n = batch * seq
    bf16 = jnp.bfloat16
    h = x.reshape(n, d)

    def route_weights(i):
        keys = peer_sub_keys[i].reshape(2 * PEER_HEADS, PEER_NKEYS, PEER_HALF).astype(bf16)
        return peer_w_q[i].T.astype(bf16), keys

    tables = [(_pack_table(peer_u[i]), _pack_table(peer_v[i])) for i in range(2)]

    cos_t, sin_t = _rope_tables(positions)
    win, wq, wkv = _mla_weights(mla_w_in[0], mla_w_q_up[0], mla_w_kv_up[0])
    q, k, v = _mla_proj(h, ln_mix[0], cos_t, sin_t, win, mla_q_norm[0], wq, mla_kv_norm[0], wkv)
    o = _attention(q, k, v, batch)
    wqt, keys = route_weights(0)
    h, xn, eid, gates = _route(o, h, mla_w_out[0].astype(bf16), ln_ffn[0], wqt, keys)
    h = _peer(xn, eid, gates, h, tables[0][0], tables[0][1], ln_final, False)

    o = _hgrn(h, ln_mix[1], hg_w_in[0].astype(bf16), hg_lb, hg_out_norm[0], batch)
    wqt, keys = route_weights(1)
    h, xn, eid, gates = _route(o, h, hg_w_out[0].astype(bf16), ln_ffn[1], wqt, keys)
    out = _peer(xn, eid, gates, h, tables[1][0], tables[1][1], ln_final, True)
    return out.reshape(batch, seq, d)
```

```python
import functools

import jax
import jax.numpy as jnp
from jax import lax
from jax.experimental import pallas as pl
from jax.experimental.pallas import tpu as pltpu
from jax.experimental.pallas import tpu_sc as plsc

D_MODEL = 1024
NORM_EPS = 1e-6
MLA_HEADS = 8
MLA_Q_LORA = 384
MLA_KV_LORA = 256
MLA_NOPE = 128
MLA_ROPE = 64
MLA_V = 128
ROPE_THETA = 10000.0
HG_HEADS = 8
HG_DK = 128
HG_DV = 128
PEER_HEADS = 8
PEER_NKEYS = 128
PEER_HALF = 128
PEER_TOPK = 16
PEER_SLOTS = PEER_HEADS * PEER_TOPK

LANES = 128
SC_CORES = 2
SC_SUBCORES = 16
VMEM_LIMIT = 48 * 1024 * 1024

ROW_WORDS = D_MODEL // 2
TOKEN_BLOCK = 256
ATTN_BLOCK = 1024
HG_BLOCK = 256
HG_CHUNK = 64
HG_SUB = 16
GATHER_WINDOW = 32
GATHER_SLOTS = 3
GATHER_TOKENS = 1024
SC_LANES = 16
DOT_ROWS = 4
AXPY_CHUNKS = 8
NEG_INF = float("-inf")
MASKED_LOG_DECAY = -1e30


def _rms(x, g):
    return x * lax.rsqrt(jnp.mean(x * x, axis=-1, keepdims=True) + NORM_EPS) * g


def _params(*sem):
    return pltpu.CompilerParams(dimension_semantics=sem, vmem_limit_bytes=VMEM_LIMIT)


def _pack_kernel(t_ref, o_ref):
    t = t_ref[...]
    lo = pltpu.bitcast(t[:, :ROW_WORDS].astype(jnp.bfloat16).astype(jnp.float32), jnp.uint32)
    hi = pltpu.bitcast(t[:, ROW_WORDS:].astype(jnp.bfloat16).astype(jnp.float32), jnp.uint32)
    o_ref[...] = (lo >> 16) | (hi & jnp.uint32(0xFFFF0000))


def _pack_table(tab):
    e, d = tab.shape
    rows = 512
    return pl.pallas_call(
        _pack_kernel,
        grid=(e // rows,),
        in_specs=[pl.BlockSpec((rows, d), lambda i: (i, 0))],
        out_specs=pl.BlockSpec((rows, d // 2), lambda i: (i, 0)),
        out_shape=jax.ShapeDtypeStruct((e, d // 2), jnp.uint32),
        compiler_params=_params("parallel"),
        name="pack_table",
    )(tab)


def _rope_kernel(pos_ref, inv_ref, sign_ref, c_ref, s_ref):
    ang = pos_ref[...].astype(jnp.float32) * inv_ref[...]
    c_ref[...] = jnp.cos(ang)
    s_ref[...] = jnp.sin(ang) * sign_ref[...]


def _rope_tables(positions):
    n = positions.size
    lane = jnp.arange(LANES)
    inv = ROPE_THETA ** (-(2 * (lane % (MLA_ROPE // 2))).astype(jnp.float32) / MLA_ROPE)
    sign = jnp.where((lane % MLA_ROPE) < MLA_ROPE // 2, -1.0, 1.0).astype(jnp.float32)
    tb = 1024
    out = jax.ShapeDtypeStruct((n, LANES), jnp.float32)
    return pl.pallas_call(
        _rope_kernel,
        grid=(n // tb,),
        in_specs=[pl.BlockSpec((tb, 1), lambda i: (i, 0)),
                  pl.BlockSpec((1, LANES), lambda i: (0, 0)),
                  pl.BlockSpec((1, LANES), lambda i: (0, 0))],
        out_specs=[pl.BlockSpec((tb, LANES), lambda i: (i, 0))] * 2,
        out_shape=[out, out],
        compiler_params=_params("parallel"),
        name="rope_tables",
    )(positions.reshape(n, 1), inv.reshape(1, LANES), sign.reshape(1, LANES))


def _mla_proj_kernel(x_ref, g_ref, c_ref, s_ref, win_ref, qn_ref, wq_ref, kvn_ref, wkv_ref,
                     q_out, k_out, v_out):
    hn = _rms(x_ref[...], g_ref[...]).astype(jnp.bfloat16)
    z = jnp.dot(hn, win_ref[...], preferred_element_type=jnp.float32)
    c = c_ref[...]
    s = s_ref[...]
    o_kv = MLA_Q_LORA
    o_kr = MLA_Q_LORA + MLA_KV_LORA
    k_rope = (z[:, o_kr:o_kr + LANES] * c + z[:, o_kr + LANES:o_kr + 2 * LANES] * s).astype(jnp.bfloat16)
    cq = _rms(z[:, :MLA_Q_LORA], qn_ref[...]).astype(jnp.bfloat16)
    q = jnp.dot(cq, wq_ref[...], preferred_element_type=jnp.float32)
    ckv = _rms(z[:, o_kv:o_kr], kvn_ref[...]).astype(jnp.bfloat16)
    kv = jnp.dot(ckv, wkv_ref[...], preferred_element_type=jnp.float32)
    scale = (MLA_NOPE + MLA_ROPE) ** -0.5
    hw = MLA_HEADS * LANES
    for h in range(MLA_HEADS):
        sl = slice(h * LANES, (h + 1) * LANES)
        q_out[h, :, :LANES] = (q[:, sl] * scale).astype(jnp.bfloat16)
        qr = q[:, hw + h * LANES:hw + (h + 1) * LANES] * c + q[:, 2 * hw + h * LANES:2 * hw + (h + 1) * LANES] * s
        q_out[h, :, LANES:] = (qr * scale).astype(jnp.bfloat16)
        k_out[h, :, :LANES] = kv[:, sl].astype(jnp.bfloat16)
        k_out[h, :, LANES:] = k_rope
        v_out[h] = kv[:, hw + h * LANES:hw + (h + 1) * LANES].astype(jnp.bfloat16)


def _swap_halves(w):
    half = w.shape[-1] // 2
    return jnp.concatenate([w[..., half:], w[..., :half]], axis=-1)


def _pad_lanes(w):
    return jnp.pad(w, [(0, 0)] * (w.ndim - 1) + [(0, LANES - w.shape[-1])])


def _mla_weights(w_in, w_q_up, w_kv_up):
    o_kr = MLA_Q_LORA + MLA_KV_LORA
    w_kr = w_in[:, o_kr:]
    win = jnp.concatenate([w_in[:, :o_kr], _pad_lanes(w_kr), _pad_lanes(_swap_halves(w_kr))], axis=1)
    wq = w_q_up.reshape(MLA_Q_LORA, MLA_HEADS, MLA_NOPE + MLA_ROPE)
    wq_n = wq[:, :, :MLA_NOPE].reshape(MLA_Q_LORA, -1)
    wq_r = _pad_lanes(wq[:, :, MLA_NOPE:]).reshape(MLA_Q_LORA, -1)
    wq_rs = _pad_lanes(_swap_halves(wq[:, :, MLA_NOPE:])).reshape(MLA_Q_LORA, -1)
    wqp = jnp.concatenate([wq_n, wq_r, wq_rs], axis=1)
    wkv = w_kv_up.reshape(MLA_KV_LORA, MLA_HEADS, MLA_NOPE + MLA_V)
    wkvp = jnp.concatenate([wkv[:, :, :MLA_NOPE].reshape(MLA_KV_LORA, -1),
                            wkv[:, :, MLA_NOPE:].reshape(MLA_KV_LORA, -1)], axis=1)
    return win.astype(jnp.bfloat16), wqp.astype(jnp.bfloat16), wkvp.astype(jnp.bfloat16)


def _mla_proj(h, g, cos_t, sin_t, win, q_norm, wq, kv_norm, wkv):
    n = h.shape[0]
    tb = TOKEN_BLOCK
    full = lambda a: pl.BlockSpec(a.shape, lambda i: (0,) * a.ndim)
    g = g.reshape(1, -1)
    q_norm = q_norm.reshape(1, -1)
    kv_norm = kv_norm.reshape(1, -1)
    qk_t = jax.ShapeDtypeStruct((MLA_HEADS, n, 2 * LANES), jnp.bfloat16)
    v_t = jax.ShapeDtypeStruct((MLA_HEADS, n, MLA_V), jnp.bfloat16)
    return pl.pallas_call(
        _mla_proj_kernel,
        grid=(n // tb,),
        in_specs=[pl.BlockSpec((tb, D_MODEL), lambda i: (i, 0)), full(g),
                  pl.BlockSpec((tb, LANES), lambda i: (i, 0)), pl.BlockSpec((tb, LANES), lambda i: (i, 0)),
                  full(win), full(q_norm), full(wq), full(kv_norm), full(wkv)],
        out_specs=[pl.BlockSpec((MLA_HEADS, tb, 2 * LANES), lambda i: (0, i, 0)),
                   pl.BlockSpec((MLA_HEADS, tb, 2 * LANES), lambda i: (0, i, 0)),
                   pl.BlockSpec((MLA_HEADS, tb, MLA_V), lambda i: (0, i, 0))],
        out_shape=[qk_t, qk_t, v_t],
        compiler_params=_params("parallel"),
        name="mla_proj",
    )(h, g, cos_t, sin_t, win, q_norm, wq, kv_norm, wkv)


def _attn_kernel(qi_ref, ki_ref, q_ref, k_ref, v_ref, o_ref, m_sc, l_sc, acc_sc):
    step = pl.program_id(2)
    qi = qi_ref[step]
    ki = ki_ref[step]

    @pl.when(ki == 0)
    def _():
        m_sc[...] = jnp.full(m_sc.shape, NEG_INF, jnp.float32)
        l_sc[...] = jnp.zeros(l_sc.shape, jnp.float32)
        acc_sc[...] = jnp.zeros(acc_sc.shape, jnp.float32)

    def update(masked):
        sc = lax.dot_general(q_ref[...], k_ref[...], (((1,), (1,)), ((), ())),
                             preferred_element_type=jnp.float32)
        if masked:
            row = lax.broadcasted_iota(jnp.int32, sc.shape, 0)
            col = lax.broadcasted_iota(jnp.int32, sc.shape, 1)
            sc = jnp.where(col <= row, sc, NEG_INF)
        m_prev = m_sc[...]
        m_next = jnp.maximum(m_prev, jnp.max(sc, axis=1, keepdims=True))
        p = jnp.exp(sc - m_next[:, :1])
        alpha = jnp.exp(m_prev - m_next)
        l_sc[...] = alpha * l_sc[...] + jnp.sum(p, axis=1, keepdims=True)
        acc_sc[...] = alpha * acc_sc[...] + jnp.dot(p.astype(jnp.bfloat16), v_ref[...],
                                                    preferred_element_type=jnp.float32)
        m_sc[...] = m_next

    @pl.when(ki < qi)
    def _():
        update(False)

    @pl.when(ki == qi)
    def _():
        update(True)
        o_ref[...] = (acc_sc[...] / l_sc[...]).astype(o_ref.dtype)


def _attention(q, k, v, batch):
    n = q.shape[1]
    seq = n // batch
    blk = min(ATTN_BLOCK, seq)
    nb = seq // blk
    qi = jnp.array([i for i in range(nb) for _ in range(i + 1)], jnp.int32)
    ki = jnp.array([j for i in range(nb) for j in range(i + 1)], jnp.int32)
    grid_spec = pltpu.PrefetchScalarGridSpec(
        num_scalar_prefetch=2,
        grid=(batch, MLA_HEADS, qi.shape[0]),
        in_specs=[pl.BlockSpec((None, blk, 2 * LANES), lambda b, h, s, qi, ki: (h, b * nb + qi[s], 0)),
                  pl.BlockSpec((None, blk, 2 * LANES), lambda b, h, s, qi, ki: (h, b * nb + ki[s], 0)),
                  pl.BlockSpec((None, blk, MLA_V), lambda b, h, s, qi, ki: (h, b * nb + ki[s], 0))],
        out_specs=pl.BlockSpec((blk, MLA_V), lambda b, h, s, qi, ki: (b * nb + qi[s], h)),
        scratch_shapes=[pltpu.VMEM((blk, LANES), jnp.float32), pltpu.VMEM((blk, LANES), jnp.float32),
                        pltpu.VMEM((blk, MLA_V), jnp.float32)],
    )
    return pl.pallas_call(
        _attn_kernel,
        grid_spec=grid_spec,
        out_shape=jax.ShapeDtypeStruct((n, MLA_HEADS * MLA_V), jnp.bfloat16),
        compiler_params=_params("parallel", "parallel", "arbitrary"),
        name="mla_attention",
    )(qi, ki, q, k, v)


def _top_rows(vals, ids, count, out_rows):
    t = vals.shape[1]
    big = jnp.int32(2 ** 30)
    orow = lax.broadcasted_iota(jnp.int32, (out_rows, t), 0)

    def body(r, carry):
        cur, ov, oi = carry
        m = jnp.max(cur, axis=0, keepdims=True)
        pick = jnp.min(jnp.where(cur == m, ids, big), axis=0, keepdims=True)
        cur = jnp.where(ids == pick, NEG_INF, cur)
        ov = jnp.where(orow == r, m, ov)
        oi = jnp.where(orow == r, pick, oi)
        return cur, ov, oi

    init = (vals, jnp.zeros((out_rows, t), jnp.float32), jnp.zeros((out_rows, t), jnp.int32))
    _, ov, oi = lax.fori_loop(0, count, body, init)
    return ov, oi


_ROW_SLABS = [(0, 0, 16), (1, 0, 8)] + [(a, 0, 8) for a in range(2, 8)]
_COL_SLAB = (8, 16, 0)
_PAIR_ROWS = sum(hi - lo for _, lo, hi in _ROW_SLABS) + (_COL_SLAB[1] - _COL_SLAB[0])


def _route_kernel(o_ref, h_ref, wo_ref, g_ref, wqt_ref, keys_ref, pos_ref,
                  hn_out, xn_out, eid_out, gate_out, qt_sc, v_sc, i_sc):
    tb = h_ref.shape[0]
    hnew = h_ref[...] + jnp.dot(o_ref[...], wo_ref[...], preferred_element_type=jnp.float32)
    hn_out[...] = hnew
    xn = _rms(hnew, g_ref[...])
    xn_out[...] = xn
    qt_sc[...] = lax.dot_general(wqt_ref[...], xn.astype(jnp.bfloat16), (((1,), (1,)), ((), ())),
                                 preferred_element_type=jnp.float32).astype(jnp.bfloat16)
    key_ids = lax.broadcasted_iota(jnp.int32, (PEER_NKEYS, tb), 0)

    def group(g, carry):
        row0 = pl.multiple_of(g * PEER_HALF, PEER_HALF)
        st = jnp.dot(keys_ref[g], qt_sc[pl.ds(row0, PEER_HALF), :], preferred_element_type=jnp.float32)
        tv, ti = _top_rows(st, key_ids, PEER_TOPK, PEER_TOPK)
        out0 = pl.multiple_of(g * PEER_TOPK, PEER_TOPK)
        v_sc[pl.ds(out0, PEER_TOPK), :] = tv
        i_sc[pl.ds(out0, PEER_TOPK), :] = ti
        return carry

    lax.fori_loop(0, 2 * PEER_HEADS, group, 0)

    pos = pos_ref[...]

    def head(hd, carry):
        base = pl.multiple_of(hd * 2 * PEER_TOPK, 2 * PEER_TOPK)
        v1 = v_sc[pl.ds(base, PEER_TOPK), :]
        i1 = i_sc[pl.ds(base, PEER_TOPK), :]
        v2 = v_sc[pl.ds(base + PEER_TOPK, PEER_TOPK), :]
        i2 = i_sc[pl.ds(base + PEER_TOPK, PEER_TOPK), :]
        cv, ce = [], []
        for a, lo, hi in _ROW_SLABS:
            cv.append(v1[a:a + 1, :] + v2[lo:hi, :])
            ce.append(i1[a:a + 1, :] * PEER_NKEYS + i2[lo:hi, :])
        a_lo, a_hi, b = _COL_SLAB
        cv.append(v1[a_lo:a_hi, :] + v2[b:b + 1, :])
        ce.append(i1[a_lo:a_hi, :] * PEER_NKEYS + i2[b:b + 1, :])
        cv = jnp.concatenate(cv, axis=0)
        ce = jnp.concatenate(ce, axis=0)
        tv, tp = _top_rows(cv, jnp.broadcast_to(pos, cv.shape), PEER_TOPK, PEER_TOPK)
        te = jnp.zeros((PEER_TOPK, tb), jnp.int32)
        orow = lax.broadcasted_iota(jnp.int32, (PEER_TOPK, tb), 0)
        for r in range(PEER_TOPK):
            e_r = jnp.sum(jnp.where(pos == tp[r:r + 1, :], ce, 0), axis=0, keepdims=True)
            te = jnp.where(orow == r, e_r, te)
        ex = jnp.exp(tv - tv[0:1, :])
        gates = ex / jnp.sum(ex, axis=0, keepdims=True)
        out0 = pl.multiple_of(hd * PEER_TOPK, PEER_TOPK)
        v_sc[pl.ds(out0, PEER_TOPK), :] = gates
        i_sc[pl.ds(out0, PEER_TOPK), :] = te
        return carry

    lax.fori_loop(0, PEER_HEADS, head, 0)

    eid_out[...] = jnp.transpose(i_sc[:PEER_SLOTS, :].astype(jnp.float32)).astype(jnp.int32)
    gate_out[...] = jnp.transpose(v_sc[:PEER_SLOTS, :])


def _pair_positions():
    pos = [a * PEER_TOPK + b for a, lo, hi in _ROW_SLABS for b in range(lo, hi)]
    a_lo, a_hi, b = _COL_SLAB
    pos += [a * PEER_TOPK + b for a in range(a_lo, a_hi)]
    return jnp.array(pos, jnp.int32).reshape(_PAIR_ROWS, 1)


def _route(o, h, w_out, g, wqt, keys):
    n = h.shape[0]
    tb = TOKEN_BLOCK
    full = lambda a: pl.BlockSpec(a.shape, lambda i: (0,) * a.ndim)
    g = g.reshape(1, -1)
    pos = _pair_positions()
    row = pl.BlockSpec((tb, D_MODEL), lambda i: (i, 0))
    f32 = jnp.float32
    return pl.pallas_call(
        _route_kernel,
        grid=(n // tb,),
        in_specs=[row, row, full(w_out), full(g), full(wqt), full(keys), full(pos)],
        out_specs=[row, row, pl.BlockSpec((tb, PEER_SLOTS), lambda i: (i, 0)),
                   pl.BlockSpec((tb, PEER_SLOTS), lambda i: (i, 0))],
        out_shape=[jax.ShapeDtypeStruct((n, D_MODEL), f32), jax.ShapeDtypeStruct((n, D_MODEL), f32),
                   jax.ShapeDtypeStruct((n, PEER_SLOTS), jnp.int32),
                   jax.ShapeDtypeStruct((n, PEER_SLOTS), f32)],
        scratch_shapes=[pltpu.VMEM((2 * PEER_HEADS * PEER_HALF, tb), jnp.bfloat16),
                        pltpu.VMEM((2 * PEER_HEADS * PEER_TOPK, tb), f32),
                        pltpu.VMEM((2 * PEER_HEADS * PEER_TOPK, tb), jnp.int32)],
        compiler_params=_params("parallel"),
        name="peer_route",
    )(o, h, w_out, g, wqt, keys, pos)


def _sc_mesh():
    return plsc.VectorSubcoreMesh(core_axis_name="c", subcore_axis_name="s")


def _sc_params():
    return pltpu.CompilerParams(needs_layout_passes=False)


def _worker_id():
    return lax.axis_index("s") * SC_CORES + lax.axis_index("c")


def _gather_rows(tab_hbm, idx_ref, dst_ref, sem):
    return pltpu.make_async_copy(tab_hbm.at[idx_ref], dst_ref, sem)


def _unpack16(word):
    lo = lax.bitcast_convert_type(word << 16, jnp.float32)
    hi = lax.bitcast_convert_type(word & jnp.uint32(0xFFFF0000), jnp.float32)
    return lo, hi


def _sc_dots(tab, idx, x):
    t = x.shape[0]
    workers = SC_CORES * SC_SUBCORES
    tok_w = t // workers
    per_worker = tok_w * PEER_SLOTS
    w = GATHER_WINDOW
    slots = GATHER_SLOTS
    n_win = per_worker // w
    win_per_tok = PEER_SLOTS // w
    n_chunks = ROW_WORDS // SC_LANES
    f32 = jnp.float32

    @functools.partial(
        pl.kernel, out_type=jax.ShapeDtypeStruct((t * PEER_SLOTS,), f32), mesh=_sc_mesh(),
        scratch_types=[pltpu.VMEM((per_worker,), jnp.int32),
                       pltpu.VMEM((tok_w, D_MODEL), f32),
                       pltpu.VMEM((slots, w, ROW_WORDS), jnp.uint32),
                       pltpu.VMEM((per_worker,), f32),
                       pltpu.SemaphoreType.DMA((slots,))],
        compiler_params=_sc_params(), name="peer_dots")
    def dots(tab_hbm, i_hbm, x_hbm, act_hbm, idx_v, x_v, rows, act_v, sem):
        wid = _worker_id()
        base = pl.multiple_of(wid * per_worker, per_worker)
        tok0 = pl.multiple_of(wid * tok_w, tok_w)
        pltpu.sync_copy(i_hbm.at[pl.ds(base, per_worker)], idx_v)
        pltpu.sync_copy(x_hbm.at[pl.ds(tok0, tok_w)], x_v)

        def gather(win, slot):
            ix = idx_v.at[pl.ds(pl.multiple_of(win * w, w), w)]
            return _gather_rows(tab_hbm, ix, rows.at[slot], sem.at[slot])

        for s in range(slots - 1):
            gather(s, s).start()
        lane = lax.iota(jnp.int32, SC_LANES)

        @pl.loop(0, n_win)
        def _(win):
            slot = lax.rem(win, slots)
            nxt = win + slots - 1

            @pl.when(nxt < n_win)
            def _():
                gather(nxt, lax.rem(nxt, slots)).start()

            gather(win, slot).wait()
            tok = win // win_per_tok
            for part in range(w // SC_LANES):

                def group(g, res):
                    r0 = part * SC_LANES + g * DOT_ROWS
                    acc = [jnp.zeros((SC_LANES,), f32) for _ in range(DOT_ROWS)]
                    for c in range(n_chunks):
                        x_lo = x_v[tok, pl.ds(c * SC_LANES, SC_LANES)]
                        x_hi = x_v[tok, pl.ds(ROW_WORDS + c * SC_LANES, SC_LANES)]
                        for r in range(DOT_ROWS):
                            lo, hi = _unpack16(rows[slot, r0 + r, pl.ds(c * SC_LANES, SC_LANES)])
                            acc[r] = acc[r] + lo * x_lo + hi * x_hi
                    for r in range(DOT_ROWS):
                        res = jnp.where(lane == g * DOT_ROWS + r, jnp.sum(acc[r]), res)
                    return res

                res = lax.fori_loop(0, SC_LANES // DOT_ROWS, group, jnp.zeros((SC_LANES,), f32))
                act_v[pl.ds(pl.multiple_of(win * w + part * SC_LANES, SC_LANES), SC_LANES)] = res

        pltpu.sync_copy(act_v, act_hbm.at[pl.ds(base, per_worker)])

    return dots(tab, idx, x)


def _sc_axpy(tab, idx, wts):
    p = idx.shape[0]
    t = p // PEER_SLOTS
    workers = SC_CORES * SC_SUBCORES
    tok_w = t // workers
    per_worker = tok_w * PEER_SLOTS
    w = GATHER_WINDOW
    slots = GATHER_SLOTS
    n_win = per_worker // w
    win_per_tok = PEER_SLOTS // w
    passes = ROW_WORDS // (SC_LANES * AXPY_CHUNKS)
    f32 = jnp.float32

    @functools.partial(
        pl.kernel, out_type=jax.ShapeDtypeStruct((t, D_MODEL), f32), mesh=_sc_mesh(),
        scratch_types=[pltpu.VMEM((per_worker,), jnp.int32),
                       pltpu.VMEM((per_worker,), f32),
                       pltpu.VMEM((slots, w, ROW_WORDS), jnp.uint32),
                       pltpu.VMEM((2, D_MODEL), f32),
                       pltpu.SemaphoreType.DMA((slots,)),
                       pltpu.SemaphoreType.DMA((2,))],
        compiler_params=_sc_params(), name="peer_axpy")
    def axpy(tab_hbm, i_hbm, w_hbm, y_hbm, idx_v, w_v, rows, y_v, sem, sem_y):
        wid = _worker_id()
        base = pl.multiple_of(wid * per_worker, per_worker)
        tok0 = wid * tok_w
        pltpu.sync_copy(i_hbm.at[pl.ds(base, per_worker)], idx_v)
        pltpu.sync_copy(w_hbm.at[pl.ds(base, per_worker)], w_v)

        def gather(win, slot):
            ix = idx_v.at[pl.ds(pl.multiple_of(win * w, w), w)]
            return _gather_rows(tab_hbm, ix, rows.at[slot], sem.at[slot])

        def y_write(tok, buf):
            return pltpu.make_async_copy(y_v.at[buf], y_hbm.at[tok0 + tok], sem_y.at[buf])

        for s in range(slots - 1):
            gather(s, s).start()

        @pl.loop(0, n_win)
        def _(win):
            slot = lax.rem(win, slots)
            nxt = win + slots - 1

            @pl.when(nxt < n_win)
            def _():
                gather(nxt, lax.rem(nxt, slots)).start()

            gather(win, slot).wait()
            tok = win // win_per_tok
            part = lax.rem(win, win_per_tok)
            buf = lax.rem(tok, 2)

            @pl.when(part == 0)
            def _():
                @pl.when(tok >= 2)
                def _():
                    y_write(tok - 2, buf).wait()

                for c in range(D_MODEL // SC_LANES):
                    y_v[buf, pl.ds(c * SC_LANES, SC_LANES)] = jnp.zeros((SC_LANES,), f32)

            for ps in range(passes):

                def group(g, accs):
                    accs = list(accs)
                    row0 = pl.multiple_of(g * SC_LANES, SC_LANES)
                    w_grp = w_v[pl.ds(pl.multiple_of(win * w + row0, SC_LANES), SC_LANES)]
                    for k in range(SC_LANES):
                        wj = jnp.take_along_axis(w_grp, jnp.full((SC_LANES,), k, jnp.int32), axis=0)
                        for c in range(AXPY_CHUNKS):
                            col = (ps * AXPY_CHUNKS + c) * SC_LANES
                            lo, hi = _unpack16(rows[slot, row0 + k, pl.ds(col, SC_LANES)])
                            accs[2 * c] = accs[2 * c] + lo * wj
                            accs[2 * c + 1] = accs[2 * c + 1] + hi * wj
                    return tuple(accs)

                zero = tuple(jnp.zeros((SC_LANES,), f32) for _ in range(2 * AXPY_CHUNKS))
                accs = lax.fori_loop(0, w // SC_LANES, group, zero)
                for c in range(AXPY_CHUNKS):
                    col = (ps * AXPY_CHUNKS + c) * SC_LANES
                    y_v[buf, pl.ds(col, SC_LANES)] = y_v[buf, pl.ds(col, SC_LANES)] + accs[2 * c]
                    y_v[buf, pl.ds(ROW_WORDS + col, SC_LANES)] = (
                        y_v[buf, pl.ds(ROW_WORDS + col, SC_LANES)] + accs[2 * c + 1])

            @pl.when(part == win_per_tok - 1)
            def _():
                y_write(tok, buf).start()

        for tok in (tok_w - 2, tok_w - 1):
            y_write(tok, tok % 2).wait()

    return axpy(tab, idx, wts)


def _slot_weight_kernel(act_ref, gate_ref, o_ref):
    act = act_ref[...]
    o_ref[...] = gate_ref[...] * (0.5 * act * (1.0 + lax.erf(act * (2.0 ** -0.5))))


def _slot_weights(act, gates):
    n = gates.shape[0]
    tb = 1024
    blk = pl.BlockSpec((tb, PEER_SLOTS), lambda i: (i, 0))
    return pl.pallas_call(
        _slot_weight_kernel, grid=(n // tb,), in_specs=[blk, blk], out_specs=blk,
        out_shape=jax.ShapeDtypeStruct((n, PEER_SLOTS), jnp.float32),
        compiler_params=_params("parallel"), name="peer_slot_weights",
    )(act, gates)


def _residual_kernel(h_ref, y_ref, g_ref, o_ref, *, final_norm):
    out = h_ref[...] + y_ref[...]
    o_ref[...] = _rms(out, g_ref[...]) if final_norm else out


def _residual(h, y, g_final, final_norm):
    n = h.shape[0]
    tb = 512
    blk = pl.BlockSpec((tb, D_MODEL), lambda i: (i, 0))
    g_final = g_final.reshape(1, -1)
    return pl.pallas_call(
        functools.partial(_residual_kernel, final_norm=final_norm),
        grid=(n // tb,), in_specs=[blk, blk, pl.BlockSpec((1, D_MODEL), lambda i: (0, 0))], out_specs=blk,
        out_shape=jax.ShapeDtypeStruct((n, D_MODEL), jnp.float32),
        compiler_params=_params("parallel"), name="peer_residual",
    )(h, y, g_final)


def _peer(xn, eid, gates, h, tab_u, tab_v, g_final, final_norm):
    n = xn.shape[0]
    tc = min(GATHER_TOKENS, n)
    chunks = [slice(c * tc, (c + 1) * tc) for c in range(n // tc)]
    act = jnp.concatenate([_sc_dots(tab_u, eid[tok].reshape(-1), xn[tok]) for tok in chunks])
    wts = _slot_weights(act.reshape(n, PEER_SLOTS), gates)
    y = jnp.concatenate([_sc_axpy(tab_v, eid[tok].reshape(-1), wts[tok].reshape(-1)) for tok in chunks])
    return _residual(h, y, g_final, final_norm)


_N_SUB = HG_CHUNK // HG_SUB
_OFF_PAIRS = [(i, j) for i in range(_N_SUB) for j in range(i)]


def _cum_matrix():
    t = jnp.arange(HG_CHUNK)[:, None]
    r = jnp.arange(HG_CHUNK)[None, :]
    sub = t // HG_SUB
    incl = r <= t
    before = r < sub * HG_SUB
    end = r < (sub + 1) * HG_SUB
    return jnp.concatenate([incl, before, end], axis=0).astype(jnp.float32)


def _hgrn_kernel(h_ref, g_ref, w_ref, lb_ref, on_ref, cum_ref, o_ref, z_sc, st_sc, lb_sc):
    @pl.when(pl.program_id(1) == 0)
    def _():
        st_sc[...] = jnp.zeros(st_sc.shape, jnp.float32)

    tb = h_ref.shape[0]
    hn = _rms(h_ref[...], g_ref[...]).astype(jnp.bfloat16)
    z_sc[...] = jnp.dot(hn, w_ref[...], preferred_element_type=jnp.float32)
    lbr = lb_ref[...]
    mx = jnp.max(lbr, axis=0, keepdims=True)
    ex = jnp.exp(lbr - mx)
    prob = ex / jnp.sum(ex, axis=0, keepdims=True)
    lb_sc[...] = jnp.broadcast_to((prob[0:1, :] + prob[1:2, :]) - prob[0:1, :], lb_sc.shape)
    wf = HG_HEADS * HG_DK
    sub_row = lax.broadcasted_iota(jnp.int32, (HG_SUB, HG_DK), 0)
    lane64 = lax.broadcasted_iota(jnp.int32, (HG_SUB, HG_CHUNK), 1)

    mxu = jnp.bfloat16
    nt = (((1,), (1,)), ((), ()))

    def head(hd, carry):
        c0 = pl.multiple_of(hd * HG_DK, HG_DK)
        lb = lb_sc[0:1, pl.ds(c0, HG_DK)]
        on = on_ref[0:1, pl.ds(c0, HG_DK)]
        st = st_sc[hd]
        for ch in range(tb // HG_CHUNK):
            rows = slice(ch * HG_CHUNK, (ch + 1) * HG_CHUNK)
            qp = z_sc[rows, pl.ds(c0, HG_DK)]
            fp = z_sc[rows, pl.ds(pl.multiple_of(wf + c0, HG_DK), HG_DK)]
            v = z_sc[rows, pl.ds(pl.multiple_of(2 * wf + c0, HG_DK), HG_DK)]
            gp = z_sc[rows, pl.ds(pl.multiple_of(2 * wf + HG_HEADS * HG_DV + c0, HG_DK), HG_DK)]
            f = lb + (1.0 - lb) * jax.nn.sigmoid(fp)
            lf = jnp.log(f)
            k = 1.0 - f
            q = qp * jax.nn.sigmoid(qp)
            cums = jnp.dot(cum_ref[...], lf, precision=lax.Precision.HIGHEST,
                           preferred_element_type=jnp.float32)
            b = cums[:HG_CHUNK]
            b_start = cums[HG_CHUNK:2 * HG_CHUNK]
            b_end = cums[2 * HG_CHUNK:]
            q_hat = q * jnp.exp(b - b_start)
            k_hat = k * jnp.exp(b_end - b)
            o_inter = lax.dot_general((q_hat * jnp.exp(b_start)).astype(mxu), st.astype(mxu), nt,
                                      preferred_element_type=jnp.float32)
            stacked = []
            for (i, j) in _OFF_PAIRS:
                d_ij = jnp.exp(b_start[i * HG_SUB:i * HG_SUB + 1, :] - b_end[j * HG_SUB:j * HG_SUB + 1, :])
                stacked.append(q_hat[i * HG_SUB:(i + 1) * HG_SUB, :] * d_ij)
            stacked = jnp.concatenate(stacked, axis=0).astype(mxu)
            off = lax.dot_general(stacked, k_hat.astype(mxu), nt,
                                  preferred_element_type=jnp.float32)
            a_rows = []
            for i in range(_N_SUB):
                blk = jnp.zeros((HG_SUB, HG_CHUNK), jnp.float32)
                for p, (pi, pj) in enumerate(_OFF_PAIRS):
                    if pi == i:
                        in_j = (lane64 >= pj * HG_SUB) & (lane64 < (pj + 1) * HG_SUB)
                        blk = jnp.where(in_j, off[p * HG_SUB:(p + 1) * HG_SUB, :], blk)
                b_blk = b[i * HG_SUB:(i + 1) * HG_SUB, :]
                q_blk = q[i * HG_SUB:(i + 1) * HG_SUB, :]
                for s in range(HG_SUB):
                    n = i * HG_SUB + s
                    e = jnp.exp(jnp.where(sub_row >= s, b_blk - b[n:n + 1, :], MASKED_LOG_DECAY))
                    col = jnp.sum(q_blk * k[n:n + 1, :] * e, axis=1, keepdims=True)
                    blk = jnp.where(lane64 == n, col, blk)
                a_rows.append(blk)
            a = jnp.concatenate(a_rows, axis=0).astype(mxu)
            o = o_inter + jnp.dot(a, v.astype(mxu), preferred_element_type=jnp.float32)
            b_last = b[HG_CHUNK - 1:HG_CHUNK, :]
            k_til = (k_hat * jnp.exp(b_last - b_end)).astype(mxu)
            upd = lax.dot_general(v.astype(mxu), k_til, (((0,), (0,)), ((), ())),
                                  preferred_element_type=jnp.float32)
            st = st * jnp.exp(b_last) + upd
            o = o * lax.rsqrt(jnp.mean(o * o, axis=-1, keepdims=True) + NORM_EPS)
            o = o * on * (gp * jax.nn.sigmoid(gp))
            o_ref[rows, pl.ds(c0, HG_DK)] = o.astype(o_ref.dtype)
        st_sc[hd] = st
        return carry

    lax.fori_loop(0, HG_HEADS, head, 0)


def _hgrn(h, g, w, lb_raw, out_norm, batch):
    n = h.shape[0]
    seq = n // batch
    tb = min(HG_BLOCK, seq)
    nb = seq // tb
    g = g.reshape(1, -1)
    out_norm = out_norm.reshape(1, -1)
    cum = _cum_matrix()
    full = lambda a: pl.BlockSpec(a.shape, lambda b, i: (0,) * a.ndim)
    f32 = jnp.float32
    return pl.pallas_call(
        _hgrn_kernel,
        grid=(batch, nb),
        in_specs=[pl.BlockSpec((tb, D_MODEL), lambda b, i: (b * nb + i, 0)), full(g), full(w),
                  full(lb_raw), full(out_norm), full(cum)],
        out_specs=pl.BlockSpec((tb, HG_HEADS * HG_DV), lambda b, i: (b * nb + i, 0)),
        out_shape=jax.ShapeDtypeStruct((n, HG_HEADS * HG_DV), jnp.bfloat16),
        scratch_shapes=[pltpu.VMEM((tb, w.shape[1]), f32),
                        pltpu.VMEM((HG_HEADS, HG_DV, HG_DK), f32),
                        pltpu.VMEM((8, HG_HEADS * HG_DK), f32)],
        compiler_params=_params("arbitrary", "arbitrary"),
        name="hgrn2",
    )(h, g, w, lb_raw, out_norm, cum)


def kernel(x, positions, ln_mix, ln_ffn, ln_final, mla_w_in, mla_q_norm, mla_w_q_up, mla_kv_norm,
           mla_w_kv_up, mla_w_out, hg_w_in, hg_lb, hg_out_norm, hg_w_out, peer_w_q, peer_sub_keys,
           peer_u, peer_v):
    batch, seq, d = x.shape
    n = batch * seq
    bf16 = jnp.bfloat16
    h = x.reshape(n, d)

    def route_weights(i):
        keys = peer_sub_keys[i].reshape(2 * PEER_HEADS, PEER_NKEYS, PEER_HALF).astype(bf16)
        return peer_w_q[i].T.astype(bf16), keys

    tables = [(_pack_table(peer_u[i]), _pack_table(peer_v[i])) for i in range(2)]

    cos_t, sin_t = _rope_tables(positions)
    win, wq, wkv = _mla_weights(mla_w_in[0], mla_w_q_up[0], mla_w_kv_up[0])
    q, k, v = _mla_proj(h, ln_mix[0], cos_t, sin_t, win, mla_q_norm[0], wq, mla_kv_norm[0], wkv)
    o = _attention(q, k, v, batch)
    wqt, keys = route_weights(0)
    h, xn, eid, gates = _route(o, h, mla_w_out[0].astype(bf16), ln_ffn[0], wqt, keys)
    h = _peer(xn, eid, gates, h, tables[0][0], tables[0][1], ln_final, False)

    o = _hgrn(h, ln_mix[1], hg_w_in[0].astype(bf16), hg_lb, hg_out_norm[0], batch)
    wqt, keys = route_weights(1)
    h, xn, eid, gates = _route(o, h, hg_w_out[0].astype(bf16), ln_ffn[1], wqt, keys)
    out = _peer(xn, eid, gates, h, tables[1][0], tables[1][1], ln_final, True)
    return out.reshape(batch, seq, d)
```

```python
import functools

import jax
import jax.numpy as jnp
from jax import lax
from jax.experimental import pallas as pl
from jax.experimental.pallas import tpu as pltpu
from jax.experimental.pallas import tpu_sc as plsc

D_MODEL = 1024
NORM_EPS = 1e-6
MLA_HEADS = 8
MLA_Q_LORA = 384
MLA_KV_LORA = 256
MLA_NOPE = 128
MLA_ROPE = 64
MLA_V = 128
ROPE_THETA = 10000.0
HG_HEADS = 8
HG_DK = 128
HG_DV = 128
PEER_HEADS = 8
PEER_NKEYS = 128
PEER_HALF = 128
PEER_TOPK = 16
PEER_SLOTS = PEER_HEADS * PEER_TOPK

LANES = 128
SC_CORES = 2
SC_SUBCORES = 16
VMEM_LIMIT = 48 * 1024 * 1024

ROW_WORDS = D_MODEL // 2
TOKEN_BLOCK = 256
ATTN_BLOCK = 1024
HG_BLOCK = 256
HG_CHUNK = 64
HG_SUB = 16
GATHER_WINDOW = 32
GATHER_SLOTS = 3
GATHER_TOKENS = 1024
SC_LANES = 16
DOT_ROWS = 8
AXPY_CHUNKS = 8
NEG_INF = float("-inf")
MASKED_LOG_DECAY = -1e30


def _rms(x, g):
    return x * lax.rsqrt(jnp.mean(x * x, axis=-1, keepdims=True) + NORM_EPS) * g


def _params(*sem):
    return pltpu.CompilerParams(dimension_semantics=sem, vmem_limit_bytes=VMEM_LIMIT)


def _pack_kernel(t_ref, o_ref):
    t = t_ref[...]
    lo = pltpu.bitcast(t[:, :ROW_WORDS].astype(jnp.bfloat16).astype(jnp.float32), jnp.uint32)
    hi = pltpu.bitcast(t[:, ROW_WORDS:].astype(jnp.bfloat16).astype(jnp.float32), jnp.uint32)
    o_ref[...] = (lo >> 16) | (hi & jnp.uint32(0xFFFF0000))


def _pack_table(tab):
    e, d = tab.shape
    rows = 512
    return pl.pallas_call(
        _pack_kernel,
        grid=(e // rows,),
        in_specs=[pl.BlockSpec((rows, d), lambda i: (i, 0))],
        out_specs=pl.BlockSpec((rows, d // 2), lambda i: (i, 0)),
        out_shape=jax.ShapeDtypeStruct((e, d // 2), jnp.uint32),
        compiler_params=_params("parallel"),
        name="pack_table",
    )(tab)


def _rope_kernel(pos_ref, inv_ref, sign_ref, c_ref, s_ref):
    ang = pos_ref[...].astype(jnp.float32) * inv_ref[...]
    c_ref[...] = jnp.cos(ang)
    s_ref[...] = jnp.sin(ang) * sign_ref[...]


def _rope_tables(positions):
    n = positions.size
    lane = jnp.arange(LANES)
    inv = ROPE_THETA ** (-(2 * (lane % (MLA_ROPE // 2))).astype(jnp.float32) / MLA_ROPE)
    sign = jnp.where((lane % MLA_ROPE) < MLA_ROPE // 2, -1.0, 1.0).astype(jnp.float32)
    tb = 1024
    out = jax.ShapeDtypeStruct((n, LANES), jnp.float32)
    return pl.pallas_call(
        _rope_kernel,
        grid=(n // tb,),
        in_specs=[pl.BlockSpec((tb, 1), lambda i: (i, 0)),
                  pl.BlockSpec((1, LANES), lambda i: (0, 0)),
                  pl.BlockSpec((1, LANES), lambda i: (0, 0))],
        out_specs=[pl.BlockSpec((tb, LANES), lambda i: (i, 0))] * 2,
        out_shape=[out, out],
        compiler_params=_params("parallel"),
        name="rope_tables",
    )(positions.reshape(n, 1), inv.reshape(1, LANES), sign.reshape(1, LANES))


def _mla_proj_kernel(x_ref, g_ref, c_ref, s_ref, win_ref, qn_ref, wq_ref, kvn_ref, wkv_ref,
                     q_out, k_out, v_out):
    hn = _rms(x_ref[...], g_ref[...]).astype(jnp.bfloat16)
    z = jnp.dot(hn, win_ref[...], preferred_element_type=jnp.float32)
    c = c_ref[...]
    s = s_ref[...]
    o_kv = MLA_Q_LORA
    o_kr = MLA_Q_LORA + MLA_KV_LORA
    k_rope = (z[:, o_kr:o_kr + LANES] * c + z[:, o_kr + LANES:o_kr + 2 * LANES] * s).astype(jnp.bfloat16)
    cq = _rms(z[:, :MLA_Q_LORA], qn_ref[...]).astype(jnp.bfloat16)
    q = jnp.dot(cq, wq_ref[...], preferred_element_type=jnp.float32)
    ckv = _rms(z[:, o_kv:o_kr], kvn_ref[...]).astype(jnp.bfloat16)
    kv = jnp.dot(ckv, wkv_ref[...], preferred_element_type=jnp.float32)
    scale = (MLA_NOPE + MLA_ROPE) ** -0.5
    hw = MLA_HEADS * LANES
    for h in range(MLA_HEADS):
        sl = slice(h * LANES, (h + 1) * LANES)
        q_out[h, :, :LANES] = (q[:, sl] * scale).astype(jnp.bfloat16)
        qr = q[:, hw + h * LANES:hw + (h + 1) * LANES] * c + q[:, 2 * hw + h * LANES:2 * hw + (h + 1) * LANES] * s
        q_out[h, :, LANES:] = (qr * scale).astype(jnp.bfloat16)
        k_out[h, :, :LANES] = kv[:, sl].astype(jnp.bfloat16)
        k_out[h, :, LANES:] = k_rope
        v_out[h] = kv[:, hw + h * LANES:hw + (h + 1) * LANES].astype(jnp.bfloat16)


def _swap_halves(w):
    half = w.shape[-1] // 2
    return jnp.concatenate([w[..., half:], w[..., :half]], axis=-1)


def _pad_lanes(w):
    return jnp.pad(w, [(0, 0)] * (w.ndim - 1) + [(0, LANES - w.shape[-1])])


def _mla_weights(w_in, w_q_up, w_kv_up):
    o_kr = MLA_Q_LORA + MLA_KV_LORA
    w_kr = w_in[:, o_kr:]
    win = jnp.concatenate([w_in[:, :o_kr], _pad_lanes(w_kr), _pad_lanes(_swap_halves(w_kr))], axis=1)
    wq = w_q_up.reshape(MLA_Q_LORA, MLA_HEADS, MLA_NOPE + MLA_ROPE)
    wq_n = wq[:, :, :MLA_NOPE].reshape(MLA_Q_LORA, -1)
    wq_r = _pad_lanes(wq[:, :, MLA_NOPE:]).reshape(MLA_Q_LORA, -1)
    wq_rs = _pad_lanes(_swap_halves(wq[:, :, MLA_NOPE:])).reshape(MLA_Q_LORA, -1)
    wqp = jnp.concatenate([wq_n, wq_r, wq_rs], axis=1)
    wkv = w_kv_up.reshape(MLA_KV_LORA, MLA_HEADS, MLA_NOPE + MLA_V)
    wkvp = jnp.concatenate([wkv[:, :, :MLA_NOPE].reshape(MLA_KV_LORA, -1),
                            wkv[:, :, MLA_NOPE:].reshape(MLA_KV_LORA, -1)], axis=1)
    return win.astype(jnp.bfloat16), wqp.astype(jnp.bfloat16), wkvp.astype(jnp.bfloat16)


def _mla_proj(h, g, cos_t, sin_t, win, q_norm, wq, kv_norm, wkv):
    n = h.shape[0]
    tb = TOKEN_BLOCK
    full = lambda a: pl.BlockSpec(a.shape, lambda i: (0,) * a.ndim)
    g = g.reshape(1, -1)
    q_norm = q_norm.reshape(1, -1)
    kv_norm = kv_norm.reshape(1, -1)
    qk_t = jax.ShapeDtypeStruct((MLA_HEADS, n, 2 * LANES), jnp.bfloat16)
    v_t = jax.ShapeDtypeStruct((MLA_HEADS, n, MLA_V), jnp.bfloat16)
    return pl.pallas_call(
        _mla_proj_kernel,
        grid=(n // tb,),
        in_specs=[pl.BlockSpec((tb, D_MODEL), lambda i: (i, 0)), full(g),
                  pl.BlockSpec((tb, LANES), lambda i: (i, 0)), pl.BlockSpec((tb, LANES), lambda i: (i, 0)),
                  full(win), full(q_norm), full(wq), full(kv_norm), full(wkv)],
        out_specs=[pl.BlockSpec((MLA_HEADS, tb, 2 * LANES), lambda i: (0, i, 0)),
                   pl.BlockSpec((MLA_HEADS, tb, 2 * LANES), lambda i: (0, i, 0)),
                   pl.BlockSpec((MLA_HEADS, tb, MLA_V), lambda i: (0, i, 0))],
        out_shape=[qk_t, qk_t, v_t],
        compiler_params=_params("parallel"),
        name="mla_proj",
    )(h, g, cos_t, sin_t, win, q_norm, wq, kv_norm, wkv)


def _attn_kernel(qi_ref, ki_ref, q_ref, k_ref, v_ref, o_ref, m_sc, l_sc, acc_sc):
    step = pl.program_id(2)
    qi = qi_ref[step]
    ki = ki_ref[step]

    @pl.when(ki == 0)
    def _():
        m_sc[...] = jnp.full(m_sc.shape, NEG_INF, jnp.float32)
        l_sc[...] = jnp.zeros(l_sc.shape, jnp.float32)
        acc_sc[...] = jnp.zeros(acc_sc.shape, jnp.float32)

    def update(masked):
        sc = lax.dot_general(q_ref[...], k_ref[...], (((1,), (1,)), ((), ())),
                             preferred_element_type=jnp.float32)
        if masked:
            row = lax.broadcasted_iota(jnp.int32, sc.shape, 0)
            col = lax.broadcasted_iota(jnp.int32, sc.shape, 1)
            sc = jnp.where(col <= row, sc, NEG_INF)
        m_prev = m_sc[...]
        m_next = jnp.maximum(m_prev, jnp.max(sc, axis=1, keepdims=True))
        p = jnp.exp(sc - m_next[:, :1])
        alpha = jnp.exp(m_prev - m_next)
        l_sc[...] = alpha * l_sc[...] + jnp.sum(p, axis=1, keepdims=True)
        acc_sc[...] = alpha * acc_sc[...] + jnp.dot(p.astype(jnp.bfloat16), v_ref[...],
                                                    preferred_element_type=jnp.float32)
        m_sc[...] = m_next

    @pl.when(ki < qi)
    def _():
        update(False)

    @pl.when(ki == qi)
    def _():
        update(True)
        o_ref[...] = (acc_sc[...] / l_sc[...]).astype(o_ref.dtype)


def _attention(q, k, v, batch):
    n = q.shape[1]
    seq = n // batch
    blk = min(ATTN_BLOCK, seq)
    nb = seq // blk
    qi = jnp.array([i for i in range(nb) for _ in range(i + 1)], jnp.int32)
    ki = jnp.array([j for i in range(nb) for j in range(i + 1)], jnp.int32)
    grid_spec = pltpu.PrefetchScalarGridSpec(
        num_scalar_prefetch=2,
        grid=(batch, MLA_HEADS, qi.shape[0]),
        in_specs=[pl.BlockSpec((None, blk, 2 * LANES), lambda b, h, s, qi, ki: (h, b * nb + qi[s], 0)),
                  pl.BlockSpec((None, blk, 2 * LANES), lambda b, h, s, qi, ki: (h, b * nb + ki[s], 0)),
                  pl.BlockSpec((None, blk, MLA_V), lambda b, h, s, qi, ki: (h, b * nb + ki[s], 0))],
        out_specs=pl.BlockSpec((blk, MLA_V), lambda b, h, s, qi, ki: (b * nb + qi[s], h)),
        scratch_shapes=[pltpu.VMEM((blk, LANES), jnp.float32), pltpu.VMEM((blk, LANES), jnp.float32),
                        pltpu.VMEM((blk, MLA_V), jnp.float32)],
    )
    return pl.pallas_call(
        _attn_kernel,
        grid_spec=grid_spec,
        out_shape=jax.ShapeDtypeStruct((n, MLA_HEADS * MLA_V), jnp.bfloat16),
        compiler_params=_params("parallel", "parallel", "arbitrary"),
        name="mla_attention",
    )(qi, ki, q, k, v)


def _top_rows(vals, ids, count, out_rows):
    t = vals.shape[1]
    big = jnp.int32(2 ** 30)
    orow = lax.broadcasted_iota(jnp.int32, (out_rows, t), 0)

    def body(r, carry):
        cur, ov, oi = carry
        m = jnp.max(cur, axis=0, keepdims=True)
        pick = jnp.min(jnp.where(cur == m, ids, big), axis=0, keepdims=True)
        cur = jnp.where(ids == pick, NEG_INF, cur)
        ov = jnp.where(orow == r, m, ov)
        oi = jnp.where(orow == r, pick, oi)
        return cur, ov, oi

    init = (vals, jnp.zeros((out_rows, t), jnp.float32), jnp.zeros((out_rows, t), jnp.int32))
    _, ov, oi = lax.fori_loop(0, count, body, init)
    return ov, oi


_ROW_SLABS = [(0, 0, 16), (1, 0, 8)] + [(a, 0, 8) for a in range(2, 8)]
_COL_SLAB = (8, 16, 0)
_PAIR_ROWS = sum(hi - lo for _, lo, hi in _ROW_SLABS) + (_COL_SLAB[1] - _COL_SLAB[0])


def _route_kernel(o_ref, h_ref, wo_ref, g_ref, wqt_ref, keys_ref, pos_ref,
                  hn_out, xn_out, eid_out, gate_out, qt_sc, v_sc, i_sc):
    tb = h_ref.shape[0]
    hnew = h_ref[...] + jnp.dot(o_ref[...], wo_ref[...], preferred_element_type=jnp.float32)
    hn_out[...] = hnew
    xn = _rms(hnew, g_ref[...])
    xn_out[...] = xn
    qt_sc[...] = lax.dot_general(wqt_ref[...], xn.astype(jnp.bfloat16), (((1,), (1,)), ((), ())),
                                 preferred_element_type=jnp.float32).astype(jnp.bfloat16)
    key_ids = lax.broadcasted_iota(jnp.int32, (PEER_NKEYS, tb), 0)

    def group(g, carry):
        row0 = pl.multiple_of(g * PEER_HALF, PEER_HALF)
        st = jnp.dot(keys_ref[g], qt_sc[pl.ds(row0, PEER_HALF), :], preferred_element_type=jnp.float32)
        tv, ti = _top_rows(st, key_ids, PEER_TOPK, PEER_TOPK)
        out0 = pl.multiple_of(g * PEER_TOPK, PEER_TOPK)
        v_sc[pl.ds(out0, PEER_TOPK), :] = tv
        i_sc[pl.ds(out0, PEER_TOPK), :] = ti
        return carry

    lax.fori_loop(0, 2 * PEER_HEADS, group, 0)

    pos = pos_ref[...]

    def head(hd, carry):
        base = pl.multiple_of(hd * 2 * PEER_TOPK, 2 * PEER_TOPK)
        v1 = v_sc[pl.ds(base, PEER_TOPK), :]
        i1 = i_sc[pl.ds(base, PEER_TOPK), :]
        v2 = v_sc[pl.ds(base + PEER_TOPK, PEER_TOPK), :]
        i2 = i_sc[pl.ds(base + PEER_TOPK, PEER_TOPK), :]
        cv, ce = [], []
        for a, lo, hi in _ROW_SLABS:
            cv.append(v1[a:a + 1, :] + v2[lo:hi, :])
            ce.append(i1[a:a + 1, :] * PEER_NKEYS + i2[lo:hi, :])
        a_lo, a_hi, b = _COL_SLAB
        cv.append(v1[a_lo:a_hi, :] + v2[b:b + 1, :])
        ce.append(i1[a_lo:a_hi, :] * PEER_NKEYS + i2[b:b + 1, :])
        cv = jnp.concatenate(cv, axis=0)
        ce = jnp.concatenate(ce, axis=0)
        tv, tp = _top_rows(cv, jnp.broadcast_to(pos, cv.shape), PEER_TOPK, PEER_TOPK)
        te = jnp.zeros((PEER_TOPK, tb), jnp.int32)
        orow = lax.broadcasted_iota(jnp.int32, (PEER_TOPK, tb), 0)
        for r in range(PEER_TOPK):
            e_r = jnp.sum(jnp.where(pos == tp[r:r + 1, :], ce, 0), axis=0, keepdims=True)
            te = jnp.where(orow == r, e_r, te)
        ex = jnp.exp(tv - tv[0:1, :])
        gates = ex / jnp.sum(ex, axis=0, keepdims=True)
        out0 = pl.multiple_of(hd * PEER_TOPK, PEER_TOPK)
        v_sc[pl.ds(out0, PEER_TOPK), :] = gates
        i_sc[pl.ds(out0, PEER_TOPK), :] = te
        return carry

    lax.fori_loop(0, PEER_HEADS, head, 0)

    eid_out[...] = jnp.transpose(i_sc[:PEER_SLOTS, :].astype(jnp.float32)).astype(jnp.int32)
    gate_out[...] = jnp.transpose(v_sc[:PEER_SLOTS, :])


def _pair_positions():
    pos = [a * PEER_TOPK + b for a, lo, hi in _ROW_SLABS for b in range(lo, hi)]
    a_lo, a_hi, b = _COL_SLAB
    pos += [a * PEER_TOPK + b for a in range(a_lo, a_hi)]
    return jnp.array(pos, jnp.int32).reshape(_PAIR_ROWS, 1)


def _route(o, h, w_out, g, wqt, keys):
    n = h.shape[0]
    tb = TOKEN_BLOCK
    full = lambda a: pl.BlockSpec(a.shape, lambda i: (0,) * a.ndim)
    g = g.reshape(1, -1)
    pos = _pair_positions()
    row = pl.BlockSpec((tb, D_MODEL), lambda i: (i, 0))
    f32 = jnp.float32
    return pl.pallas_call(
        _route_kernel,
        grid=(n // tb,),
        in_specs=[row, row, full(w_out), full(g), full(wqt), full(keys), full(pos)],
        out_specs=[row, row, pl.BlockSpec((tb, PEER_SLOTS), lambda i: (i, 0)),
                   pl.BlockSpec((tb, PEER_SLOTS), lambda i: (i, 0))],
        out_shape=[jax.ShapeDtypeStruct((n, D_MODEL), f32), jax.ShapeDtypeStruct((n, D_MODEL), f32),
                   jax.ShapeDtypeStruct((n, PEER_SLOTS), jnp.int32),
                   jax.ShapeDtypeStruct((n, PEER_SLOTS), f32)],
        scratch_shapes=[pltpu.VMEM((2 * PEER_HEADS * PEER_HALF, tb), jnp.bfloat16),
                        pltpu.VMEM((2 * PEER_HEADS * PEER_TOPK, tb), f32),
                        pltpu.VMEM((2 * PEER_HEADS * PEER_TOPK, tb), jnp.int32)],
        compiler_params=_params("parallel"),
        name="peer_route",
    )(o, h, w_out, g, wqt, keys, pos)


def _sc_mesh():
    return plsc.VectorSubcoreMesh(core_axis_name="c", subcore_axis_name="s")


def _sc_params():
    return pltpu.CompilerParams(needs_layout_passes=False)


def _worker_id():
    return lax.axis_index("s") * SC_CORES + lax.axis_index("c")


def _gather_rows(tab_hbm, idx_ref, dst_ref, sem):
    return pltpu.make_async_copy(tab_hbm.at[idx_ref], dst_ref, sem)


def _unpack16(word):
    lo = lax.bitcast_convert_type(word << 16, jnp.float32)
    hi = lax.bitcast_convert_type(word & jnp.uint32(0xFFFF0000), jnp.float32)
    return lo, hi


def _sc_dots(tab, idx, x):
    t = x.shape[0]
    workers = SC_CORES * SC_SUBCORES
    tok_w = t // workers
    per_worker = tok_w * PEER_SLOTS
    w = GATHER_WINDOW
    slots = GATHER_SLOTS
    n_win = per_worker // w
    win_per_tok = PEER_SLOTS // w
    n_chunks = ROW_WORDS // SC_LANES
    f32 = jnp.float32

    @functools.partial(
        pl.kernel, out_type=jax.ShapeDtypeStruct((t * PEER_SLOTS,), f32), mesh=_sc_mesh(),
        scratch_types=[pltpu.VMEM((per_worker,), jnp.int32),
                       pltpu.VMEM((tok_w, D_MODEL), f32),
                       pltpu.VMEM((slots, w, ROW_WORDS), jnp.uint32),
                       pltpu.VMEM((per_worker,), f32),
                       pltpu.VMEM((w * SC_LANES,), f32),
                       pltpu.SemaphoreType.DMA((slots,))],
        compiler_params=_sc_params(), name="peer_dots")
    def dots(tab_hbm, i_hbm, x_hbm, act_hbm, idx_v, x_v, rows, act_v, part_v, sem):
        wid = _worker_id()
        base = pl.multiple_of(wid * per_worker, per_worker)
        tok0 = pl.multiple_of(wid * tok_w, tok_w)
        pltpu.sync_copy(i_hbm.at[pl.ds(base, per_worker)], idx_v)
        pltpu.sync_copy(x_hbm.at[pl.ds(tok0, tok_w)], x_v)

        def gather(win, slot):
            ix = idx_v.at[pl.ds(pl.multiple_of(win * w, w), w)]
            return _gather_rows(tab_hbm, ix, rows.at[slot], sem.at[slot])

        for s in range(slots - 1):
            gather(s, s).start()
        lane = lax.iota(jnp.int32, SC_LANES)

        @pl.loop(0, n_win)
        def _(win):
            slot = lax.rem(win, slots)
            nxt = win + slots - 1

            @pl.when(nxt < n_win)
            def _():
                gather(nxt, lax.rem(nxt, slots)).start()

            gather(win, slot).wait()
            tok = win // win_per_tok

            @pl.loop(0, w // DOT_ROWS)
            def _(g):
                r0 = g * DOT_ROWS
                acc_lo = [jnp.zeros((SC_LANES,), f32) for _ in range(DOT_ROWS)]
                acc_hi = [jnp.zeros((SC_LANES,), f32) for _ in range(DOT_ROWS)]
                for c in range(n_chunks):
                    x_lo = x_v[tok, pl.ds(c * SC_LANES, SC_LANES)]
                    x_hi = x_v[tok, pl.ds(ROW_WORDS + c * SC_LANES, SC_LANES)]
                    for r in range(DOT_ROWS):
                        lo, hi = _unpack16(rows[slot, r0 + r, pl.ds(c * SC_LANES, SC_LANES)])
                        acc_lo[r] = acc_lo[r] + lo * x_lo
                        acc_hi[r] = acc_hi[r] + hi * x_hi
                for r in range(DOT_ROWS):
                    part_v[pl.ds(pl.multiple_of((r0 + r) * SC_LANES, SC_LANES), SC_LANES)] = acc_lo[r] + acc_hi[r]

            for blk in range(w // SC_LANES):
                res = jnp.zeros((SC_LANES,), f32)
                for l in range(SC_LANES):
                    res = res + plsc.load_gather(part_v, [lane * SC_LANES + (blk * SC_LANES * SC_LANES + l)])
                act_v[pl.ds(pl.multiple_of(win * w + blk * SC_LANES, SC_LANES), SC_LANES)] = res

        pltpu.sync_copy(act_v, act_hbm.at[pl.ds(base, per_worker)])

    return dots(tab, idx, x)


def _sc_axpy(tab, idx, wts):
    p = idx.shape[0]
    t = p // PEER_SLOTS
    workers = SC_CORES * SC_SUBCORES
    tok_w = t // workers
    per_worker = tok_w * PEER_SLOTS
    w = GATHER_WINDOW
    slots = GATHER_SLOTS
    n_win = per_worker // w
    win_per_tok = PEER_SLOTS // w
    passes = ROW_WORDS // (SC_LANES * AXPY_CHUNKS)
    f32 = jnp.float32

    @functools.partial(
        pl.kernel, out_type=jax.ShapeDtypeStruct((t, D_MODEL), f32), mesh=_sc_mesh(),
        scratch_types=[pltpu.VMEM((per_worker,), jnp.int32),
                       pltpu.VMEM((per_worker,), f32),
                       pltpu.VMEM((slots, w, ROW_WORDS), jnp.uint32),
                       pltpu.VMEM((2, D_MODEL), f32),
                       pltpu.SemaphoreType.DMA((slots,)),
                       pltpu.SemaphoreType.DMA((2,))],
        compiler_params=_sc_params(), name="peer_axpy")
    def axpy(tab_hbm, i_hbm, w_hbm, y_hbm, idx_v, w_v, rows, y_v, sem, sem_y):
        wid = _worker_id()
        base = pl.multiple_of(wid * per_worker, per_worker)
        tok0 = wid * tok_w
        pltpu.sync_copy(i_hbm.at[pl.ds(base, per_worker)], idx_v)
        pltpu.sync_copy(w_hbm.at[pl.ds(base, per_worker)], w_v)

        def gather(win, slot):
            ix = idx_v.at[pl.ds(pl.multiple_of(win * w, w), w)]
            return _gather_rows(tab_hbm, ix, rows.at[slot], sem.at[slot])

        def y_write(tok, buf):
            return pltpu.make_async_copy(y_v.at[buf], y_hbm.at[tok0 + tok], sem_y.at[buf])

        for s in range(slots - 1):
            gather(s, s).start()

        @pl.loop(0, n_win)
        def _(win):
            slot = lax.rem(win, slots)
            nxt = win + slots - 1

            @pl.when(nxt < n_win)
            def _():
                gather(nxt, lax.rem(nxt, slots)).start()

            gather(win, slot).wait()
            tok = win // win_per_tok
            part = lax.rem(win, win_per_tok)
            buf = lax.rem(tok, 2)

            @pl.when(part == 0)
            def _():
                @pl.when(tok >= 2)
                def _():
                    y_write(tok - 2, buf).wait()

                for c in range(D_MODEL // SC_LANES):
                    y_v[buf, pl.ds(c * SC_LANES, SC_LANES)] = jnp.zeros((SC_LANES,), f32)

            for ps in range(passes):

                def group(g, accs):
                    accs = list(accs)
                    row0 = pl.multiple_of(g * SC_LANES, SC_LANES)
                    w_grp = w_v[pl.ds(pl.multiple_of(win * w + row0, SC_LANES), SC_LANES)]
                    for k in range(SC_LANES):
                        wj = jnp.take_along_axis(w_grp, jnp.full((SC_LANES,), k, jnp.int32), axis=0)
                        for c in range(AXPY_CHUNKS):
                            col = (ps * AXPY_CHUNKS + c) * SC_LANES
                            lo, hi = _unpack16(rows[slot, row0 + k, pl.ds(col, SC_LANES)])
                            accs[2 * c] = accs[2 * c] + lo * wj
                            accs[2 * c + 1] = accs[2 * c + 1] + hi * wj
                    return tuple(accs)

                zero = tuple(jnp.zeros((SC_LANES,), f32) for _ in range(2 * AXPY_CHUNKS))
                accs = lax.fori_loop(0, w // SC_LANES, group, zero)
                for c in range(AXPY_CHUNKS):
                    col = (ps * AXPY_CHUNKS + c) * SC_LANES
                    y_v[buf, pl.ds(col, SC_LANES)] = y_v[buf, pl.ds(col, SC_LANES)] + accs[2 * c]
                    y_v[buf, pl.ds(ROW_WORDS + col, SC_LANES)] = (
                        y_v[buf, pl.ds(ROW_WORDS + col, SC_LANES)] + accs[2 * c + 1])

            @pl.when(part == win_per_tok - 1)
            def _():
                y_write(tok, buf).start()

        for tok in (tok_w - 2, tok_w - 1):
            y_write(tok, tok % 2).wait()

    return axpy(tab, idx, wts)


def _slot_weight_kernel(act_ref, gate_ref, o_ref):
    act = act_ref[...]
    o_ref[...] = gate_ref[...] * (0.5 * act * (1.0 + lax.erf(act * (2.0 ** -0.5))))


def _slot_weights(act, gates):
    n = gates.shape[0]
    tb = 1024
    blk = pl.BlockSpec((tb, PEER_SLOTS), lambda i: (i, 0))
    return pl.pallas_call(
        _slot_weight_kernel, grid=(n // tb,), in_specs=[blk, blk], out_specs=blk,
        out_shape=jax.ShapeDtypeStruct((n, PEER_SLOTS), jnp.float32),
        compiler_params=_params("parallel"), name="peer_slot_weights",
    )(act, gates)


def _residual_kernel(h_ref, y_ref, g_ref, o_ref, *, final_norm):
    out = h_ref[...] + y_ref[...]
    o_ref[...] = _rms(out, g_ref[...]) if final_norm else out


def _residual(h, y, g_final, final_norm):
    n = h.shape[0]
    tb = 512
    blk = pl.BlockSpec((tb, D_MODEL), lambda i: (i, 0))
    g_final = g_final.reshape(1, -1)
    return pl.pallas_call(
        functools.partial(_residual_kernel, final_norm=final_norm),
        grid=(n // tb,), in_specs=[blk, blk, pl.BlockSpec((1, D_MODEL), lambda i: (0, 0))], out_specs=blk,
        out_shape=jax.ShapeDtypeStruct((n, D_MODEL), jnp.float32),
        compiler_params=_params("parallel"), name="peer_residual",
    )(h, y, g_final)


def _peer(xn, eid, gates, h, tab_u, tab_v, g_final, final_norm):
    n = xn.shape[0]
    tc = min(GATHER_TOKENS, n)
    chunks = [slice(c * tc, (c + 1) * tc) for c in range(n // tc)]
    act = jnp.concatenate([_sc_dots(tab_u, eid[tok].reshape(-1), xn[tok]) for tok in chunks])
    wts = _slot_weights(act.reshape(n, PEER_SLOTS), gates)
    y = jnp.concatenate([_sc_axpy(tab_v, eid[tok].reshape(-1), wts[tok].reshape(-1)) for tok in chunks])
    return _residual(h, y, g_final, final_norm)


_N_SUB = HG_CHUNK // HG_SUB
_OFF_PAIRS = [(i, j) for i in range(_N_SUB) for j in range(i)]


def _cum_matrix():
    t = jnp.arange(HG_CHUNK)[:, None]
    r = jnp.arange(HG_CHUNK)[None, :]
    sub = t // HG_SUB
    incl = r <= t
    before = r < sub * HG_SUB
    end = r < (sub + 1) * HG_SUB
    return jnp.concatenate([incl, before, end], axis=0).astype(jnp.float32)


def _hgrn_kernel(h_ref, g_ref, w_ref, lb_ref, on_ref, cum_ref, o_ref, z_sc, st_sc, lb_sc):
    @pl.when(pl.program_id(1) == 0)
    def _():
        st_sc[...] = jnp.zeros(st_sc.shape, jnp.float32)

    tb = h_ref.shape[0]
    hn = _rms(h_ref[...], g_ref[...]).astype(jnp.bfloat16)
    z_sc[...] = jnp.dot(hn, w_ref[...], preferred_element_type=jnp.float32)
    lbr = lb_ref[...]
    mx = jnp.max(lbr, axis=0, keepdims=True)
    ex = jnp.exp(lbr - mx)
    prob = ex / jnp.sum(ex, axis=0, keepdims=True)
    lb_sc[...] = jnp.broadcast_to((prob[0:1, :] + prob[1:2, :]) - prob[0:1, :], lb_sc.shape)
    wf = HG_HEADS * HG_DK
    sub_row = lax.broadcasted_iota(jnp.int32, (HG_SUB, HG_DK), 0)
    lane64 = lax.broadcasted_iota(jnp.int32, (HG_SUB, HG_CHUNK), 1)

    mxu = jnp.bfloat16
    nt = (((1,), (1,)), ((), ()))

    def head(hd, carry):
        c0 = pl.multiple_of(hd * HG_DK, HG_DK)
        lb = lb_sc[0:1, pl.ds(c0, HG_DK)]
        on = on_ref[0:1, pl.ds(c0, HG_DK)]
        st = st_sc[hd]
        for ch in range(tb // HG_CHUNK):
            rows = slice(ch * HG_CHUNK, (ch + 1) * HG_CHUNK)
            qp = z_sc[rows, pl.ds(c0, HG_DK)]
            fp = z_sc[rows, pl.ds(pl.multiple_of(wf + c0, HG_DK), HG_DK)]
            v = z_sc[rows, pl.ds(pl.multiple_of(2 * wf + c0, HG_DK), HG_DK)]
            gp = z_sc[rows, pl.ds(pl.multiple_of(2 * wf + HG_HEADS * HG_DV + c0, HG_DK), HG_DK)]
            f = lb + (1.0 - lb) * jax.nn.sigmoid(fp)
            lf = jnp.log(f)
            k = 1.0 - f
            q = qp * jax.nn.sigmoid(qp)
            cums = jnp.dot(cum_ref[...], lf, precision=lax.Precision.HIGHEST,
                           preferred_element_type=jnp.float32)
            b = cums[:HG_CHUNK]
            b_start = cums[HG_CHUNK:2 * HG_CHUNK]
            b_end = cums[2 * HG_CHUNK:]
            q_hat = q * jnp.exp(b - b_start)
            k_hat = k * jnp.exp(b_end - b)
            o_inter = lax.dot_general((q_hat * jnp.exp(b_start)).astype(mxu), st.astype(mxu), nt,
                                      preferred_element_type=jnp.float32)
            stacked = []
            for (i, j) in _OFF_PAIRS:
                d_ij = jnp.exp(b_start[i * HG_SUB:i * HG_SUB + 1, :] - b_end[j * HG_SUB:j * HG_SUB + 1, :])
                stacked.append(q_hat[i * HG_SUB:(i + 1) * HG_SUB, :] * d_ij)
            stacked = jnp.concatenate(stacked, axis=0).astype(mxu)
            off = lax.dot_general(stacked, k_hat.astype(mxu), nt,
                                  preferred_element_type=jnp.float32)
            a_rows = []
            for i in range(_N_SUB):
                blk = jnp.zeros((HG_SUB, HG_CHUNK), jnp.float32)
                for p, (pi, pj) in enumerate(_OFF_PAIRS):
                    if pi == i:
                        in_j = (lane64 >= pj * HG_SUB) & (lane64 < (pj + 1) * HG_SUB)
                        blk = jnp.where(in_j, off[p * HG_SUB:(p + 1) * HG_SUB, :], blk)
                b_blk = b[i * HG_SUB:(i + 1) * HG_SUB, :]
                q_blk = q[i * HG_SUB:(i + 1) * HG_SUB, :]
                for s in range(HG_SUB):
                    n = i * HG_SUB + s
                    e = jnp.exp(jnp.where(sub_row >= s, b_blk - b[n:n + 1, :], MASKED_LOG_DECAY))
                    col = jnp.sum(q_blk * k[n:n + 1, :] * e, axis=1, keepdims=True)
                    blk = jnp.where(lane64 == n, col, blk)
                a_rows.append(blk)
            a = jnp.concatenate(a_rows, axis=0).astype(mxu)
            o = o_inter + jnp.dot(a, v.astype(mxu), preferred_element_type=jnp.float32)
            b_last = b[HG_CHUNK - 1:HG_CHUNK, :]
            k_til = (k_hat * jnp.exp(b_last - b_end)).astype(mxu)
            upd = lax.dot_general(v.astype(mxu), k_til, (((0,), (0,)), ((), ())),
                                  preferred_element_type=jnp.float32)
            st = st * jnp.exp(b_last) + upd
            o = o * lax.rsqrt(jnp.mean(o * o, axis=-1, keepdims=True) + NORM_EPS)
            o = o * on * (gp * jax.nn.sigmoid(gp))
            o_ref[rows, pl.ds(c0, HG_DK)] = o.astype(o_ref.dtype)
        st_sc[hd] = st
        return carry

    lax.fori_loop(0, HG_HEADS, head, 0)


def _hgrn(h, g, w, lb_raw, out_norm, batch):
    n = h.shape[0]
    seq = n // batch
    tb = min(HG_BLOCK, seq)
    nb = seq // tb
    g = g.reshape(1, -1)
    out_norm = out_norm.reshape(1, -1)
    cum = _cum_matrix()
    full = lambda a: pl.BlockSpec(a.shape, lambda b, i: (0,) * a.ndim)
    f32 = jnp.float32
    return pl.pallas_call(
        _hgrn_kernel,
        grid=(batch, nb),
        in_specs=[pl.BlockSpec((tb, D_MODEL), lambda b, i: (b * nb + i, 0)), full(g), full(w),
                  full(lb_raw), full(out_norm), full(cum)],
        out_specs=pl.BlockSpec((tb, HG_HEADS * HG_DV), lambda b, i: (b * nb + i, 0)),
        out_shape=jax.ShapeDtypeStruct((n, HG_HEADS * HG_DV), jnp.bfloat16),
        scratch_shapes=[pltpu.VMEM((tb, w.shape[1]), f32),
                        pltpu.VMEM((HG_HEADS, HG_DV, HG_DK), f32),
                        pltpu.VMEM((8, HG_HEADS * HG_DK), f32)],
        compiler_params=_params("arbitrary", "arbitrary"),
        name="hgrn2",
    )(h, g, w, lb_raw, out_norm, cum)


def kernel(x, positions, ln_mix, ln_ffn, ln_final, mla_w_in, mla_q_norm, mla_w_q_up, mla_kv_norm,
           mla_w_kv_up, mla_w_out, hg_w_in, hg_lb, hg_out_norm, hg_w_out, peer_w_q, peer_sub_keys,
           peer_u, peer_v):
    batch, seq, d = x.shape
    n = batch * seq
    bf16 = jnp.bfloat16
    h = x.reshape(n, d)

    def route_weights(i):
        keys = peer_sub_keys[i].reshape(2 * PEER_HEADS, PEER_NKEYS, PEER_HALF).astype(bf16)
        return peer_w_q[i].T.astype(bf16), keys

    tables = [(_pack_table(peer_u[i]), _pack_table(peer_v[i])) for i in range(2)]

    cos_t, sin_t = _rope_tables(positions)
    win, wq, wkv = _mla_weights(mla_w_in[0], mla_w_q_up[0], mla_w_kv_up[0])
    q, k, v = _mla_proj(h, ln_mix[0], cos_t, sin_t, win, mla_q_norm[0], wq, mla_kv_norm[0], wkv)
    o = _attention(q, k, v, batch)
    wqt, keys = route_weights(0)
    h, xn, eid, gates = _route(o, h, mla_w_out[0].astype(bf16), ln_ffn[0], wqt, keys)
    h = _peer(xn, eid, gates, h, tables[0][0], tables[0][1], ln_final, False)

    o = _hgrn(h, ln_mix[1], hg_w_in[0].astype(bf16), hg_lb, hg_out_norm[0], batch)
    wqt, keys = route_weights(1)
    h, xn, eid, gates = _route(o, h, hg_w_out[0].astype(bf16), ln_ffn[1], wqt, keys)
    out = _peer(xn, eid, gates, h, tables[1][0], tables[1][1], ln_final, True)
    return out.reshape(batch, seq, d)
```

```python
import functools

import jax
import jax.numpy as jnp
from jax import lax
from jax.experimental import pallas as pl
from jax.experimental.pallas import tpu as pltpu
from jax.experimental.pallas import tpu_sc as plsc

D_MODEL = 1024
NORM_EPS = 1e-6
MLA_HEADS = 8
MLA_Q_LORA = 384
MLA_KV_LORA = 256
MLA_NOPE = 128
MLA_ROPE = 64
MLA_V = 128
ROPE_THETA = 10000.0
HG_HEADS = 8
HG_DK = 128
HG_DV = 128
PEER_HEADS = 8
PEER_NKEYS = 128
PEER_HALF = 128
PEER_TOPK = 16
PEER_SLOTS = PEER_HEADS * PEER_TOPK

LANES = 128
SC_CORES = 2
SC_SUBCORES = 16
VMEM_LIMIT = 48 * 1024 * 1024

ROW_WORDS = D_MODEL // 2
TOKEN_BLOCK = 256
ATTN_BLOCK = 1024
HG_BLOCK = 256
HG_CHUNK = 64
HG_SUB = 16
PIPE_TOKENS = 2048
GATHER_WINDOW = 32
DOTS_SLOTS = 5
AXPY_SLOTS = 7
GATHER_TOKENS = 1024
SC_LANES = 16
DOT_ROWS = 8
AXPY_CHUNKS = 8
NEG_INF = float("-inf")
MASKED_LOG_DECAY = -1e30


def _rms(x, g):
    return x * lax.rsqrt(jnp.mean(x * x, axis=-1, keepdims=True) + NORM_EPS) * g


def _params(*sem):
    return pltpu.CompilerParams(dimension_semantics=sem, vmem_limit_bytes=VMEM_LIMIT)


def _pack_kernel(t_ref, o_ref):
    t = t_ref[...]
    lo = pltpu.bitcast(t[:, :ROW_WORDS].astype(jnp.bfloat16).astype(jnp.float32), jnp.uint32)
    hi = pltpu.bitcast(t[:, ROW_WORDS:].astype(jnp.bfloat16).astype(jnp.float32), jnp.uint32)
    o_ref[...] = (lo >> 16) | (hi & jnp.uint32(0xFFFF0000))


def _pack_table(tab):
    e, d = tab.shape
    rows = 512
    return pl.pallas_call(
        _pack_kernel,
        grid=(e // rows,),
        in_specs=[pl.BlockSpec((rows, d), lambda i: (i, 0))],
        out_specs=pl.BlockSpec((rows, d // 2), lambda i: (i, 0)),
        out_shape=jax.ShapeDtypeStruct((e, d // 2), jnp.uint32),
        compiler_params=_params("parallel"),
        name="pack_table",
    )(tab)


def _rope_kernel(pos_ref, inv_ref, sign_ref, c_ref, s_ref):
    ang = pos_ref[...].astype(jnp.float32) * inv_ref[...]
    c_ref[...] = jnp.cos(ang)
    s_ref[...] = jnp.sin(ang) * sign_ref[...]


def _rope_tables(positions):
    n = positions.size
    lane = jnp.arange(LANES)
    inv = ROPE_THETA ** (-(2 * (lane % (MLA_ROPE // 2))).astype(jnp.float32) / MLA_ROPE)
    sign = jnp.where((lane % MLA_ROPE) < MLA_ROPE // 2, -1.0, 1.0).astype(jnp.float32)
    tb = 1024
    out = jax.ShapeDtypeStruct((n, LANES), jnp.float32)
    return pl.pallas_call(
        _rope_kernel,
        grid=(n // tb,),
        in_specs=[pl.BlockSpec((tb, 1), lambda i: (i, 0)),
                  pl.BlockSpec((1, LANES), lambda i: (0, 0)),
                  pl.BlockSpec((1, LANES), lambda i: (0, 0))],
        out_specs=[pl.BlockSpec((tb, LANES), lambda i: (i, 0))] * 2,
        out_shape=[out, out],
        compiler_params=_params("parallel"),
        name="rope_tables",
    )(positions.reshape(n, 1), inv.reshape(1, LANES), sign.reshape(1, LANES))


def _mla_proj_kernel(x_ref, g_ref, c_ref, s_ref, win_ref, qn_ref, wq_ref, kvn_ref, wkv_ref,
                     q_out, k_out, v_out):
    hn = _rms(x_ref[...], g_ref[...]).astype(jnp.bfloat16)
    z = jnp.dot(hn, win_ref[...], preferred_element_type=jnp.float32)
    c = c_ref[...]
    s = s_ref[...]
    o_kv = MLA_Q_LORA
    o_kr = MLA_Q_LORA + MLA_KV_LORA
    k_rope = (z[:, o_kr:o_kr + LANES] * c + z[:, o_kr + LANES:o_kr + 2 * LANES] * s).astype(jnp.bfloat16)
    cq = _rms(z[:, :MLA_Q_LORA], qn_ref[...]).astype(jnp.bfloat16)
    q = jnp.dot(cq, wq_ref[...], preferred_element_type=jnp.float32)
    ckv = _rms(z[:, o_kv:o_kr], kvn_ref[...]).astype(jnp.bfloat16)
    kv = jnp.dot(ckv, wkv_ref[...], preferred_element_type=jnp.float32)
    scale = (MLA_NOPE + MLA_ROPE) ** -0.5
    hw = MLA_HEADS * LANES
    for h in range(MLA_HEADS):
        sl = slice(h * LANES, (h + 1) * LANES)
        q_out[h, :, :LANES] = (q[:, sl] * scale).astype(jnp.bfloat16)
        qr = q[:, hw + h * LANES:hw + (h + 1) * LANES] * c + q[:, 2 * hw + h * LANES:2 * hw + (h + 1) * LANES] * s
        q_out[h, :, LANES:] = (qr * scale).astype(jnp.bfloat16)
        k_out[h, :, :LANES] = kv[:, sl].astype(jnp.bfloat16)
        k_out[h, :, LANES:] = k_rope
        v_out[h] = kv[:, hw + h * LANES:hw + (h + 1) * LANES].astype(jnp.bfloat16)


def _swap_halves(w):
    half = w.shape[-1] // 2
    return jnp.concatenate([w[..., half:], w[..., :half]], axis=-1)


def _pad_lanes(w):
    return jnp.pad(w, [(0, 0)] * (w.ndim - 1) + [(0, LANES - w.shape[-1])])


def _mla_weights(w_in, w_q_up, w_kv_up):
    o_kr = MLA_Q_LORA + MLA_KV_LORA
    w_kr = w_in[:, o_kr:]
    win = jnp.concatenate([w_in[:, :o_kr], _pad_lanes(w_kr), _pad_lanes(_swap_halves(w_kr))], axis=1)
    wq = w_q_up.reshape(MLA_Q_LORA, MLA_HEADS, MLA_NOPE + MLA_ROPE)
    wq_n = wq[:, :, :MLA_NOPE].reshape(MLA_Q_LORA, -1)
    wq_r = _pad_lanes(wq[:, :, MLA_NOPE:]).reshape(MLA_Q_LORA, -1)
    wq_rs = _pad_lanes(_swap_halves(wq[:, :, MLA_NOPE:])).reshape(MLA_Q_LORA, -1)
    wqp = jnp.concatenate([wq_n, wq_r, wq_rs], axis=1)
    wkv = w_kv_up.reshape(MLA_KV_LORA, MLA_HEADS, MLA_NOPE + MLA_V)
    wkvp = jnp.concatenate([wkv[:, :, :MLA_NOPE].reshape(MLA_KV_LORA, -1),
                            wkv[:, :, MLA_NOPE:].reshape(MLA_KV_LORA, -1)], axis=1)
    return win.astype(jnp.bfloat16), wqp.astype(jnp.bfloat16), wkvp.astype(jnp.bfloat16)


def _mla_proj(h, g, cos_t, sin_t, win, q_norm, wq, kv_norm, wkv):
    n = h.shape[0]
    tb = TOKEN_BLOCK
    full = lambda a: pl.BlockSpec(a.shape, lambda i: (0,) * a.ndim)
    g = g.reshape(1, -1)
    q_norm = q_norm.reshape(1, -1)
    kv_norm = kv_norm.reshape(1, -1)
    qk_t = jax.ShapeDtypeStruct((MLA_HEADS, n, 2 * LANES), jnp.bfloat16)
    v_t = jax.ShapeDtypeStruct((MLA_HEADS, n, MLA_V), jnp.bfloat16)
    return pl.pallas_call(
        _mla_proj_kernel,
        grid=(n // tb,),
        in_specs=[pl.BlockSpec((tb, D_MODEL), lambda i: (i, 0)), full(g),
                  pl.BlockSpec((tb, LANES), lambda i: (i, 0)), pl.BlockSpec((tb, LANES), lambda i: (i, 0)),
                  full(win), full(q_norm), full(wq), full(kv_norm), full(wkv)],
        out_specs=[pl.BlockSpec((MLA_HEADS, tb, 2 * LANES), lambda i: (0, i, 0)),
                   pl.BlockSpec((MLA_HEADS, tb, 2 * LANES), lambda i: (0, i, 0)),
                   pl.BlockSpec((MLA_HEADS, tb, MLA_V), lambda i: (0, i, 0))],
        out_shape=[qk_t, qk_t, v_t],
        compiler_params=_params("parallel"),
        name="mla_proj",
    )(h, g, cos_t, sin_t, win, q_norm, wq, kv_norm, wkv)


def _attn_kernel(qi_ref, ki_ref, q_ref, k_ref, v_ref, o_ref, m_sc, l_sc, acc_sc):
    step = pl.program_id(1)
    qi = qi_ref[step]
    ki = ki_ref[step]

    @pl.when(ki == 0)
    def _():
        m_sc[...] = jnp.full(m_sc.shape, NEG_INF, jnp.float32)
        l_sc[...] = jnp.zeros(l_sc.shape, jnp.float32)
        acc_sc[...] = jnp.zeros(acc_sc.shape, jnp.float32)

    def update(masked):
        sc = lax.dot_general(q_ref[...], k_ref[...], (((1,), (1,)), ((), ())),
                             preferred_element_type=jnp.float32)
        if masked:
            row = lax.broadcasted_iota(jnp.int32, sc.shape, 0)
            col = lax.broadcasted_iota(jnp.int32, sc.shape, 1)
            sc = jnp.where(col <= row, sc, NEG_INF)
        m_prev = m_sc[...]
        m_next = jnp.maximum(m_prev, jnp.max(sc, axis=1, keepdims=True))
        p = jnp.exp(sc - m_next[:, :1])
        alpha = jnp.exp(m_prev - m_next)
        l_sc[...] = alpha * l_sc[...] + jnp.sum(p, axis=1, keepdims=True)
        acc_sc[...] = alpha * acc_sc[...] + jnp.dot(p.astype(jnp.bfloat16), v_ref[...],
                                                    preferred_element_type=jnp.float32)
        m_sc[...] = m_next

    @pl.when(ki < qi)
    def _():
        update(False)

    @pl.when(ki == qi)
    def _():
        update(True)
        o_ref[...] = (acc_sc[...] / l_sc[...]).astype(o_ref.dtype)


def _attention(q, k, v, seq, blk, b, q_lo, q_hi):
    nb = seq // blk
    qi = jnp.array([i for i in range(q_lo, q_hi) for _ in range(i + 1)], jnp.int32)
    ki = jnp.array([j for i in range(q_lo, q_hi) for j in range(i + 1)], jnp.int32)
    grid_spec = pltpu.PrefetchScalarGridSpec(
        num_scalar_prefetch=2,
        grid=(MLA_HEADS, qi.shape[0]),
        in_specs=[pl.BlockSpec((None, blk, 2 * LANES), lambda h, s, qi, ki: (h, b * nb + qi[s], 0)),
                  pl.BlockSpec((None, blk, 2 * LANES), lambda h, s, qi, ki: (h, b * nb + ki[s], 0)),
                  pl.BlockSpec((None, blk, MLA_V), lambda h, s, qi, ki: (h, b * nb + ki[s], 0))],
        out_specs=pl.BlockSpec((blk, MLA_V), lambda h, s, qi, ki: (qi[s] - q_lo, h)),
        scratch_shapes=[pltpu.VMEM((blk, LANES), jnp.float32), pltpu.VMEM((blk, LANES), jnp.float32),
                        pltpu.VMEM((blk, MLA_V), jnp.float32)],
    )
    return pl.pallas_call(
        _attn_kernel,
        grid_spec=grid_spec,
        out_shape=jax.ShapeDtypeStruct(((q_hi - q_lo) * blk, MLA_HEADS * MLA_V), jnp.bfloat16),
        compiler_params=_params("parallel", "arbitrary"),
        name="mla_attention",
    )(qi, ki, q, k, v)


def _top_rows(vals, ids, count, out_rows):
    t = vals.shape[1]
    big = jnp.int32(2 ** 30)
    orow = lax.broadcasted_iota(jnp.int32, (out_rows, t), 0)

    def body(r, carry):
        cur, ov, oi = carry
        m = jnp.max(cur, axis=0, keepdims=True)
        pick = jnp.min(jnp.where(cur == m, ids, big), axis=0, keepdims=True)
        cur = jnp.where(ids == pick, NEG_INF, cur)
        ov = jnp.where(orow == r, m, ov)
        oi = jnp.where(orow == r, pick, oi)
        return cur, ov, oi

    init = (vals, jnp.zeros((out_rows, t), jnp.float32), jnp.zeros((out_rows, t), jnp.int32))
    _, ov, oi = lax.fori_loop(0, count, body, init)
    return ov, oi


_ROW_SLABS = [(0, 0, 16), (1, 0, 8)] + [(a, 0, 8) for a in range(2, 8)]
_COL_SLAB = (8, 16, 0)
_PAIR_ROWS = sum(hi - lo for _, lo, hi in _ROW_SLABS) + (_COL_SLAB[1] - _COL_SLAB[0])


def _route_kernel(o_ref, h_ref, wo_ref, g_ref, wqt_ref, keys_ref, pos_ref,
                  hn_out, xn_out, eid_out, gate_out, qt_sc, v_sc, i_sc):
    tb = h_ref.shape[0]
    hnew = h_ref[...] + jnp.dot(o_ref[...], wo_ref[...], preferred_element_type=jnp.float32)
    hn_out[...] = hnew
    xn = _rms(hnew, g_ref[...])
    xn_out[...] = xn
    qt_sc[...] = lax.dot_general(wqt_ref[...], xn.astype(jnp.bfloat16), (((1,), (1,)), ((), ())),
                                 preferred_element_type=jnp.float32).astype(jnp.bfloat16)
    key_ids = lax.broadcasted_iota(jnp.int32, (PEER_NKEYS, tb), 0)

    def group(g, carry):
        row0 = pl.multiple_of(g * PEER_HALF, PEER_HALF)
        st = jnp.dot(keys_ref[g], qt_sc[pl.ds(row0, PEER_HALF), :], preferred_element_type=jnp.float32)
        tv, ti = _top_rows(st, key_ids, PEER_TOPK, PEER_TOPK)
        out0 = pl.multiple_of(g * PEER_TOPK, PEER_TOPK)
        v_sc[pl.ds(out0, PEER_TOPK), :] = tv
        i_sc[pl.ds(out0, PEER_TOPK), :] = ti
        return carry

    lax.fori_loop(0, 2 * PEER_HEADS, group, 0)

    pos = pos_ref[...]

    def head(hd, carry):
        base = pl.multiple_of(hd * 2 * PEER_TOPK, 2 * PEER_TOPK)
        v1 = v_sc[pl.ds(base, PEER_TOPK), :]
        i1 = i_sc[pl.ds(base, PEER_TOPK), :]
        v2 = v_sc[pl.ds(base + PEER_TOPK, PEER_TOPK), :]
        i2 = i_sc[pl.ds(base + PEER_TOPK, PEER_TOPK), :]
        cv, ce = [], []
        for a, lo, hi in _ROW_SLABS:
            cv.append(v1[a:a + 1, :] + v2[lo:hi, :])
            ce.append(i1[a:a + 1, :] * PEER_NKEYS + i2[lo:hi, :])
        a_lo, a_hi, b = _COL_SLAB
        cv.append(v1[a_lo:a_hi, :] + v2[b:b + 1, :])
        ce.append(i1[a_lo:a_hi, :] * PEER_NKEYS + i2[b:b + 1, :])
        cv = jnp.concatenate(cv, axis=0)
        ce = jnp.concatenate(ce, axis=0)
        tv, tp = _top_rows(cv, jnp.broadcast_to(pos, cv.shape), PEER_TOPK, PEER_TOPK)
        te = jnp.zeros((PEER_TOPK, tb), jnp.int32)
        orow = lax.broadcasted_iota(jnp.int32, (PEER_TOPK, tb), 0)
        for r in range(PEER_TOPK):
            e_r = jnp.sum(jnp.where(pos == tp[r:r + 1, :], ce, 0), axis=0, keepdims=True)
            te = jnp.where(orow == r, e_r, te)
        ex = jnp.exp(tv - tv[0:1, :])
        gates = ex / jnp.sum(ex, axis=0, keepdims=True)
        out0 = pl.multiple_of(hd * PEER_TOPK, PEER_TOPK)
        v_sc[pl.ds(out0, PEER_TOPK), :] = gates
        i_sc[pl.ds(out0, PEER_TOPK), :] = te
        return carry

    lax.fori_loop(0, PEER_HEADS, head, 0)

    eid_out[...] = jnp.transpose(i_sc[:PEER_SLOTS, :].astype(jnp.float32)).astype(jnp.int32)
    gate_out[...] = jnp.transpose(v_sc[:PEER_SLOTS, :])


def _pair_positions():
    pos = [a * PEER_TOPK + b for a, lo, hi in _ROW_SLABS for b in range(lo, hi)]
    a_lo, a_hi, b = _COL_SLAB
    pos += [a * PEER_TOPK + b for a in range(a_lo, a_hi)]
    return jnp.array(pos, jnp.int32).reshape(_PAIR_ROWS, 1)


def _route(o, h, h_row0, w_out, g, wqt, keys):
    n = o.shape[0]
    tb = TOKEN_BLOCK
    full = lambda a: pl.BlockSpec(a.shape, lambda i: (0,) * a.ndim)
    g = g.reshape(1, -1)
    pos = _pair_positions()
    row = pl.BlockSpec((tb, D_MODEL), lambda i: (i, 0))
    h_row = pl.BlockSpec((tb, D_MODEL), lambda i: (i + h_row0 // tb, 0))
    f32 = jnp.float32
    return pl.pallas_call(
        _route_kernel,
        grid=(n // tb,),
        in_specs=[row, h_row, full(w_out), full(g), full(wqt), full(keys), full(pos)],
        out_specs=[row, row, pl.BlockSpec((tb, PEER_SLOTS), lambda i: (i, 0)),
                   pl.BlockSpec((tb, PEER_SLOTS), lambda i: (i, 0))],
        out_shape=[jax.ShapeDtypeStruct((n, D_MODEL), f32), jax.ShapeDtypeStruct((n, D_MODEL), f32),
                   jax.ShapeDtypeStruct((n, PEER_SLOTS), jnp.int32),
                   jax.ShapeDtypeStruct((n, PEER_SLOTS), f32)],
        scratch_shapes=[pltpu.VMEM((2 * PEER_HEADS * PEER_HALF, tb), jnp.bfloat16),
                        pltpu.VMEM((2 * PEER_HEADS * PEER_TOPK, tb), f32),
                        pltpu.VMEM((2 * PEER_HEADS * PEER_TOPK, tb), jnp.int32)],
        compiler_params=_params("parallel"),
        name="peer_route",
    )(o, h, w_out, g, wqt, keys, pos)


def _sc_mesh():
    return plsc.VectorSubcoreMesh(core_axis_name="c", subcore_axis_name="s")


def _sc_params():
    return pltpu.CompilerParams(needs_layout_passes=False)


def _worker_id():
    return lax.axis_index("s") * SC_CORES + lax.axis_index("c")


def _gather_rows(tab_hbm, idx_ref, dst_ref, sem):
    return pltpu.make_async_copy(tab_hbm.at[idx_ref], dst_ref, sem)


def _unpack16(word):
    lo = lax.bitcast_convert_type(word << 16, jnp.float32)
    hi = lax.bitcast_convert_type(word & jnp.uint32(0xFFFF0000), jnp.float32)
    return lo, hi


def _sc_dots(tab, idx, x):
    t = x.shape[0]
    workers = SC_CORES * SC_SUBCORES
    tok_w = t // workers
    per_worker = tok_w * PEER_SLOTS
    w = GATHER_WINDOW
    slots = DOTS_SLOTS
    n_win = per_worker // w
    win_per_tok = PEER_SLOTS // w
    n_chunks = ROW_WORDS // SC_LANES
    f32 = jnp.float32

    @functools.partial(
        pl.kernel, out_type=jax.ShapeDtypeStruct((t * PEER_SLOTS,), f32), mesh=_sc_mesh(),
        scratch_types=[pltpu.VMEM((per_worker,), jnp.int32),
                       pltpu.VMEM((tok_w, D_MODEL), f32),
                       pltpu.VMEM((slots, w, ROW_WORDS), jnp.uint32),
                       pltpu.VMEM((per_worker,), f32),
                       pltpu.VMEM((w * SC_LANES,), f32),
                       pltpu.SemaphoreType.DMA((slots,))],
        compiler_params=_sc_params(), name="peer_dots")
    def dots(tab_hbm, i_hbm, x_hbm, act_hbm, idx_v, x_v, rows, act_v, part_v, sem):
        wid = _worker_id()
        base = pl.multiple_of(wid * per_worker, per_worker)
        tok0 = pl.multiple_of(wid * tok_w, tok_w)
        pltpu.sync_copy(i_hbm.at[pl.ds(base, per_worker)], idx_v)
        pltpu.sync_copy(x_hbm.at[pl.ds(tok0, tok_w)], x_v)

        def gather(win, slot):
            ix = idx_v.at[pl.ds(pl.multiple_of(win * w, w), w)]
            return _gather_rows(tab_hbm, ix, rows.at[slot], sem.at[slot])

        for s in range(slots - 1):
            gather(s, s).start()
        lane = lax.iota(jnp.int32, SC_LANES)

        @pl.loop(0, n_win)
        def _(win):
            slot = lax.rem(win, slots)
            nxt = win + slots - 1

            @pl.when(nxt < n_win)
            def _():
                gather(nxt, lax.rem(nxt, slots)).start()

            gather(win, slot).wait()
            tok = win // win_per_tok

            @pl.loop(0, w // DOT_ROWS)
            def _(g):
                r0 = g * DOT_ROWS
                acc_lo = [jnp.zeros((SC_LANES,), f32) for _ in range(DOT_ROWS)]
                acc_hi = [jnp.zeros((SC_LANES,), f32) for _ in range(DOT_ROWS)]
                for c in range(n_chunks):
                    x_lo = x_v[tok, pl.ds(c * SC_LANES, SC_LANES)]
                    x_hi = x_v[tok, pl.ds(ROW_WORDS + c * SC_LANES, SC_LANES)]
                    for r in range(DOT_ROWS):
                        lo, hi = _unpack16(rows[slot, r0 + r, pl.ds(c * SC_LANES, SC_LANES)])
                        acc_lo[r] = acc_lo[r] + lo * x_lo
                        acc_hi[r] = acc_hi[r] + hi * x_hi
                for r in range(DOT_ROWS):
                    part_v[pl.ds(pl.multiple_of((r0 + r) * SC_LANES, SC_LANES), SC_LANES)] = acc_lo[r] + acc_hi[r]

            for blk in range(w // SC_LANES):
                res = jnp.zeros((SC_LANES,), f32)
                for l in range(SC_LANES):
                    res = res + plsc.load_gather(part_v, [lane * SC_LANES + (blk * SC_LANES * SC_LANES + l)])
                act_v[pl.ds(pl.multiple_of(win * w + blk * SC_LANES, SC_LANES), SC_LANES)] = res

        pltpu.sync_copy(act_v, act_hbm.at[pl.ds(base, per_worker)])

    return dots(tab, idx, x)


def _sc_axpy(tab, idx, wts):
    p = idx.shape[0]
    t = p // PEER_SLOTS
    workers = SC_CORES * SC_SUBCORES
    tok_w = t // workers
    per_worker = tok_w * PEER_SLOTS
    w = GATHER_WINDOW
    slots = AXPY_SLOTS
    n_win = per_worker // w
    win_per_tok = PEER_SLOTS // w
    passes = ROW_WORDS // (SC_LANES * AXPY_CHUNKS)
    f32 = jnp.float32

    @functools.partial(
        pl.kernel, out_type=jax.ShapeDtypeStruct((t, D_MODEL), f32), mesh=_sc_mesh(),
        scratch_types=[pltpu.VMEM((per_worker,), jnp.int32),
                       pltpu.VMEM((per_worker,), f32),
                       pltpu.VMEM((slots, w, ROW_WORDS), jnp.uint32),
                       pltpu.VMEM((2, D_MODEL), f32),
                       pltpu.SemaphoreType.DMA((slots,)),
                       pltpu.SemaphoreType.DMA((2,))],
        compiler_params=_sc_params(), name="peer_axpy")
    def axpy(tab_hbm, i_hbm, w_hbm, y_hbm, idx_v, w_v, rows, y_v, sem, sem_y):
        wid = _worker_id()
        base = pl.multiple_of(wid * per_worker, per_worker)
        tok0 = wid * tok_w
        pltpu.sync_copy(i_hbm.at[pl.ds(base, per_worker)], idx_v)
        pltpu.sync_copy(w_hbm.at[pl.ds(base, per_worker)], w_v)

        def gather(win, slot):
            ix = idx_v.at[pl.ds(pl.multiple_of(win * w, w), w)]
            return _gather_rows(tab_hbm, ix, rows.at[slot], sem.at[slot])

        def y_write(tok, buf):
            return pltpu.make_async_copy(y_v.at[buf], y_hbm.at[tok0 + tok], sem_y.at[buf])

        for s in range(slots - 1):
            gather(s, s).start()

        @pl.loop(0, n_win)
        def _(win):
            slot = lax.rem(win, slots)
            nxt = win + slots - 1

            @pl.when(nxt < n_win)
            def _():
                gather(nxt, lax.rem(nxt, slots)).start()

            gather(win, slot).wait()
            tok = win // win_per_tok
            part = lax.rem(win, win_per_tok)
            buf = lax.rem(tok, 2)

            @pl.when(part == 0)
            def _():
                @pl.when(tok >= 2)
                def _():
                    y_write(tok - 2, buf).wait()

                for c in range(D_MODEL // SC_LANES):
                    y_v[buf, pl.ds(c * SC_LANES, SC_LANES)] = jnp.zeros((SC_LANES,), f32)

            for ps in range(passes):

                def group(g, accs):
                    accs = list(accs)
                    row0 = pl.multiple_of(g * SC_LANES, SC_LANES)
                    w_grp = w_v[pl.ds(pl.multiple_of(win * w + row0, SC_LANES), SC_LANES)]
                    for k in range(SC_LANES):
                        wj = jnp.take_along_axis(w_grp, jnp.full((SC_LANES,), k, jnp.int32), axis=0)
                        for c in range(AXPY_CHUNKS):
                            col = (ps * AXPY_CHUNKS + c) * SC_LANES
                            lo, hi = _unpack16(rows[slot, row0 + k, pl.ds(col, SC_LANES)])
                            accs[2 * c] = accs[2 * c] + lo * wj
                            accs[2 * c + 1] = accs[2 * c + 1] + hi * wj
                    return tuple(accs)

                zero = tuple(jnp.zeros((SC_LANES,), f32) for _ in range(2 * AXPY_CHUNKS))
                accs = lax.fori_loop(0, w // SC_LANES, group, zero)
                for c in range(AXPY_CHUNKS):
                    col = (ps * AXPY_CHUNKS + c) * SC_LANES
                    y_v[buf, pl.ds(col, SC_LANES)] = y_v[buf, pl.ds(col, SC_LANES)] + accs[2 * c]
                    y_v[buf, pl.ds(ROW_WORDS + col, SC_LANES)] = (
                        y_v[buf, pl.ds(ROW_WORDS + col, SC_LANES)] + accs[2 * c + 1])

            @pl.when(part == win_per_tok - 1)
            def _():
                y_write(tok, buf).start()

        for tok in (tok_w - 2, tok_w - 1):
            y_write(tok, tok % 2).wait()

    return axpy(tab, idx, wts)


def _slot_weight_kernel(act_ref, gate_ref, o_ref):
    act = act_ref[...]
    o_ref[...] = gate_ref[...] * (0.5 * act * (1.0 + lax.erf(act * (2.0 ** -0.5))))


def _slot_weights(act, gates):
    n = gates.shape[0]
    tb = 1024
    blk = pl.BlockSpec((tb, PEER_SLOTS), lambda i: (i, 0))
    return pl.pallas_call(
        _slot_weight_kernel, grid=(n // tb,), in_specs=[blk, blk], out_specs=blk,
        out_shape=jax.ShapeDtypeStruct((n, PEER_SLOTS), jnp.float32),
        compiler_params=_params("parallel"), name="peer_slot_weights",
    )(act, gates)


def _residual_kernel(h_ref, y_ref, g_ref, o_ref, *, final_norm):
    out = h_ref[...] + y_ref[...]
    o_ref[...] = _rms(out, g_ref[...]) if final_norm else out


def _residual(h, y, g_final, final_norm):
    n = h.shape[0]
    tb = 512
    blk = pl.BlockSpec((tb, D_MODEL), lambda i: (i, 0))
    g_final = g_final.reshape(1, -1)
    return pl.pallas_call(
        functools.partial(_residual_kernel, final_norm=final_norm),
        grid=(n // tb,), in_specs=[blk, blk, pl.BlockSpec((1, D_MODEL), lambda i: (0, 0))], out_specs=blk,
        out_shape=jax.ShapeDtypeStruct((n, D_MODEL), jnp.float32),
        compiler_params=_params("parallel"), name="peer_residual",
    )(h, y, g_final)


def _peer(xn, eid, gates, h, tab_u, tab_v, g_final, final_norm):
    n = xn.shape[0]
    tc = min(GATHER_TOKENS, n)
    chunks = [slice(c * tc, (c + 1) * tc) for c in range(n // tc)]
    act = jnp.concatenate([_sc_dots(tab_u, eid[tok].reshape(-1), xn[tok]) for tok in chunks])
    wts = _slot_weights(act.reshape(n, PEER_SLOTS), gates)
    y = jnp.concatenate([_sc_axpy(tab_v, eid[tok].reshape(-1), wts[tok].reshape(-1)) for tok in chunks])
    return _residual(h, y, g_final, final_norm)


_N_SUB = HG_CHUNK // HG_SUB
_OFF_PAIRS = [(i, j) for i in range(_N_SUB) for j in range(i)]


def _cum_matrix():
    t = jnp.arange(HG_CHUNK)[:, None]
    r = jnp.arange(HG_CHUNK)[None, :]
    sub = t // HG_SUB
    incl = r <= t
    before = r < sub * HG_SUB
    end = r < (sub + 1) * HG_SUB
    return jnp.concatenate([incl, before, end], axis=0).astype(jnp.float32)


def _hgrn_kernel(h_ref, g_ref, w_ref, lb_ref, on_ref, cum_ref, st_in_ref, o_ref, st_out_ref,
                 z_sc, st_sc, lb_sc):
    @pl.when(pl.program_id(0) == 0)
    def _():
        st_sc[...] = st_in_ref[...]

    tb = h_ref.shape[0]
    hn = _rms(h_ref[...], g_ref[...]).astype(jnp.bfloat16)
    z_sc[...] = jnp.dot(hn, w_ref[...], preferred_element_type=jnp.float32)
    lbr = lb_ref[...]
    mx = jnp.max(lbr, axis=0, keepdims=True)
    ex = jnp.exp(lbr - mx)
    prob = ex / jnp.sum(ex, axis=0, keepdims=True)
    lb_sc[...] = jnp.broadcast_to((prob[0:1, :] + prob[1:2, :]) - prob[0:1, :], lb_sc.shape)
    wf = HG_HEADS * HG_DK
    sub_row = lax.broadcasted_iota(jnp.int32, (HG_SUB, HG_DK), 0)
    lane64 = lax.broadcasted_iota(jnp.int32, (HG_SUB, HG_CHUNK), 1)

    mxu = jnp.bfloat16
    nt = (((1,), (1,)), ((), ()))

    def head(hd, carry):
        c0 = pl.multiple_of(hd * HG_DK, HG_DK)
        lb = lb_sc[0:1, pl.ds(c0, HG_DK)]
        on = on_ref[0:1, pl.ds(c0, HG_DK)]
        st = st_sc[hd]
        for ch in range(tb // HG_CHUNK):
            rows = slice(ch * HG_CHUNK, (ch + 1) * HG_CHUNK)
            qp = z_sc[rows, pl.ds(c0, HG_DK)]
            fp = z_sc[rows, pl.ds(pl.multiple_of(wf + c0, HG_DK), HG_DK)]
            v = z_sc[rows, pl.ds(pl.multiple_of(2 * wf + c0, HG_DK), HG_DK)]
            gp = z_sc[rows, pl.ds(pl.multiple_of(2 * wf + HG_HEADS * HG_DV + c0, HG_DK), HG_DK)]
            f = lb + (1.0 - lb) * jax.nn.sigmoid(fp)
            lf = jnp.log(f)
            k = 1.0 - f
            q = qp * jax.nn.sigmoid(qp)
            cums = jnp.dot(cum_ref[...], lf, precision=lax.Precision.HIGHEST,
                           preferred_element_type=jnp.float32)
            b = cums[:HG_CHUNK]
            b_start = cums[HG_CHUNK:2 * HG_CHUNK]
            b_end = cums[2 * HG_CHUNK:]
            q_hat = q * jnp.exp(b - b_start)
            k_hat = k * jnp.exp(b_end - b)
            o_inter = lax.dot_general((q_hat * jnp.exp(b_start)).astype(mxu), st.astype(mxu), nt,
                                      preferred_element_type=jnp.float32)
            stacked = []
            for (i, j) in _OFF_PAIRS:
                d_ij = jnp.exp(b_start[i * HG_SUB:i * HG_SUB + 1, :] - b_end[j * HG_SUB:j * HG_SUB + 1, :])
                stacked.append(q_hat[i * HG_SUB:(i + 1) * HG_SUB, :] * d_ij)
            stacked = jnp.concatenate(stacked, axis=0).astype(mxu)
            off = lax.dot_general(stacked, k_hat.astype(mxu), nt,
                                  preferred_element_type=jnp.float32)
            a_rows = []
            for i in range(_N_SUB):
                blk = jnp.zeros((HG_SUB, HG_CHUNK), jnp.float32)
                for p, (pi, pj) in enumerate(_OFF_PAIRS):
                    if pi == i:
                        in_j = (lane64 >= pj * HG_SUB) & (lane64 < (pj + 1) * HG_SUB)
                        blk = jnp.where(in_j, off[p * HG_SUB:(p + 1) * HG_SUB, :], blk)
                b_blk = b[i * HG_SUB:(i + 1) * HG_SUB, :]
                q_blk = q[i * HG_SUB:(i + 1) * HG_SUB, :]
                for s in range(HG_SUB):
                    n = i * HG_SUB + s
                    e = jnp.exp(jnp.where(sub_row >= s, b_blk - b[n:n + 1, :], MASKED_LOG_DECAY))
                    col = jnp.sum(q_blk * k[n:n + 1, :] * e, axis=1, keepdims=True)
                    blk = jnp.where(lane64 == n, col, blk)
                a_rows.append(blk)
            a = jnp.concatenate(a_rows, axis=0).astype(mxu)
            o = o_inter + jnp.dot(a, v.astype(mxu), preferred_element_type=jnp.float32)
            b_last = b[HG_CHUNK - 1:HG_CHUNK, :]
            k_til = (k_hat * jnp.exp(b_last - b_end)).astype(mxu)
            upd = lax.dot_general(v.astype(mxu), k_til, (((0,), (0,)), ((), ())),
                                  preferred_element_type=jnp.float32)
            st = st * jnp.exp(b_last) + upd
            o = o * lax.rsqrt(jnp.mean(o * o, axis=-1, keepdims=True) + NORM_EPS)
            o = o * on * (gp * jax.nn.sigmoid(gp))
            o_ref[rows, pl.ds(c0, HG_DK)] = o.astype(o_ref.dtype)
        st_sc[hd] = st
        return carry

    lax.fori_loop(0, HG_HEADS, head, 0)

    @pl.when(pl.program_id(0) == pl.num_programs(0) - 1)
    def _():
        st_out_ref[...] = st_sc[...]


def _hgrn(h, state, g, w, lb_raw, out_norm):
    n = h.shape[0]
    tb = min(HG_BLOCK, n)
    g = g.reshape(1, -1)
    out_norm = out_norm.reshape(1, -1)
    cum = _cum_matrix()
    full = lambda a: pl.BlockSpec(a.shape, lambda i: (0,) * a.ndim)
    f32 = jnp.float32
    return pl.pallas_call(
        _hgrn_kernel,
        grid=(n // tb,),
        in_specs=[pl.BlockSpec((tb, D_MODEL), lambda i: (i, 0)), full(g), full(w),
                  full(lb_raw), full(out_norm), full(cum), full(state)],
        out_specs=[pl.BlockSpec((tb, HG_HEADS * HG_DV), lambda i: (i, 0)), full(state)],
        out_shape=[jax.ShapeDtypeStruct((n, HG_HEADS * HG_DV), jnp.bfloat16),
                   jax.ShapeDtypeStruct(state.shape, f32)],
        scratch_shapes=[pltpu.VMEM((tb, w.shape[1]), f32),
                        pltpu.VMEM((HG_HEADS, HG_DV, HG_DK), f32),
                        pltpu.VMEM((8, HG_HEADS * HG_DK), f32)],
        compiler_params=_params("arbitrary"),
        name="hgrn2",
    )(h, g, w, lb_raw, out_norm, cum, state)


def kernel(x, positions, ln_mix, ln_ffn, ln_final, mla_w_in, mla_q_norm, mla_w_q_up, mla_kv_norm,
           mla_w_kv_up, mla_w_out, hg_w_in, hg_lb, hg_out_norm, hg_w_out, peer_w_q, peer_sub_keys,
           peer_u, peer_v):
    batch, seq, d = x.shape
    n = batch * seq
    bf16 = jnp.bfloat16
    h = x.reshape(n, d)

    def route_weights(i):
        keys = peer_sub_keys[i].reshape(2 * PEER_HEADS, PEER_NKEYS, PEER_HALF).astype(bf16)
        return peer_w_q[i].T.astype(bf16), keys

    tables = [(_pack_table(peer_u[i]), _pack_table(peer_v[i])) for i in range(2)]
    cos_t, sin_t = _rope_tables(positions)
    win, wq, wkv = _mla_weights(mla_w_in[0], mla_w_q_up[0], mla_w_kv_up[0])
    q, k, v = _mla_proj(h, ln_mix[0], cos_t, sin_t, win, mla_q_norm[0], wq, mla_kv_norm[0], wkv)
    wqt0, keys0 = route_weights(0)
    wqt1, keys1 = route_weights(1)
    mla_wo = mla_w_out[0].astype(bf16)
    hg_wi = hg_w_in[0].astype(bf16)
    hg_wo = hg_w_out[0].astype(bf16)

    chunk = min(PIPE_TOKENS, seq)
    blk = min(ATTN_BLOCK, chunk)
    outs = []
    for b in range(batch):
        state = jnp.zeros((HG_HEADS, HG_DV, HG_DK), jnp.float32)
        for j in range(seq // chunk):
            row0 = b * seq + j * chunk
            o = _attention(q, k, v, seq, blk, b, j * chunk // blk, (j + 1) * chunk // blk)
            hc, xn, eid, gates = _route(o, h, row0, mla_wo, ln_ffn[0], wqt0, keys0)
            hc = _peer(xn, eid, gates, hc, tables[0][0], tables[0][1], ln_final, False)
            o, state = _hgrn(hc, state, ln_mix[1], hg_wi, hg_lb, hg_out_norm[0])
            hc, xn, eid, gates = _route(o, hc, 0, hg_wo, ln_ffn[1], wqt1, keys1)
            outs.append(_peer(xn, eid, gates, hc, tables[1][0], tables[1][1], ln_final, True))
    return jnp.concatenate(outs, axis=0).reshape(batch, seq, d)
```

```python
import functools

import jax
import jax.numpy as jnp
from jax import lax
from jax.experimental import pallas as pl
from jax.experimental.pallas import tpu as pltpu
from jax.experimental.pallas import tpu_sc as plsc

D_MODEL = 1024
NORM_EPS = 1e-6
MLA_HEADS = 8
MLA_Q_LORA = 384
MLA_KV_LORA = 256
MLA_NOPE = 128
MLA_ROPE = 64
MLA_V = 128
ROPE_THETA = 10000.0
HG_HEADS = 8
HG_DK = 128
HG_DV = 128
PEER_HEADS = 8
PEER_NKEYS = 128
PEER_HALF = 128
PEER_TOPK = 16
PEER_SLOTS = PEER_HEADS * PEER_TOPK

LANES = 128
SC_CORES = 2
SC_SUBCORES = 16
VMEM_LIMIT = 48 * 1024 * 1024

ROW_WORDS = D_MODEL // 2
TOKEN_BLOCK = 256
ATTN_BLOCK = 1024
HG_BLOCK = 256
HG_CHUNK = 64
HG_SUB = 16
PIPE_TOKENS = 2048
GATHER_WINDOW = 32
DOTS_SLOTS = 5
AXPY_SLOTS = 7
GATHER_TOKENS = 1024
SC_LANES = 16
DOT_ROWS = 8
AXPY_CHUNKS = 8
NEG_INF = float("-inf")
MASKED_LOG_DECAY = -1e30


def _rms(x, g):
    return x * lax.rsqrt(jnp.mean(x * x, axis=-1, keepdims=True) + NORM_EPS) * g


def _params(*sem):
    return pltpu.CompilerParams(dimension_semantics=sem, vmem_limit_bytes=VMEM_LIMIT)


def _bf16_bits(x):
    return pltpu.bitcast(x.astype(jnp.bfloat16).astype(jnp.float32), jnp.uint32)


def _pack_pairs(t):
    return (_bf16_bits(t[:, :ROW_WORDS]) >> 16) | _bf16_bits(t[:, ROW_WORDS:])


def _pack_kernel(t_ref, o_ref):
    o_ref[...] = _pack_pairs(t_ref[...])


def _pack_table(tab):
    e, d = tab.shape
    rows = 512
    return pl.pallas_call(
        _pack_kernel,
        grid=(e // rows,),
        in_specs=[pl.BlockSpec((rows, d), lambda i: (i, 0))],
        out_specs=pl.BlockSpec((rows, d // 2), lambda i: (i, 0)),
        out_shape=jax.ShapeDtypeStruct((e, d // 2), jnp.uint32),
        compiler_params=_params("parallel"),
        name="pack_table",
    )(tab)


def _rope_kernel(pos_ref, inv_ref, sign_ref, c_ref, s_ref):
    ang = pos_ref[...].astype(jnp.float32) * inv_ref[...]
    c_ref[...] = jnp.cos(ang)
    s_ref[...] = jnp.sin(ang) * sign_ref[...]


def _rope_tables(positions):
    n = positions.size
    lane = jnp.arange(LANES)
    inv = ROPE_THETA ** (-(2 * (lane % (MLA_ROPE // 2))).astype(jnp.float32) / MLA_ROPE)
    sign = jnp.where((lane % MLA_ROPE) < MLA_ROPE // 2, -1.0, 1.0).astype(jnp.float32)
    tb = 1024
    out = jax.ShapeDtypeStruct((n, LANES), jnp.float32)
    return pl.pallas_call(
        _rope_kernel,
        grid=(n // tb,),
        in_specs=[pl.BlockSpec((tb, 1), lambda i: (i, 0)),
                  pl.BlockSpec((1, LANES), lambda i: (0, 0)),
                  pl.BlockSpec((1, LANES), lambda i: (0, 0))],
        out_specs=[pl.BlockSpec((tb, LANES), lambda i: (i, 0))] * 2,
        out_shape=[out, out],
        compiler_params=_params("parallel"),
        name="rope_tables",
    )(positions.reshape(n, 1), inv.reshape(1, LANES), sign.reshape(1, LANES))


def _mla_proj_kernel(x_ref, g_ref, c_ref, s_ref, win_ref, qn_ref, wq_ref, kvn_ref, wkv_ref,
                     q_out, k_out, v_out):
    hn = _rms(x_ref[...], g_ref[...]).astype(jnp.bfloat16)
    z = jnp.dot(hn, win_ref[...], preferred_element_type=jnp.float32)
    c = c_ref[...]
    s = s_ref[...]
    o_kv = MLA_Q_LORA
    o_kr = MLA_Q_LORA + MLA_KV_LORA
    k_rope = (z[:, o_kr:o_kr + LANES] * c + z[:, o_kr + LANES:o_kr + 2 * LANES] * s).astype(jnp.bfloat16)
    cq = _rms(z[:, :MLA_Q_LORA], qn_ref[...]).astype(jnp.bfloat16)
    q = jnp.dot(cq, wq_ref[...], preferred_element_type=jnp.float32)
    ckv = _rms(z[:, o_kv:o_kr], kvn_ref[...]).astype(jnp.bfloat16)
    kv = jnp.dot(ckv, wkv_ref[...], preferred_element_type=jnp.float32)
    scale = (MLA_NOPE + MLA_ROPE) ** -0.5
    hw = MLA_HEADS * LANES
    for h in range(MLA_HEADS):
        sl = slice(h * LANES, (h + 1) * LANES)
        q_out[h, :, :LANES] = (q[:, sl] * scale).astype(jnp.bfloat16)
        qr = q[:, hw + h * LANES:hw + (h + 1) * LANES] * c + q[:, 2 * hw + h * LANES:2 * hw + (h + 1) * LANES] * s
        q_out[h, :, LANES:] = (qr * scale).astype(jnp.bfloat16)
        k_out[h, :, :LANES] = kv[:, sl].astype(jnp.bfloat16)
        k_out[h, :, LANES:] = k_rope
        v_out[h] = kv[:, hw + h * LANES:hw + (h + 1) * LANES].astype(jnp.bfloat16)


def _swap_halves(w):
    half = w.shape[-1] // 2
    return jnp.concatenate([w[..., half:], w[..., :half]], axis=-1)


def _pad_lanes(w):
    return jnp.pad(w, [(0, 0)] * (w.ndim - 1) + [(0, LANES - w.shape[-1])])


def _mla_weights(w_in, w_q_up, w_kv_up):
    o_kr = MLA_Q_LORA + MLA_KV_LORA
    w_kr = w_in[:, o_kr:]
    win = jnp.concatenate([w_in[:, :o_kr], _pad_lanes(w_kr), _pad_lanes(_swap_halves(w_kr))], axis=1)
    wq = w_q_up.reshape(MLA_Q_LORA, MLA_HEADS, MLA_NOPE + MLA_ROPE)
    wq_n = wq[:, :, :MLA_NOPE].reshape(MLA_Q_LORA, -1)
    wq_r = _pad_lanes(wq[:, :, MLA_NOPE:]).reshape(MLA_Q_LORA, -1)
    wq_rs = _pad_lanes(_swap_halves(wq[:, :, MLA_NOPE:])).reshape(MLA_Q_LORA, -1)
    wqp = jnp.concatenate([wq_n, wq_r, wq_rs], axis=1)
    wkv = w_kv_up.reshape(MLA_KV_LORA, MLA_HEADS, MLA_NOPE + MLA_V)
    wkvp = jnp.concatenate([wkv[:, :, :MLA_NOPE].reshape(MLA_KV_LORA, -1),
                            wkv[:, :, MLA_NOPE:].reshape(MLA_KV_LORA, -1)], axis=1)
    return win.astype(jnp.bfloat16), wqp.astype(jnp.bfloat16), wkvp.astype(jnp.bfloat16)


def _mla_proj(h, g, cos_t, sin_t, win, q_norm, wq, kv_norm, wkv):
    n = h.shape[0]
    tb = TOKEN_BLOCK
    full = lambda a: pl.BlockSpec(a.shape, lambda i: (0,) * a.ndim)
    g = g.reshape(1, -1)
    q_norm = q_norm.reshape(1, -1)
    kv_norm = kv_norm.reshape(1, -1)
    qk_t = jax.ShapeDtypeStruct((MLA_HEADS, n, 2 * LANES), jnp.bfloat16)
    v_t = jax.ShapeDtypeStruct((MLA_HEADS, n, MLA_V), jnp.bfloat16)
    return pl.pallas_call(
        _mla_proj_kernel,
        grid=(n // tb,),
        in_specs=[pl.BlockSpec((tb, D_MODEL), lambda i: (i, 0)), full(g),
                  pl.BlockSpec((tb, LANES), lambda i: (i, 0)), pl.BlockSpec((tb, LANES), lambda i: (i, 0)),
                  full(win), full(q_norm), full(wq), full(kv_norm), full(wkv)],
        out_specs=[pl.BlockSpec((MLA_HEADS, tb, 2 * LANES), lambda i: (0, i, 0)),
                   pl.BlockSpec((MLA_HEADS, tb, 2 * LANES), lambda i: (0, i, 0)),
                   pl.BlockSpec((MLA_HEADS, tb, MLA_V), lambda i: (0, i, 0))],
        out_shape=[qk_t, qk_t, v_t],
        compiler_params=_params("parallel"),
        name="mla_proj",
    )(h, g, cos_t, sin_t, win, q_norm, wq, kv_norm, wkv)


def _attn_kernel(qi_ref, ki_ref, q_ref, k_ref, v_ref, o_ref, m_sc, l_sc, acc_sc):
    step = pl.program_id(1)
    qi = qi_ref[step]
    ki = ki_ref[step]

    @pl.when(ki == 0)
    def _():
        m_sc[...] = jnp.full(m_sc.shape, NEG_INF, jnp.float32)
        l_sc[...] = jnp.zeros(l_sc.shape, jnp.float32)
        acc_sc[...] = jnp.zeros(acc_sc.shape, jnp.float32)

    def update(masked):
        sc = lax.dot_general(q_ref[...], k_ref[...], (((1,), (1,)), ((), ())),
                             preferred_element_type=jnp.float32)
        if masked:
            row = lax.broadcasted_iota(jnp.int32, sc.shape, 0)
            col = lax.broadcasted_iota(jnp.int32, sc.shape, 1)
            sc = jnp.where(col <= row, sc, NEG_INF)
        m_prev = m_sc[...]
        m_next = jnp.maximum(m_prev, jnp.max(sc, axis=1, keepdims=True))
        p = jnp.exp(sc - m_next[:, :1])
        alpha = jnp.exp(m_prev - m_next)
        l_sc[...] = alpha * l_sc[...] + jnp.sum(p, axis=1, keepdims=True)
        acc_sc[...] = alpha * acc_sc[...] + jnp.dot(p.astype(jnp.bfloat16), v_ref[...],
                                                    preferred_element_type=jnp.float32)
        m_sc[...] = m_next

    @pl.when(ki < qi)
    def _():
        update(False)

    @pl.when(ki == qi)
    def _():
        update(True)
        o_ref[...] = (acc_sc[...] / l_sc[...]).astype(o_ref.dtype)


def _attention(q, k, v, seq, blk, b, q_lo, q_hi):
    nb = seq // blk
    qi = jnp.array([i for i in range(q_lo, q_hi) for _ in range(i + 1)], jnp.int32)
    ki = jnp.array([j for i in range(q_lo, q_hi) for j in range(i + 1)], jnp.int32)
    grid_spec = pltpu.PrefetchScalarGridSpec(
        num_scalar_prefetch=2,
        grid=(MLA_HEADS, qi.shape[0]),
        in_specs=[pl.BlockSpec((None, blk, 2 * LANES), lambda h, s, qi, ki: (h, b * nb + qi[s], 0)),
                  pl.BlockSpec((None, blk, 2 * LANES), lambda h, s, qi, ki: (h, b * nb + ki[s], 0)),
                  pl.BlockSpec((None, blk, MLA_V), lambda h, s, qi, ki: (h, b * nb + ki[s], 0))],
        out_specs=pl.BlockSpec((blk, MLA_V), lambda h, s, qi, ki: (qi[s] - q_lo, h)),
        scratch_shapes=[pltpu.VMEM((blk, LANES), jnp.float32), pltpu.VMEM((blk, LANES), jnp.float32),
                        pltpu.VMEM((blk, MLA_V), jnp.float32)],
    )
    return pl.pallas_call(
        _attn_kernel,
        grid_spec=grid_spec,
        out_shape=jax.ShapeDtypeStruct(((q_hi - q_lo) * blk, MLA_HEADS * MLA_V), jnp.bfloat16),
        compiler_params=_params("parallel", "arbitrary"),
        name="mla_attention",
    )(qi, ki, q, k, v)


def _top_rows(vals, ids, count, out_rows):
    t = vals.shape[1]
    big = jnp.int32(2 ** 30)
    orow = lax.broadcasted_iota(jnp.int32, (out_rows, t), 0)

    def body(r, carry):
        cur, ov, oi = carry
        m = jnp.max(cur, axis=0, keepdims=True)
        pick = jnp.min(jnp.where(cur == m, ids, big), axis=0, keepdims=True)
        cur = jnp.where(ids == pick, NEG_INF, cur)
        ov = jnp.where(orow == r, m, ov)
        oi = jnp.where(orow == r, pick, oi)
        return cur, ov, oi

    init = (vals, jnp.zeros((out_rows, t), jnp.float32), jnp.zeros((out_rows, t), jnp.int32))
    _, ov, oi = lax.fori_loop(0, count, body, init)
    return ov, oi


_ROW_SLABS = [(0, 0, 16), (1, 0, 8)] + [(a, 0, 8) for a in range(2, 8)]
_COL_SLAB = (8, 16, 0)
_PAIR_ROWS = sum(hi - lo for _, lo, hi in _ROW_SLABS) + (_COL_SLAB[1] - _COL_SLAB[0])


def _route_kernel(o_ref, h_ref, wo_ref, g_ref, wqt_ref, keys_ref, pos_ref,
                  hn_out, xn_out, eid_out, gate_out, qt_sc, v_sc, i_sc):
    tb = h_ref.shape[0]
    hnew = h_ref[...] + jnp.dot(o_ref[...], wo_ref[...], preferred_element_type=jnp.float32)
    hn_out[...] = hnew
    xn = _rms(hnew, g_ref[...])
    xn_out[...] = _pack_pairs(xn)
    qt_sc[...] = lax.dot_general(wqt_ref[...], xn.astype(jnp.bfloat16), (((1,), (1,)), ((), ())),
                                 preferred_element_type=jnp.float32).astype(jnp.bfloat16)
    key_ids = lax.broadcasted_iota(jnp.int32, (PEER_NKEYS, tb), 0)

    def group(g, carry):
        row0 = pl.multiple_of(g * PEER_HALF, PEER_HALF)
        st = jnp.dot(keys_ref[g], qt_sc[pl.ds(row0, PEER_HALF), :], preferred_element_type=jnp.float32)
        tv, ti = _top_rows(st, key_ids, PEER_TOPK, PEER_TOPK)
        out0 = pl.multiple_of(g * PEER_TOPK, PEER_TOPK)
        v_sc[pl.ds(out0, PEER_TOPK), :] = tv
        i_sc[pl.ds(out0, PEER_TOPK), :] = ti
        return carry

    lax.fori_loop(0, 2 * PEER_HEADS, group, 0)

    pos = pos_ref[...]

    def head(hd, carry):
        base = pl.multiple_of(hd * 2 * PEER_TOPK, 2 * PEER_TOPK)
        v1 = v_sc[pl.ds(base, PEER_TOPK), :]
        i1 = i_sc[pl.ds(base, PEER_TOPK), :]
        v2 = v_sc[pl.ds(base + PEER_TOPK, PEER_TOPK), :]
        i2 = i_sc[pl.ds(base + PEER_TOPK, PEER_TOPK), :]
        cv, ce = [], []
        for a, lo, hi in _ROW_SLABS:
            cv.append(v1[a:a + 1, :] + v2[lo:hi, :])
            ce.append(i1[a:a + 1, :] * PEER_NKEYS + i2[lo:hi, :])
        a_lo, a_hi, b = _COL_SLAB
        cv.append(v1[a_lo:a_hi, :] + v2[b:b + 1, :])
        ce.append(i1[a_lo:a_hi, :] * PEER_NKEYS + i2[b:b + 1, :])
        cv = jnp.concatenate(cv, axis=0)
        ce = jnp.concatenate(ce, axis=0)
        tv, tp = _top_rows(cv, jnp.broadcast_to(pos, cv.shape), PEER_TOPK, PEER_TOPK)
        te = jnp.zeros((PEER_TOPK, tb), jnp.int32)
        orow = lax.broadcasted_iota(jnp.int32, (PEER_TOPK, tb), 0)
        for r in range(PEER_TOPK):
            e_r = jnp.sum(jnp.where(pos == tp[r:r + 1, :], ce, 0), axis=0, keepdims=True)
            te = jnp.where(orow == r, e_r, te)
        ex = jnp.exp(tv - tv[0:1, :])
        gates = ex / jnp.sum(ex, axis=0, keepdims=True)
        out0 = pl.multiple_of(hd * PEER_TOPK, PEER_TOPK)
        v_sc[pl.ds(out0, PEER_TOPK), :] = gates
        i_sc[pl.ds(out0, PEER_TOPK), :] = te
        return carry

    lax.fori_loop(0, PEER_HEADS, head, 0)

    eid_out[...] = jnp.transpose(i_sc[:PEER_SLOTS, :].astype(jnp.float32)).astype(jnp.int32)
    gate_out[...] = jnp.transpose(v_sc[:PEER_SLOTS, :])


def _pair_positions():
    pos = [a * PEER_TOPK + b for a, lo, hi in _ROW_SLABS for b in range(lo, hi)]
    a_lo, a_hi, b = _COL_SLAB
    pos += [a * PEER_TOPK + b for a in range(a_lo, a_hi)]
    return jnp.array(pos, jnp.int32).reshape(_PAIR_ROWS, 1)


def _route(o, h, h_row0, w_out, g, wqt, keys):
    n = o.shape[0]
    tb = TOKEN_BLOCK
    full = lambda a: pl.BlockSpec(a.shape, lambda i: (0,) * a.ndim)
    g = g.reshape(1, -1)
    pos = _pair_positions()
    row = pl.BlockSpec((tb, D_MODEL), lambda i: (i, 0))
    h_row = pl.BlockSpec((tb, D_MODEL), lambda i: (i + h_row0 // tb, 0))
    f32 = jnp.float32
    return pl.pallas_call(
        _route_kernel,
        grid=(n // tb,),
        in_specs=[row, h_row, full(w_out), full(g), full(wqt), full(keys), full(pos)],
        out_specs=[row, pl.BlockSpec((tb, ROW_WORDS), lambda i: (i, 0)),
                   pl.BlockSpec((tb, PEER_SLOTS), lambda i: (i, 0)),
                   pl.BlockSpec((tb, PEER_SLOTS), lambda i: (i, 0))],
        out_shape=[jax.ShapeDtypeStruct((n, D_MODEL), f32), jax.ShapeDtypeStruct((n, ROW_WORDS), jnp.uint32),
                   jax.ShapeDtypeStruct((n, PEER_SLOTS), jnp.int32),
                   jax.ShapeDtypeStruct((n, PEER_SLOTS), f32)],
        scratch_shapes=[pltpu.VMEM((2 * PEER_HEADS * PEER_HALF, tb), jnp.bfloat16),
                        pltpu.VMEM((2 * PEER_HEADS * PEER_TOPK, tb), f32),
                        pltpu.VMEM((2 * PEER_HEADS * PEER_TOPK, tb), jnp.int32)],
        compiler_params=_params("parallel"),
        name="peer_route",
    )(o, h, w_out, g, wqt, keys, pos)


def _sc_mesh():
    return plsc.VectorSubcoreMesh(core_axis_name="c", subcore_axis_name="s")


def _sc_params():
    return pltpu.CompilerParams(needs_layout_passes=False)


def _worker_id():
    return lax.axis_index("s") * SC_CORES + lax.axis_index("c")


def _gather_rows(tab_hbm, idx_ref, dst_ref, sem):
    return pltpu.make_async_copy(tab_hbm.at[idx_ref], dst_ref, sem)


def _unpack16(word):
    lo = lax.bitcast_convert_type(word << 16, jnp.float32)
    hi = lax.bitcast_convert_type(word & jnp.uint32(0xFFFF0000), jnp.float32)
    return lo, hi


def _as_bf16(word):
    return plsc.bitcast(word, jnp.bfloat16)


def _sc_dots(tab, idx, x):
    t = x.shape[0]
    workers = SC_CORES * SC_SUBCORES
    tok_w = t // workers
    per_worker = tok_w * PEER_SLOTS
    w = GATHER_WINDOW
    slots = DOTS_SLOTS
    n_win = per_worker // w
    win_per_tok = PEER_SLOTS // w
    n_chunks = ROW_WORDS // SC_LANES
    f32 = jnp.float32

    @functools.partial(
        pl.kernel, out_type=jax.ShapeDtypeStruct((t * PEER_SLOTS,), f32), mesh=_sc_mesh(),
        scratch_types=[pltpu.VMEM((per_worker,), jnp.int32),
                       pltpu.VMEM((tok_w, ROW_WORDS), jnp.uint32),
                       pltpu.VMEM((slots, w, ROW_WORDS), jnp.uint32),
                       pltpu.VMEM((per_worker,), f32),
                       pltpu.VMEM((w * SC_LANES,), f32),
                       pltpu.SemaphoreType.DMA((slots,))],
        compiler_params=_sc_params(), name="peer_dots")
    def dots(tab_hbm, i_hbm, x_hbm, act_hbm, idx_v, x_v, rows, act_v, part_v, sem):
        wid = _worker_id()
        base = pl.multiple_of(wid * per_worker, per_worker)
        tok0 = pl.multiple_of(wid * tok_w, tok_w)
        pltpu.sync_copy(i_hbm.at[pl.ds(base, per_worker)], idx_v)
        pltpu.sync_copy(x_hbm.at[pl.ds(tok0, tok_w)], x_v)

        def gather(win, slot):
            ix = idx_v.at[pl.ds(pl.multiple_of(win * w, w), w)]
            return _gather_rows(tab_hbm, ix, rows.at[slot], sem.at[slot])

        for s in range(slots - 1):
            gather(s, s).start()
        lane = lax.iota(jnp.int32, SC_LANES)

        @pl.loop(0, n_win)
        def _(win):
            slot = lax.rem(win, slots)
            nxt = win + slots - 1

            @pl.when(nxt < n_win)
            def _():
                gather(nxt, lax.rem(nxt, slots)).start()

            gather(win, slot).wait()
            tok = win // win_per_tok

            @pl.loop(0, w // DOT_ROWS)
            def _(g):
                r0 = g * DOT_ROWS
                acc_lo = [jnp.zeros((SC_LANES,), f32) for _ in range(DOT_ROWS)]
                acc_hi = [jnp.zeros((SC_LANES,), f32) for _ in range(DOT_ROWS)]
                for c in range(0, n_chunks, 2):
                    xa = _as_bf16(x_v[tok, pl.ds(c * SC_LANES, SC_LANES)])
                    xb = _as_bf16(x_v[tok, pl.ds((c + 1) * SC_LANES, SC_LANES)])
                    for r in range(DOT_ROWS):
                        ra = _as_bf16(rows[slot, r0 + r, pl.ds(c * SC_LANES, SC_LANES)])
                        rb = _as_bf16(rows[slot, r0 + r, pl.ds((c + 1) * SC_LANES, SC_LANES)])
                        lo, hi = _unpack16(plsc.bitcast(ra * xa + rb * xb, jnp.uint32))
                        acc_lo[r] = acc_lo[r] + lo
                        acc_hi[r] = acc_hi[r] + hi
                for r in range(DOT_ROWS):
                    part_v[pl.ds(pl.multiple_of((r0 + r) * SC_LANES, SC_LANES), SC_LANES)] = acc_lo[r] + acc_hi[r]

            for blk in range(w // SC_LANES):
                res = jnp.zeros((SC_LANES,), f32)
                for l in range(SC_LANES):
                    res = res + plsc.load_gather(part_v, [lane * SC_LANES + (blk * SC_LANES * SC_LANES + l)])
                act_v[pl.ds(pl.multiple_of(win * w + blk * SC_LANES, SC_LANES), SC_LANES)] = res

        pltpu.sync_copy(act_v, act_hbm.at[pl.ds(base, per_worker)])

    return dots(tab, idx, x)


def _sc_axpy(tab, idx, wts):
    p = idx.shape[0]
    t = p // PEER_SLOTS
    workers = SC_CORES * SC_SUBCORES
    tok_w = t // workers
    per_worker = tok_w * PEER_SLOTS
    w = GATHER_WINDOW
    slots = AXPY_SLOTS
    n_win = per_worker // w
    win_per_tok = PEER_SLOTS // w
    passes = ROW_WORDS // (SC_LANES * AXPY_CHUNKS)
    f32 = jnp.float32

    @functools.partial(
        pl.kernel, out_type=jax.ShapeDtypeStruct((t, D_MODEL), f32), mesh=_sc_mesh(),
        scratch_types=[pltpu.VMEM((per_worker,), jnp.int32),
                       pltpu.VMEM((per_worker,), jnp.uint32),
                       pltpu.VMEM((slots, w, ROW_WORDS), jnp.uint32),
                       pltpu.VMEM((2, D_MODEL), f32),
                       pltpu.SemaphoreType.DMA((slots,)),
                       pltpu.SemaphoreType.DMA((2,))],
        compiler_params=_sc_params(), name="peer_axpy")
    def axpy(tab_hbm, i_hbm, w_hbm, y_hbm, idx_v, w_v, rows, y_v, sem, sem_y):
        wid = _worker_id()
        base = pl.multiple_of(wid * per_worker, per_worker)
        tok0 = wid * tok_w
        pltpu.sync_copy(i_hbm.at[pl.ds(base, per_worker)], idx_v)
        pltpu.sync_copy(w_hbm.at[pl.ds(base, per_worker)], w_v)

        def gather(win, slot):
            ix = idx_v.at[pl.ds(pl.multiple_of(win * w, w), w)]
            return _gather_rows(tab_hbm, ix, rows.at[slot], sem.at[slot])

        def y_write(tok, buf):
            return pltpu.make_async_copy(y_v.at[buf], y_hbm.at[tok0 + tok], sem_y.at[buf])

        for s in range(slots - 1):
            gather(s, s).start()

        @pl.loop(0, n_win)
        def _(win):
            slot = lax.rem(win, slots)
            nxt = win + slots - 1

            @pl.when(nxt < n_win)
            def _():
                gather(nxt, lax.rem(nxt, slots)).start()

            gather(win, slot).wait()
            tok = win // win_per_tok
            part = lax.rem(win, win_per_tok)
            buf = lax.rem(tok, 2)

            @pl.when(part == 0)
            def _():
                @pl.when(tok >= 2)
                def _():
                    y_write(tok - 2, buf).wait()

                for c in range(D_MODEL // SC_LANES):
                    y_v[buf, pl.ds(c * SC_LANES, SC_LANES)] = jnp.zeros((SC_LANES,), f32)

            for ps in range(passes):

                def group(g, accs):
                    accs = list(accs)
                    row0 = pl.multiple_of(g * SC_LANES, SC_LANES)
                    w_grp = w_v[pl.ds(pl.multiple_of(win * w + row0, SC_LANES), SC_LANES)]
                    for k in range(0, SC_LANES, 2):
                        wa = _as_bf16(jnp.take_along_axis(w_grp, jnp.full((SC_LANES,), k, jnp.int32), axis=0))
                        wb = _as_bf16(jnp.take_along_axis(w_grp, jnp.full((SC_LANES,), k + 1, jnp.int32), axis=0))
                        for c in range(AXPY_CHUNKS):
                            col = (ps * AXPY_CHUNKS + c) * SC_LANES
                            ra = _as_bf16(rows[slot, row0 + k, pl.ds(col, SC_LANES)])
                            rb = _as_bf16(rows[slot, row0 + k + 1, pl.ds(col, SC_LANES)])
                            lo, hi = _unpack16(plsc.bitcast(ra * wa + rb * wb, jnp.uint32))
                            accs[2 * c] = accs[2 * c] + lo
                            accs[2 * c + 1] = accs[2 * c + 1] + hi
                    return tuple(accs)

                zero = tuple(jnp.zeros((SC_LANES,), f32) for _ in range(2 * AXPY_CHUNKS))
                accs = lax.fori_loop(0, w // SC_LANES, group, zero)
                for c in range(AXPY_CHUNKS):
                    col = (ps * AXPY_CHUNKS + c) * SC_LANES
                    y_v[buf, pl.ds(col, SC_LANES)] = y_v[buf, pl.ds(col, SC_LANES)] + accs[2 * c]
                    y_v[buf, pl.ds(ROW_WORDS + col, SC_LANES)] = (
                        y_v[buf, pl.ds(ROW_WORDS + col, SC_LANES)] + accs[2 * c + 1])

            @pl.when(part == win_per_tok - 1)
            def _():
                y_write(tok, buf).start()

        for tok in (tok_w - 2, tok_w - 1):
            y_write(tok, tok % 2).wait()

    return axpy(tab, idx, wts)


def _slot_weight_kernel(act_ref, gate_ref, o_ref):
    act = act_ref[...]
    wts = gate_ref[...] * (0.5 * act * (1.0 + lax.erf(act * (2.0 ** -0.5))))
    hi = _bf16_bits(wts)
    o_ref[...] = hi | (hi >> 16)


def _slot_weights(act, gates):
    n = gates.shape[0]
    tb = 1024
    blk = pl.BlockSpec((tb, PEER_SLOTS), lambda i: (i, 0))
    return pl.pallas_call(
        _slot_weight_kernel, grid=(n // tb,), in_specs=[blk, blk], out_specs=blk,
        out_shape=jax.ShapeDtypeStruct((n, PEER_SLOTS), jnp.uint32),
        compiler_params=_params("parallel"), name="peer_slot_weights",
    )(act, gates)


def _residual_kernel(h_ref, y_ref, g_ref, o_ref, *, final_norm):
    out = h_ref[...] + y_ref[...]
    o_ref[...] = _rms(out, g_ref[...]) if final_norm else out


def _residual(h, y, g_final, final_norm):
    n = h.shape[0]
    tb = 512
    blk = pl.BlockSpec((tb, D_MODEL), lambda i: (i, 0))
    g_final = g_final.reshape(1, -1)
    return pl.pallas_call(
        functools.partial(_residual_kernel, final_norm=final_norm),
        grid=(n // tb,), in_specs=[blk, blk, pl.BlockSpec((1, D_MODEL), lambda i: (0, 0))], out_specs=blk,
        out_shape=jax.ShapeDtypeStruct((n, D_MODEL), jnp.float32),
        compiler_params=_params("parallel"), name="peer_residual",
    )(h, y, g_final)


def _peer(xn, eid, gates, h, tab_u, tab_v, g_final, final_norm):
    n = xn.shape[0]
    tc = min(GATHER_TOKENS, n)
    chunks = [slice(c * tc, (c + 1) * tc) for c in range(n // tc)]
    act = jnp.concatenate([_sc_dots(tab_u, eid[tok].reshape(-1), xn[tok]) for tok in chunks])
    wts = _slot_weights(act.reshape(n, PEER_SLOTS), gates)
    y = jnp.concatenate([_sc_axpy(tab_v, eid[tok].reshape(-1), wts[tok].reshape(-1)) for tok in chunks])
    return _residual(h, y, g_final, final_norm)


_N_SUB = HG_CHUNK // HG_SUB
_OFF_PAIRS = [(i, j) for i in range(_N_SUB) for j in range(i)]


def _cum_matrix():
    t = jnp.arange(HG_CHUNK)[:, None]
    r = jnp.arange(HG_CHUNK)[None, :]
    sub = t // HG_SUB
    incl = r <= t
    before = r < sub * HG_SUB
    end = r < (sub + 1) * HG_SUB
    return jnp.concatenate([incl, before, end], axis=0).astype(jnp.float32)


def _hgrn_kernel(h_ref, g_ref, w_ref, lb_ref, on_ref, cum_ref, st_in_ref, o_ref, st_out_ref,
                 z_sc, st_sc, lb_sc):
    @pl.when(pl.program_id(0) == 0)
    def _():
        st_sc[...] = st_in_ref[...]

    tb = h_ref.shape[0]
    hn = _rms(h_ref[...], g_ref[...]).astype(jnp.bfloat16)
    z_sc[...] = jnp.dot(hn, w_ref[...], preferred_element_type=jnp.float32)
    lbr = lb_ref[...]
    mx = jnp.max(lbr, axis=0, keepdims=True)
    ex = jnp.exp(lbr - mx)
    prob = ex / jnp.sum(ex, axis=0, keepdims=True)
    lb_sc[...] = jnp.broadcast_to((prob[0:1, :] + prob[1:2, :]) - prob[0:1, :], lb_sc.shape)
    wf = HG_HEADS * HG_DK
    sub_row = lax.broadcasted_iota(jnp.int32, (HG_SUB, HG_DK), 0)
    lane64 = lax.broadcasted_iota(jnp.int32, (HG_SUB, HG_CHUNK), 1)

    mxu = jnp.bfloat16
    nt = (((1,), (1,)), ((), ()))

    def head(hd, carry):
        c0 = pl.multiple_of(hd * HG_DK, HG_DK)
        lb = lb_sc[0:1, pl.ds(c0, HG_DK)]
        on = on_ref[0:1, pl.ds(c0, HG_DK)]
        st = st_sc[hd]
        for ch in range(tb // HG_CHUNK):
            rows = slice(ch * HG_CHUNK, (ch + 1) * HG_CHUNK)
            qp = z_sc[rows, pl.ds(c0, HG_DK)]
            fp = z_sc[rows, pl.ds(pl.multiple_of(wf + c0, HG_DK), HG_DK)]
            v = z_sc[rows, pl.ds(pl.multiple_of(2 * wf + c0, HG_DK), HG_DK)]
            gp = z_sc[rows, pl.ds(pl.multiple_of(2 * wf + HG_HEADS * HG_DV + c0, HG_DK), HG_DK)]
            f = lb + (1.0 - lb) * jax.nn.sigmoid(fp)
            lf = jnp.log(f)
            k = 1.0 - f
            q = qp * jax.nn.sigmoid(qp)
            cums = jnp.dot(cum_ref[...], lf, precision=lax.Precision.HIGHEST,
                           preferred_element_type=jnp.float32)
            b = cums[:HG_CHUNK]
            b_start = cums[HG_CHUNK:2 * HG_CHUNK]
            b_end = cums[2 * HG_CHUNK:]
            q_hat = q * jnp.exp(b - b_start)
            k_hat = k * jnp.exp(b_end - b)
            o_inter = lax.dot_general((q_hat * jnp.exp(b_start)).astype(mxu), st.astype(mxu), nt,
                                      preferred_element_type=jnp.float32)
            stacked = []
            for (i, j) in _OFF_PAIRS:
                d_ij = jnp.exp(b_start[i * HG_SUB:i * HG_SUB + 1, :] - b_end[j * HG_SUB:j * HG_SUB + 1, :])
                stacked.append(q_hat[i * HG_SUB:(i + 1) * HG_SUB, :] * d_ij)
            stacked = jnp.concatenate(stacked, axis=0).astype(mxu)
            off = lax.dot_general(stacked, k_hat.astype(mxu), nt,
                                  preferred_element_type=jnp.float32)
            a_rows = []
            for i in range(_N_SUB):
                blk = jnp.zeros((HG_SUB, HG_CHUNK), jnp.float32)
                for p, (pi, pj) in enumerate(_OFF_PAIRS):
                    if pi == i:
                        in_j = (lane64 >= pj * HG_SUB) & (lane64 < (pj + 1) * HG_SUB)
                        blk = jnp.where(in_j, off[p * HG_SUB:(p + 1) * HG_SUB, :], blk)
                b_blk = b[i * HG_SUB:(i + 1) * HG_SUB, :]
                q_blk = q[i * HG_SUB:(i + 1) * HG_SUB, :]
                for s in range(HG_SUB):
                    n = i * HG_SUB + s
                    e = jnp.exp(jnp.where(sub_row >= s, b_blk - b[n:n + 1, :], MASKED_LOG_DECAY))
                    col = jnp.sum(q_blk * k[n:n + 1, :] * e, axis=1, keepdims=True)
                    blk = jnp.where(lane64 == n, col, blk)
                a_rows.append(blk)
            a = jnp.concatenate(a_rows, axis=0).astype(mxu)
            o = o_inter + jnp.dot(a, v.astype(mxu), preferred_element_type=jnp.float32)
            b_last = b[HG_CHUNK - 1:HG_CHUNK, :]
            k_til = (k_hat * jnp.exp(b_last - b_end)).astype(mxu)
            upd = lax.dot_general(v.astype(mxu), k_til, (((0,), (0,)), ((), ())),
                                  preferred_element_type=jnp.float32)
            st = st * jnp.exp(b_last) + upd
            o = o * lax.rsqrt(jnp.mean(o * o, axis=-1, keepdims=True) + NORM_EPS)
            o = o * on * (gp * jax.nn.sigmoid(gp))
            o_ref[rows, pl.ds(c0, HG_DK)] = o.astype(o_ref.dtype)
        st_sc[hd] = st
        return carry

    lax.fori_loop(0, HG_HEADS, head, 0)

    @pl.when(pl.program_id(0) == pl.num_programs(0) - 1)
    def _():
        st_out_ref[...] = st_sc[...]


def _hgrn(h, state, g, w, lb_raw, out_norm):
    n = h.shape[0]
    tb = min(HG_BLOCK, n)
    g = g.reshape(1, -1)
    out_norm = out_norm.reshape(1, -1)
    cum = _cum_matrix()
    full = lambda a: pl.BlockSpec(a.shape, lambda i: (0,) * a.ndim)
    f32 = jnp.float32
    return pl.pallas_call(
        _hgrn_kernel,
        grid=(n // tb,),
        in_specs=[pl.BlockSpec((tb, D_MODEL), lambda i: (i, 0)), full(g), full(w),
                  full(lb_raw), full(out_norm), full(cum), full(state)],
        out_specs=[pl.BlockSpec((tb, HG_HEADS * HG_DV), lambda i: (i, 0)), full(state)],
        out_shape=[jax.ShapeDtypeStruct((n, HG_HEADS * HG_DV), jnp.bfloat16),
                   jax.ShapeDtypeStruct(state.shape, f32)],
        scratch_shapes=[pltpu.VMEM((tb, w.shape[1]), f32),
                        pltpu.VMEM((HG_HEADS, HG_DV, HG_DK), f32),
                        pltpu.VMEM((8, HG_HEADS * HG_DK), f32)],
        compiler_params=_params("arbitrary"),
        name="hgrn2",
    )(h, g, w, lb_raw, out_norm, cum, state)


def kernel(x, positions, ln_mix, ln_ffn, ln_final, mla_w_in, mla_q_norm, mla_w_q_up, mla_kv_norm,
           mla_w_kv_up, mla_w_out, hg_w_in, hg_lb, hg_out_norm, hg_w_out, peer_w_q, peer_sub_keys,
           peer_u, peer_v):
    batch, seq, d = x.shape
    n = batch * seq
    bf16 = jnp.bfloat16
    h = x.reshape(n, d)

    def route_weights(i):
        keys = peer_sub_keys[i].reshape(2 * PEER_HEADS, PEER_NKEYS, PEER_HALF).astype(bf16)
        return peer_w_q[i].T.astype(bf16), keys

    tables = [(_pack_table(peer_u[i]), _pack_table(peer_v[i])) for i in range(2)]
    cos_t, sin_t = _rope_tables(positions)
    win, wq, wkv = _mla_weights(mla_w_in[0], mla_w_q_up[0], mla_w_kv_up[0])
    q, k, v = _mla_proj(h, ln_mix[0], cos_t, sin_t, win, mla_q_norm[0], wq, mla_kv_norm[0], wkv)
    wqt0, keys0 = route_weights(0)
    wqt1, keys1 = route_weights(1)
    mla_wo = mla_w_out[0].astype(bf16)
    hg_wi = hg_w_in[0].astype(bf16)
    hg_wo = hg_w_out[0].astype(bf16)

    chunk = min(PIPE_TOKENS, seq)
    blk = min(ATTN_BLOCK, chunk)
    n_chunks = seq // chunk
    states = [jnp.zeros((HG_HEADS, HG_DV, HG_DK), jnp.float32) for _ in range(batch)]
    outs = {}

    def layer0(b, j):
        o = _attention(q, k, v, seq, blk, b, j * chunk // blk, (j + 1) * chunk // blk)
        hc, xp, eid, gates = _route(o, h, b * seq + j * chunk, mla_wo, ln_ffn[0], wqt0, keys0)
        return _peer(xp, eid, gates, hc, tables[0][0], tables[0][1], ln_final, False)

    def layer1(b, j, hc):
        o, states[b] = _hgrn(hc, states[b], ln_mix[1], hg_wi, hg_lb, hg_out_norm[0])
        hc, xp, eid, gates = _route(o, hc, 0, hg_wo, ln_ffn[1], wqt1, keys1)
        outs[(b, j)] = _peer(xp, eid, gates, hc, tables[1][0], tables[1][1], ln_final, True)

    pending = None
    for j in range(n_chunks):
        for b in range(batch):
            hc = layer0(b, j)
            if pending is not None:
                layer1(*pending)
            pending = (b, j, hc)
    layer1(*pending)
    out = jnp.concatenate([outs[(b, j)] for b in range(batch) for j in range(n_chunks)], axis=0)
    return out.reshape(batch, seq, d)
```

```python
import functools

import jax
import jax.numpy as jnp
from jax import lax
from jax.experimental import pallas as pl
from jax.experimental.pallas import tpu as pltpu
from jax.experimental.pallas import tpu_sc as plsc

D_MODEL = 1024
NORM_EPS = 1e-6
MLA_HEADS = 8
MLA_Q_LORA = 384
MLA_KV_LORA = 256
MLA_NOPE = 128
MLA_ROPE = 64
MLA_V = 128
ROPE_THETA = 10000.0
HG_HEADS = 8
HG_DK = 128
HG_DV = 128
PEER_HEADS = 8
PEER_NKEYS = 128
PEER_HALF = 128
PEER_TOPK = 16
PEER_SLOTS = PEER_HEADS * PEER_TOPK

LANES = 128
SC_CORES = 2
SC_SUBCORES = 16
VMEM_LIMIT = 48 * 1024 * 1024

ROW_WORDS = D_MODEL // 2
TOKEN_BLOCK = 256
ATTN_BLOCK = 1024
HG_BLOCK = 256
HG_CHUNK = 64
HG_SUB = 16
PIPE_TOKENS = 2048
GATHER_WINDOW = 32
DOTS_SLOTS = 5
AXPY_SLOTS = 7
GATHER_TOKENS = 1024
SC_LANES = 16
DOT_ROWS = 8
AXPY_CHUNKS = 8
NEG_INF = float("-inf")
MASKED_LOG_DECAY = -1e30


def _rms(x, g):
    return x * lax.rsqrt(jnp.mean(x * x, axis=-1, keepdims=True) + NORM_EPS) * g


def _params(*sem):
    return pltpu.CompilerParams(dimension_semantics=sem, vmem_limit_bytes=VMEM_LIMIT)


def _bf16_bits(x):
    return pltpu.bitcast(x.astype(jnp.bfloat16).astype(jnp.float32), jnp.uint32)


def _pack_pairs(t):
    return (_bf16_bits(t[:, :ROW_WORDS]) >> 16) | _bf16_bits(t[:, ROW_WORDS:])


def _pack_kernel(t_ref, o_ref):
    o_ref[...] = _pack_pairs(t_ref[...])


def _pack_table(tab):
    e, d = tab.shape
    rows = 512
    return pl.pallas_call(
        _pack_kernel,
        grid=(e // rows,),
        in_specs=[pl.BlockSpec((rows, d), lambda i: (i, 0))],
        out_specs=pl.BlockSpec((rows, d // 2), lambda i: (i, 0)),
        out_shape=jax.ShapeDtypeStruct((e, d // 2), jnp.uint32),
        compiler_params=_params("parallel"),
        name="pack_table",
    )(tab)


def _rope_kernel(pos_ref, inv_ref, sign_ref, c_ref, s_ref):
    ang = pos_ref[...].astype(jnp.float32) * inv_ref[...]
    c_ref[...] = jnp.cos(ang)
    s_ref[...] = jnp.sin(ang) * sign_ref[...]


def _rope_tables(positions):
    n = positions.size
    lane = jnp.arange(LANES)
    inv = ROPE_THETA ** (-(2 * (lane % (MLA_ROPE // 2))).astype(jnp.float32) / MLA_ROPE)
    sign = jnp.where((lane % MLA_ROPE) < MLA_ROPE // 2, -1.0, 1.0).astype(jnp.float32)
    tb = 1024
    out = jax.ShapeDtypeStruct((n, LANES), jnp.float32)
    return pl.pallas_call(
        _rope_kernel,
        grid=(n // tb,),
        in_specs=[pl.BlockSpec((tb, 1), lambda i: (i, 0)),
                  pl.BlockSpec((1, LANES), lambda i: (0, 0)),
                  pl.BlockSpec((1, LANES), lambda i: (0, 0))],
        out_specs=[pl.BlockSpec((tb, LANES), lambda i: (i, 0))] * 2,
        out_shape=[out, out],
        compiler_params=_params("parallel"),
        name="rope_tables",
    )(positions.reshape(n, 1), inv.reshape(1, LANES), sign.reshape(1, LANES))


def _mla_proj_kernel(x_ref, g_ref, c_ref, s_ref, win_ref, qn_ref, wq_ref, kvn_ref, wkv_ref,
                     q_out, k_out, v_out):
    hn = _rms(x_ref[...], g_ref[...]).astype(jnp.bfloat16)
    z = jnp.dot(hn, win_ref[...], preferred_element_type=jnp.float32)
    c = c_ref[...]
    s = s_ref[...]
    o_kv = MLA_Q_LORA
    o_kr = MLA_Q_LORA + MLA_KV_LORA
    k_rope = (z[:, o_kr:o_kr + LANES] * c + z[:, o_kr + LANES:o_kr + 2 * LANES] * s).astype(jnp.bfloat16)
    cq = _rms(z[:, :MLA_Q_LORA], qn_ref[...]).astype(jnp.bfloat16)
    q = jnp.dot(cq, wq_ref[...], preferred_element_type=jnp.float32)
    ckv = _rms(z[:, o_kv:o_kr], kvn_ref[...]).astype(jnp.bfloat16)
    kv = jnp.dot(ckv, wkv_ref[...], preferred_element_type=jnp.float32)
    scale = (MLA_NOPE + MLA_ROPE) ** -0.5
    hw = MLA_HEADS * LANES
    for h in range(MLA_HEADS):
        sl = slice(h * LANES, (h + 1) * LANES)
        q_out[h, :, :LANES] = (q[:, sl] * scale).astype(jnp.bfloat16)
        qr = q[:, hw + h * LANES:hw + (h + 1) * LANES] * c + q[:, 2 * hw + h * LANES:2 * hw + (h + 1) * LANES] * s
        q_out[h, :, LANES:] = (qr * scale).astype(jnp.bfloat16)
        k_out[h, :, :LANES] = kv[:, sl].astype(jnp.bfloat16)
        k_out[h, :, LANES:] = k_rope
        v_out[h] = kv[:, hw + h * LANES:hw + (h + 1) * LANES].astype(jnp.bfloat16)


def _swap_halves(w):
    half = w.shape[-1] // 2
    return jnp.concatenate([w[..., half:], w[..., :half]], axis=-1)


def _pad_lanes(w):
    return jnp.pad(w, [(0, 0)] * (w.ndim - 1) + [(0, LANES - w.shape[-1])])


def _mla_weights(w_in, w_q_up, w_kv_up):
    o_kr = MLA_Q_LORA + MLA_KV_LORA
    w_kr = w_in[:, o_kr:]
    win = jnp.concatenate([w_in[:, :o_kr], _pad_lanes(w_kr), _pad_lanes(_swap_halves(w_kr))], axis=1)
    wq = w_q_up.reshape(MLA_Q_LORA, MLA_HEADS, MLA_NOPE + MLA_ROPE)
    wq_n = wq[:, :, :MLA_NOPE].reshape(MLA_Q_LORA, -1)
    wq_r = _pad_lanes(wq[:, :, MLA_NOPE:]).reshape(MLA_Q_LORA, -1)
    wq_rs = _pad_lanes(_swap_halves(wq[:, :, MLA_NOPE:])).reshape(MLA_Q_LORA, -1)
    wqp = jnp.concatenate([wq_n, wq_r, wq_rs], axis=1)
    wkv = w_kv_up.reshape(MLA_KV_LORA, MLA_HEADS, MLA_NOPE + MLA_V)
    wkvp = jnp.concatenate([wkv[:, :, :MLA_NOPE].reshape(MLA_KV_LORA, -1),
                            wkv[:, :, MLA_NOPE:].reshape(MLA_KV_LORA, -1)], axis=1)
    return win.astype(jnp.bfloat16), wqp.astype(jnp.bfloat16), wkvp.astype(jnp.bfloat16)


def _mla_proj(h, g, cos_t, sin_t, win, q_norm, wq, kv_norm, wkv):
    n = h.shape[0]
    tb = TOKEN_BLOCK
    full = lambda a: pl.BlockSpec(a.shape, lambda i: (0,) * a.ndim)
    g = g.reshape(1, -1)
    q_norm = q_norm.reshape(1, -1)
    kv_norm = kv_norm.reshape(1, -1)
    qk_t = jax.ShapeDtypeStruct((MLA_HEADS, n, 2 * LANES), jnp.bfloat16)
    v_t = jax.ShapeDtypeStruct((MLA_HEADS, n, MLA_V), jnp.bfloat16)
    return pl.pallas_call(
        _mla_proj_kernel,
        grid=(n // tb,),
        in_specs=[pl.BlockSpec((tb, D_MODEL), lambda i: (i, 0)), full(g),
                  pl.BlockSpec((tb, LANES), lambda i: (i, 0)), pl.BlockSpec((tb, LANES), lambda i: (i, 0)),
                  full(win), full(q_norm), full(wq), full(kv_norm), full(wkv)],
        out_specs=[pl.BlockSpec((MLA_HEADS, tb, 2 * LANES), lambda i: (0, i, 0)),
                   pl.BlockSpec((MLA_HEADS, tb, 2 * LANES), lambda i: (0, i, 0)),
                   pl.BlockSpec((MLA_HEADS, tb, MLA_V), lambda i: (0, i, 0))],
        out_shape=[qk_t, qk_t, v_t],
        compiler_params=_params("parallel"),
        name="mla_proj",
    )(h, g, cos_t, sin_t, win, q_norm, wq, kv_norm, wkv)


def _attn_kernel(qi_ref, ki_ref, q_ref, k_ref, v_ref, o_ref, m_sc, l_sc, acc_sc):
    step = pl.program_id(1)
    qi = qi_ref[step]
    ki = ki_ref[step]

    @pl.when(ki == 0)
    def _():
        m_sc[...] = jnp.full(m_sc.shape, NEG_INF, jnp.float32)
        l_sc[...] = jnp.zeros(l_sc.shape, jnp.float32)
        acc_sc[...] = jnp.zeros(acc_sc.shape, jnp.float32)

    def update(masked):
        sc = lax.dot_general(q_ref[...], k_ref[...], (((1,), (1,)), ((), ())),
                             preferred_element_type=jnp.float32)
        if masked:
            row = lax.broadcasted_iota(jnp.int32, sc.shape, 0)
            col = lax.broadcasted_iota(jnp.int32, sc.shape, 1)
            sc = jnp.where(col <= row, sc, NEG_INF)
        m_prev = m_sc[...]
        m_next = jnp.maximum(m_prev, jnp.max(sc, axis=1, keepdims=True))
        p = jnp.exp(sc - m_next[:, :1])
        alpha = jnp.exp(m_prev - m_next)
        l_sc[...] = alpha * l_sc[...] + jnp.sum(p, axis=1, keepdims=True)
        acc_sc[...] = alpha * acc_sc[...] + jnp.dot(p.astype(jnp.bfloat16), v_ref[...],
                                                    preferred_element_type=jnp.float32)
        m_sc[...] = m_next

    @pl.when(ki < qi)
    def _():
        update(False)

    @pl.when(ki == qi)
    def _():
        update(True)
        o_ref[...] = (acc_sc[...] / l_sc[...]).astype(o_ref.dtype)


def _attention(q, k, v, seq, blk, b, q_lo, q_hi, after=None):
    nb = seq // blk
    qi = jnp.array([i for i in range(q_lo, q_hi) for _ in range(i + 1)], jnp.int32)
    ki = jnp.array([j for i in range(q_lo, q_hi) for j in range(i + 1)], jnp.int32)
    if after is not None:
        qi = qi + jnp.minimum(after, 0)
    grid_spec = pltpu.PrefetchScalarGridSpec(
        num_scalar_prefetch=2,
        grid=(MLA_HEADS, qi.shape[0]),
        in_specs=[pl.BlockSpec((None, blk, 2 * LANES), lambda h, s, qi, ki: (h, b * nb + qi[s], 0)),
                  pl.BlockSpec((None, blk, 2 * LANES), lambda h, s, qi, ki: (h, b * nb + ki[s], 0)),
                  pl.BlockSpec((None, blk, MLA_V), lambda h, s, qi, ki: (h, b * nb + ki[s], 0))],
        out_specs=pl.BlockSpec((blk, MLA_V), lambda h, s, qi, ki: (qi[s] - q_lo, h)),
        scratch_shapes=[pltpu.VMEM((blk, LANES), jnp.float32), pltpu.VMEM((blk, LANES), jnp.float32),
                        pltpu.VMEM((blk, MLA_V), jnp.float32)],
    )
    return pl.pallas_call(
        _attn_kernel,
        grid_spec=grid_spec,
        out_shape=jax.ShapeDtypeStruct(((q_hi - q_lo) * blk, MLA_HEADS * MLA_V), jnp.bfloat16),
        compiler_params=_params("parallel", "arbitrary"),
        name="mla_attention",
    )(qi, ki, q, k, v)


def _col_reduce(x, op, reduce_fn):
    slabs = [x[i:i + 8] for i in range(0, x.shape[0], 8)]
    while len(slabs) > 1:
        nxt = [op(slabs[i], slabs[i + 1]) for i in range(0, len(slabs) - 1, 2)]
        if len(slabs) % 2:
            nxt.append(slabs[-1])
        slabs = nxt
    return reduce_fn(slabs[0], axis=0, keepdims=True)


def _top_rows(vals, ids, count, out_rows):
    t = vals.shape[1]
    big = jnp.int32(2 ** 30)
    orow = lax.broadcasted_iota(jnp.int32, (out_rows, t), 0)

    def body(r, carry):
        cur, ov, oi = carry
        m = _col_reduce(cur, jnp.maximum, jnp.max)
        pick = _col_reduce(jnp.where(cur == m, ids, big), jnp.minimum, jnp.min)
        cur = jnp.where(ids == pick, NEG_INF, cur)
        ov = jnp.where(orow == r, m, ov)
        oi = jnp.where(orow == r, pick, oi)
        return cur, ov, oi

    init = (vals, jnp.zeros((out_rows, t), jnp.float32), jnp.zeros((out_rows, t), jnp.int32))
    _, ov, oi = lax.fori_loop(0, count, body, init)
    return ov, oi


_ROW_SLABS = [(0, 0, 16), (1, 0, 8)] + [(a, 0, 8) for a in range(2, 8)]
_COL_SLAB = (8, 16, 0)
_PAIR_ROWS = sum(hi - lo for _, lo, hi in _ROW_SLABS) + (_COL_SLAB[1] - _COL_SLAB[0])


def _route_kernel(o_ref, h_ref, wo_ref, g_ref, wqt_ref, keys_ref, pos_ref,
                  hn_out, xn_out, eid_out, gate_out, qt_sc, v_sc, i_sc):
    tb = h_ref.shape[0]
    hnew = h_ref[...] + jnp.dot(o_ref[...], wo_ref[...], preferred_element_type=jnp.float32)
    hn_out[...] = hnew
    xn = _rms(hnew, g_ref[...])
    xn_out[...] = _pack_pairs(xn)
    qt_sc[...] = lax.dot_general(wqt_ref[...], xn.astype(jnp.bfloat16), (((1,), (1,)), ((), ())),
                                 preferred_element_type=jnp.float32).astype(jnp.bfloat16)
    key_ids = lax.broadcasted_iota(jnp.int32, (PEER_NKEYS, tb), 0)

    def group(g, carry):
        row0 = pl.multiple_of(g * PEER_HALF, PEER_HALF)
        st = jnp.dot(keys_ref[g], qt_sc[pl.ds(row0, PEER_HALF), :], preferred_element_type=jnp.float32)
        tv, ti = _top_rows(st, key_ids, PEER_TOPK, PEER_TOPK)
        out0 = pl.multiple_of(g * PEER_TOPK, PEER_TOPK)
        v_sc[pl.ds(out0, PEER_TOPK), :] = tv
        i_sc[pl.ds(out0, PEER_TOPK), :] = ti
        return carry

    lax.fori_loop(0, 2 * PEER_HEADS, group, 0)

    pos = pos_ref[...]

    def head(hd, carry):
        base = pl.multiple_of(hd * 2 * PEER_TOPK, 2 * PEER_TOPK)
        v1 = v_sc[pl.ds(base, PEER_TOPK), :]
        i1 = i_sc[pl.ds(base, PEER_TOPK), :]
        v2 = v_sc[pl.ds(base + PEER_TOPK, PEER_TOPK), :]
        i2 = i_sc[pl.ds(base + PEER_TOPK, PEER_TOPK), :]
        cv, ce = [], []
        for a, lo, hi in _ROW_SLABS:
            cv.append(v1[a:a + 1, :] + v2[lo:hi, :])
            ce.append(i1[a:a + 1, :] * PEER_NKEYS + i2[lo:hi, :])
        a_lo, a_hi, b = _COL_SLAB
        cv.append(v1[a_lo:a_hi, :] + v2[b:b + 1, :])
        ce.append(i1[a_lo:a_hi, :] * PEER_NKEYS + i2[b:b + 1, :])
        cv = jnp.concatenate(cv, axis=0)
        ce = jnp.concatenate(ce, axis=0)
        tv, tp = _top_rows(cv, jnp.broadcast_to(pos, cv.shape), PEER_TOPK, PEER_TOPK)
        te = jnp.zeros((PEER_TOPK, tb), jnp.int32)
        orow = lax.broadcasted_iota(jnp.int32, (PEER_TOPK, tb), 0)
        for r in range(PEER_TOPK):
            e_r = jnp.sum(jnp.where(pos == tp[r:r + 1, :], ce, 0), axis=0, keepdims=True)
            te = jnp.where(orow == r, e_r, te)
        ex = jnp.exp(tv - tv[0:1, :])
        gates = ex / jnp.sum(ex, axis=0, keepdims=True)
        out0 = pl.multiple_of(hd * PEER_TOPK, PEER_TOPK)
        v_sc[pl.ds(out0, PEER_TOPK), :] = gates
        i_sc[pl.ds(out0, PEER_TOPK), :] = te
        return carry

    lax.fori_loop(0, PEER_HEADS, head, 0)

    eid_out[...] = jnp.transpose(i_sc[:PEER_SLOTS, :].astype(jnp.float32)).astype(jnp.int32)
    gate_out[...] = jnp.transpose(v_sc[:PEER_SLOTS, :])


def _pair_positions():
    pos = [a * PEER_TOPK + b for a, lo, hi in _ROW_SLABS for b in range(lo, hi)]
    a_lo, a_hi, b = _COL_SLAB
    pos += [a * PEER_TOPK + b for a in range(a_lo, a_hi)]
    return jnp.array(pos, jnp.int32).reshape(_PAIR_ROWS, 1)


def _route(o, h, h_row0, w_out, g, wqt, keys):
    n = o.shape[0]
    tb = TOKEN_BLOCK
    full = lambda a: pl.BlockSpec(a.shape, lambda i: (0,) * a.ndim)
    g = g.reshape(1, -1)
    pos = _pair_positions()
    row = pl.BlockSpec((tb, D_MODEL), lambda i: (i, 0))
    h_row = pl.BlockSpec((tb, D_MODEL), lambda i: (i + h_row0 // tb, 0))
    f32 = jnp.float32
    return pl.pallas_call(
        _route_kernel,
        grid=(n // tb,),
        in_specs=[row, h_row, full(w_out), full(g), full(wqt), full(keys), full(pos)],
        out_specs=[row, pl.BlockSpec((tb, ROW_WORDS), lambda i: (i, 0)),
                   pl.BlockSpec((tb, PEER_SLOTS), lambda i: (i, 0)),
                   pl.BlockSpec((tb, PEER_SLOTS), lambda i: (i, 0))],
        out_shape=[jax.ShapeDtypeStruct((n, D_MODEL), f32), jax.ShapeDtypeStruct((n, ROW_WORDS), jnp.uint32),
                   jax.ShapeDtypeStruct((n, PEER_SLOTS), jnp.int32),
                   jax.ShapeDtypeStruct((n, PEER_SLOTS), f32)],
        scratch_shapes=[pltpu.VMEM((2 * PEER_HEADS * PEER_HALF, tb), jnp.bfloat16),
                        pltpu.VMEM((2 * PEER_HEADS * PEER_TOPK, tb), f32),
                        pltpu.VMEM((2 * PEER_HEADS * PEER_TOPK, tb), jnp.int32)],
        compiler_params=_params("parallel"),
        name="peer_route",
    )(o, h, w_out, g, wqt, keys, pos)


def _sc_mesh():
    return plsc.VectorSubcoreMesh(core_axis_name="c", subcore_axis_name="s")


def _sc_params():
    return pltpu.CompilerParams(needs_layout_passes=False)


def _worker_id():
    return lax.axis_index("s") * SC_CORES + lax.axis_index("c")


def _gather_rows(tab_hbm, idx_ref, dst_ref, sem):
    return pltpu.make_async_copy(tab_hbm.at[idx_ref], dst_ref, sem)


def _unpack16(word):
    lo = lax.bitcast_convert_type(word << 16, jnp.float32)
    hi = lax.bitcast_convert_type(word & jnp.uint32(0xFFFF0000), jnp.float32)
    return lo, hi


def _as_bf16(word):
    return plsc.bitcast(word, jnp.bfloat16)


def _sc_dots(tab, idx, x):
    t = x.shape[0]
    workers = SC_CORES * SC_SUBCORES
    tok_w = t // workers
    per_worker = tok_w * PEER_SLOTS
    w = GATHER_WINDOW
    slots = DOTS_SLOTS
    n_win = per_worker // w
    win_per_tok = PEER_SLOTS // w
    n_chunks = ROW_WORDS // SC_LANES
    f32 = jnp.float32

    @functools.partial(
        pl.kernel, out_type=jax.ShapeDtypeStruct((t * PEER_SLOTS,), f32), mesh=_sc_mesh(),
        scratch_types=[pltpu.VMEM((per_worker,), jnp.int32),
                       pltpu.VMEM((tok_w, ROW_WORDS), jnp.uint32),
                       pltpu.VMEM((slots, w, ROW_WORDS), jnp.uint32),
                       pltpu.VMEM((per_worker,), f32),
                       pltpu.VMEM((w * SC_LANES,), f32),
                       pltpu.SemaphoreType.DMA((slots,))],
        compiler_params=_sc_params(), name="peer_dots")
    def dots(tab_hbm, i_hbm, x_hbm, act_hbm, idx_v, x_v, rows, act_v, part_v, sem):
        wid = _worker_id()
        base = pl.multiple_of(wid * per_worker, per_worker)
        tok0 = pl.multiple_of(wid * tok_w, tok_w)
        pltpu.sync_copy(i_hbm.at[pl.ds(base, per_worker)], idx_v)
        pltpu.sync_copy(x_hbm.at[pl.ds(tok0, tok_w)], x_v)

        def gather(win, slot):
            ix = idx_v.at[pl.ds(pl.multiple_of(win * w, w), w)]
            return _gather_rows(tab_hbm, ix, rows.at[slot], sem.at[slot])

        for s in range(slots - 1):
            gather(s, s).start()
        lane = lax.iota(jnp.int32, SC_LANES)

        @pl.loop(0, n_win)
        def _(win):
            slot = lax.rem(win, slots)
            nxt = win + slots - 1

            @pl.when(nxt < n_win)
            def _():
                gather(nxt, lax.rem(nxt, slots)).start()

            gather(win, slot).wait()
            tok = win // win_per_tok

            @plsc.parallel_loop(0, w // DOT_ROWS)
            def _(g):
                r0 = g * DOT_ROWS
                acc_lo = [jnp.zeros((SC_LANES,), f32) for _ in range(DOT_ROWS)]
                acc_hi = [jnp.zeros((SC_LANES,), f32) for _ in range(DOT_ROWS)]
                for c in range(0, n_chunks, 2):
                    xa = _as_bf16(x_v[tok, pl.ds(c * SC_LANES, SC_LANES)])
                    xb = _as_bf16(x_v[tok, pl.ds((c + 1) * SC_LANES, SC_LANES)])
                    for r in range(DOT_ROWS):
                        ra = _as_bf16(rows[slot, r0 + r, pl.ds(c * SC_LANES, SC_LANES)])
                        rb = _as_bf16(rows[slot, r0 + r, pl.ds((c + 1) * SC_LANES, SC_LANES)])
                        lo, hi = _unpack16(plsc.bitcast(ra * xa + rb * xb, jnp.uint32))
                        acc_lo[r] = acc_lo[r] + lo
                        acc_hi[r] = acc_hi[r] + hi
                for r in range(DOT_ROWS):
                    part_v[pl.ds(pl.multiple_of((r0 + r) * SC_LANES, SC_LANES), SC_LANES)] = acc_lo[r] + acc_hi[r]

            for blk in range(w // SC_LANES):
                res = jnp.zeros((SC_LANES,), f32)
                for l in range(SC_LANES):
                    res = res + plsc.load_gather(part_v, [lane * SC_LANES + (blk * SC_LANES * SC_LANES + l)])
                act_v[pl.ds(pl.multiple_of(win * w + blk * SC_LANES, SC_LANES), SC_LANES)] = res

        pltpu.sync_copy(act_v, act_hbm.at[pl.ds(base, per_worker)])

    return dots(tab, idx, x)


def _sc_axpy(tab, idx, wts):
    p = idx.shape[0]
    t = p // PEER_SLOTS
    workers = SC_CORES * SC_SUBCORES
    tok_w = t // workers
    per_worker = tok_w * PEER_SLOTS
    w = GATHER_WINDOW
    slots = AXPY_SLOTS
    n_win = per_worker // w
    win_per_tok = PEER_SLOTS // w
    passes = ROW_WORDS // (SC_LANES * AXPY_CHUNKS)
    f32 = jnp.float32

    @functools.partial(
        pl.kernel, out_type=jax.ShapeDtypeStruct((t, D_MODEL), f32), mesh=_sc_mesh(),
        scratch_types=[pltpu.VMEM((per_worker,), jnp.int32),
                       pltpu.VMEM((per_worker,), jnp.uint32),
                       pltpu.VMEM((slots, w, ROW_WORDS), jnp.uint32),
                       pltpu.VMEM((2, D_MODEL), f32),
                       pltpu.SemaphoreType.DMA((slots,)),
                       pltpu.SemaphoreType.DMA((2,))],
        compiler_params=_sc_params(), name="peer_axpy")
    def axpy(tab_hbm, i_hbm, w_hbm, y_hbm, idx_v, w_v, rows, y_v, sem, sem_y):
        wid = _worker_id()
        base = pl.multiple_of(wid * per_worker, per_worker)
        tok0 = wid * tok_w
        pltpu.sync_copy(i_hbm.at[pl.ds(base, per_worker)], idx_v)
        pltpu.sync_copy(w_hbm.at[pl.ds(base, per_worker)], w_v)

        def gather(win, slot):
            ix = idx_v.at[pl.ds(pl.multiple_of(win * w, w), w)]
            return _gather_rows(tab_hbm, ix, rows.at[slot], sem.at[slot])

        def y_write(tok, buf):
            return pltpu.make_async_copy(y_v.at[buf], y_hbm.at[tok0 + tok], sem_y.at[buf])

        for s in range(slots - 1):
            gather(s, s).start()

        @pl.loop(0, n_win)
        def _(win):
            slot = lax.rem(win, slots)
            nxt = win + slots - 1

            @pl.when(nxt < n_win)
            def _():
                gather(nxt, lax.rem(nxt, slots)).start()

            gather(win, slot).wait()
            tok = win // win_per_tok
            part = lax.rem(win, win_per_tok)
            buf = lax.rem(tok, 2)

            @pl.when(part == 0)
            def _():
                @pl.when(tok >= 2)
                def _():
                    y_write(tok - 2, buf).wait()

                for c in range(D_MODEL // SC_LANES):
                    y_v[buf, pl.ds(c * SC_LANES, SC_LANES)] = jnp.zeros((SC_LANES,), f32)

            for ps in range(passes):

                def group(g, accs):
                    accs = list(accs)
                    row0 = pl.multiple_of(g * SC_LANES, SC_LANES)
                    w_grp = w_v[pl.ds(pl.multiple_of(win * w + row0, SC_LANES), SC_LANES)]
                    for k in range(0, SC_LANES, 2):
                        wa = _as_bf16(jnp.take_along_axis(w_grp, jnp.full((SC_LANES,), k, jnp.int32), axis=0))
                        wb = _as_bf16(jnp.take_along_axis(w_grp, jnp.full((SC_LANES,), k + 1, jnp.int32), axis=0))
                        for c in range(AXPY_CHUNKS):
                            col = (ps * AXPY_CHUNKS + c) * SC_LANES
                            ra = _as_bf16(rows[slot, row0 + k, pl.ds(col, SC_LANES)])
                            rb = _as_bf16(rows[slot, row0 + k + 1, pl.ds(col, SC_LANES)])
                            lo, hi = _unpack16(plsc.bitcast(ra * wa + rb * wb, jnp.uint32))
                            accs[2 * c] = accs[2 * c] + lo
                            accs[2 * c + 1] = accs[2 * c + 1] + hi
                    return tuple(accs)

                zero = tuple(jnp.zeros((SC_LANES,), f32) for _ in range(2 * AXPY_CHUNKS))
                accs = lax.fori_loop(0, w // SC_LANES, group, zero)
                for c in range(AXPY_CHUNKS):
                    col = (ps * AXPY_CHUNKS + c) * SC_LANES
                    y_v[buf, pl.ds(col, SC_LANES)] = y_v[buf, pl.ds(col, SC_LANES)] + accs[2 * c]
                    y_v[buf, pl.ds(ROW_WORDS + col, SC_LANES)] = (
                        y_v[buf, pl.ds(ROW_WORDS + col, SC_LANES)] + accs[2 * c + 1])

            @pl.when(part == win_per_tok - 1)
            def _():
                y_write(tok, buf).start()

        for tok in (tok_w - 2, tok_w - 1):
            y_write(tok, tok % 2).wait()

    return axpy(tab, idx, wts)


def _slot_weight_kernel(act_ref, gate_ref, o_ref):
    act = act_ref[...]
    wts = gate_ref[...] * (0.5 * act * (1.0 + lax.erf(act * (2.0 ** -0.5))))
    hi = _bf16_bits(wts)
    o_ref[...] = hi | (hi >> 16)


def _slot_weights(act, gates):
    n = gates.shape[0]
    tb = 1024
    blk = pl.BlockSpec((tb, PEER_SLOTS), lambda i: (i, 0))
    return pl.pallas_call(
        _slot_weight_kernel, grid=(n // tb,), in_specs=[blk, blk], out_specs=blk,
        out_shape=jax.ShapeDtypeStruct((n, PEER_SLOTS), jnp.uint32),
        compiler_params=_params("parallel"), name="peer_slot_weights",
    )(act, gates)


def _residual_kernel(h_ref, y_ref, g_ref, o_ref, *, final_norm):
    out = h_ref[...] + y_ref[...]
    o_ref[...] = _rms(out, g_ref[...]) if final_norm else out


def _residual(h, y, g_final, final_norm):
    n = h.shape[0]
    tb = 512
    blk = pl.BlockSpec((tb, D_MODEL), lambda i: (i, 0))
    g_final = g_final.reshape(1, -1)
    return pl.pallas_call(
        functools.partial(_residual_kernel, final_norm=final_norm),
        grid=(n // tb,), in_specs=[blk, blk, pl.BlockSpec((1, D_MODEL), lambda i: (0, 0))], out_specs=blk,
        out_shape=jax.ShapeDtypeStruct((n, D_MODEL), jnp.float32),
        compiler_params=_params("parallel"), name="peer_residual",
    )(h, y, g_final)


def _peer(xn, eid, gates, h, tab_u, tab_v, g_final, final_norm):
    n = xn.shape[0]
    tc = min(GATHER_TOKENS, n)
    chunks = [slice(c * tc, (c + 1) * tc) for c in range(n // tc)]
    act = jnp.concatenate([_sc_dots(tab_u, eid[tok].reshape(-1), xn[tok]) for tok in chunks])
    wts = _slot_weights(act.reshape(n, PEER_SLOTS), gates)
    y = jnp.concatenate([_sc_axpy(tab_v, eid[tok].reshape(-1), wts[tok].reshape(-1)) for tok in chunks])
    return _residual(h, y, g_final, final_norm)


_N_SUB = HG_CHUNK // HG_SUB
_OFF_PAIRS = [(i, j) for i in range(_N_SUB) for j in range(i)]


def _cum_matrix():
    t = jnp.arange(HG_CHUNK)[:, None]
    r = jnp.arange(HG_CHUNK)[None, :]
    sub = t // HG_SUB
    incl = r <= t
    before = r < sub * HG_SUB
    end = r < (sub + 1) * HG_SUB
    return jnp.concatenate([incl, before, end], axis=0).astype(jnp.float32)


def _hgrn_kernel(h_ref, g_ref, w_ref, lb_ref, on_ref, cum_ref, st_in_ref, o_ref, st_out_ref,
                 z_sc, st_sc, lb_sc):
    @pl.when(pl.program_id(0) == 0)
    def _():
        st_sc[...] = st_in_ref[...]

    tb = h_ref.shape[0]
    hn = _rms(h_ref[...], g_ref[...]).astype(jnp.bfloat16)
    z_sc[...] = jnp.dot(hn, w_ref[...], preferred_element_type=jnp.float32)
    lbr = lb_ref[...]
    mx = jnp.max(lbr, axis=0, keepdims=True)
    ex = jnp.exp(lbr - mx)
    prob = ex / jnp.sum(ex, axis=0, keepdims=True)
    lb_sc[...] = jnp.broadcast_to((prob[0:1, :] + prob[1:2, :]) - prob[0:1, :], lb_sc.shape)
    wf = HG_HEADS * HG_DK
    sub_row = lax.broadcasted_iota(jnp.int32, (HG_SUB, HG_DK), 0)
    lane64 = lax.broadcasted_iota(jnp.int32, (HG_SUB, HG_CHUNK), 1)

    mxu = jnp.bfloat16
    nt = (((1,), (1,)), ((), ()))

    def head(hd, carry):
        c0 = pl.multiple_of(hd * HG_DK, HG_DK)
        lb = lb_sc[0:1, pl.ds(c0, HG_DK)]
        on = on_ref[0:1, pl.ds(c0, HG_DK)]
        st = st_sc[hd]
        for ch in range(tb // HG_CHUNK):
            rows = slice(ch * HG_CHUNK, (ch + 1) * HG_CHUNK)
            qp = z_sc[rows, pl.ds(c0, HG_DK)]
            fp = z_sc[rows, pl.ds(pl.multiple_of(wf + c0, HG_DK), HG_DK)]
            v = z_sc[rows, pl.ds(pl.multiple_of(2 * wf + c0, HG_DK), HG_DK)]
            gp = z_sc[rows, pl.ds(pl.multiple_of(2 * wf + HG_HEADS * HG_DV + c0, HG_DK), HG_DK)]
            f = lb + (1.0 - lb) * jax.nn.sigmoid(fp)
            lf = jnp.log(f)
            k = 1.0 - f
            q = qp * jax.nn.sigmoid(qp)
            cums = jnp.dot(cum_ref[...], lf, precision=lax.Precision.HIGHEST,
                           preferred_element_type=jnp.float32)
            b = cums[:HG_CHUNK]
            b_start = cums[HG_CHUNK:2 * HG_CHUNK]
            b_end = cums[2 * HG_CHUNK:]
            q_hat = q * jnp.exp(b - b_start)
            k_hat = k * jnp.exp(b_end - b)
            o_inter = lax.dot_general((q_hat * jnp.exp(b_start)).astype(mxu), st.astype(mxu), nt,
                                      preferred_element_type=jnp.float32)
            stacked = []
            for (i, j) in _OFF_PAIRS:
                d_ij = jnp.exp(b_start[i * HG_SUB:i * HG_SUB + 1, :] - b_end[j * HG_SUB:j * HG_SUB + 1, :])
                stacked.append(q_hat[i * HG_SUB:(i + 1) * HG_SUB, :] * d_ij)
            stacked = jnp.concatenate(stacked, axis=0).astype(mxu)
            off = lax.dot_general(stacked, k_hat.astype(mxu), nt,
                                  preferred_element_type=jnp.float32)
            a_rows = []
            for i in range(_N_SUB):
                blk = jnp.zeros((HG_SUB, HG_CHUNK), jnp.float32)
                for p, (pi, pj) in enumerate(_OFF_PAIRS):
                    if pi == i:
                        in_j = (lane64 >= pj * HG_SUB) & (lane64 < (pj + 1) * HG_SUB)
                        blk = jnp.where(in_j, off[p * HG_SUB:(p + 1) * HG_SUB, :], blk)
                b_blk = b[i * HG_SUB:(i + 1) * HG_SUB, :]
                q_blk = q[i * HG_SUB:(i + 1) * HG_SUB, :]
                for s in range(HG_SUB):
                    n = i * HG_SUB + s
                    e = jnp.exp(jnp.where(sub_row >= s, b_blk - b[n:n + 1, :], MASKED_LOG_DECAY))
                    col = jnp.sum(q_blk * k[n:n + 1, :] * e, axis=1, keepdims=True)
                    blk = jnp.where(lane64 == n, col, blk)
                a_rows.append(blk)
            a = jnp.concatenate(a_rows, axis=0).astype(mxu)
            o = o_inter + jnp.dot(a, v.astype(mxu), preferred_element_type=jnp.float32)
            b_last = b[HG_CHUNK - 1:HG_CHUNK, :]
            k_til = (k_hat * jnp.exp(b_last - b_end)).astype(mxu)
            upd = lax.dot_general(v.astype(mxu), k_til, (((0,), (0,)), ((), ())),
                                  preferred_element_type=jnp.float32)
            st = st * jnp.exp(b_last) + upd
            o = o * lax.rsqrt(jnp.mean(o * o, axis=-1, keepdims=True) + NORM_EPS)
            o = o * on * (gp * jax.nn.sigmoid(gp))
            o_ref[rows, pl.ds(c0, HG_DK)] = o.astype(o_ref.dtype)
        st_sc[hd] = st
        return carry

    lax.fori_loop(0, HG_HEADS, head, 0)

    @pl.when(pl.program_id(0) == pl.num_programs(0) - 1)
    def _():
        st_out_ref[...] = st_sc[...]


def _hgrn(h, state, g, w, lb_raw, out_norm):
    n = h.shape[0]
    tb = min(HG_BLOCK, n)
    g = g.reshape(1, -1)
    out_norm = out_norm.reshape(1, -1)
    cum = _cum_matrix()
    full = lambda a: pl.BlockSpec(a.shape, lambda i: (0,) * a.ndim)
    f32 = jnp.float32
    return pl.pallas_call(
        _hgrn_kernel,
        grid=(n // tb,),
        in_specs=[pl.BlockSpec((tb, D_MODEL), lambda i: (i, 0)), full(g), full(w),
                  full(lb_raw), full(out_norm), full(cum), full(state)],
        out_specs=[pl.BlockSpec((tb, HG_HEADS * HG_DV), lambda i: (i, 0)), full(state)],
        out_shape=[jax.ShapeDtypeStruct((n, HG_HEADS * HG_DV), jnp.bfloat16),
                   jax.ShapeDtypeStruct(state.shape, f32)],
        scratch_shapes=[pltpu.VMEM((tb, w.shape[1]), f32),
                        pltpu.VMEM((HG_HEADS, HG_DV, HG_DK), f32),
                        pltpu.VMEM((8, HG_HEADS * HG_DK), f32)],
        compiler_params=_params("arbitrary"),
        name="hgrn2",
    )(h, g, w, lb_raw, out_norm, cum, state)


def kernel(x, positions, ln_mix, ln_ffn, ln_final, mla_w_in, mla_q_norm, mla_w_q_up, mla_kv_norm,
           mla_w_kv_up, mla_w_out, hg_w_in, hg_lb, hg_out_norm, hg_w_out, peer_w_q, peer_sub_keys,
           peer_u, peer_v):
    batch, seq, d = x.shape
    n = batch * seq
    bf16 = jnp.bfloat16
    h = x.reshape(n, d)

    def route_weights(i):
        keys = peer_sub_keys[i].reshape(2 * PEER_HEADS, PEER_NKEYS, PEER_HALF).astype(bf16)
        return peer_w_q[i].T.astype(bf16), keys

    tables = [(_pack_table(peer_u[i]), _pack_table(peer_v[i])) for i in range(2)]
    cos_t, sin_t = _rope_tables(positions)
    win, wq, wkv = _mla_weights(mla_w_in[0], mla_w_q_up[0], mla_w_kv_up[0])
    q, k, v = _mla_proj(h, ln_mix[0], cos_t, sin_t, win, mla_q_norm[0], wq, mla_kv_norm[0], wkv)
    wqt0, keys0 = route_weights(0)
    wqt1, keys1 = route_weights(1)
    mla_wo = mla_w_out[0].astype(bf16)
    hg_wi = hg_w_in[0].astype(bf16)
    hg_wo = hg_w_out[0].astype(bf16)

    chunk = min(PIPE_TOKENS, seq)
    blk = min(ATTN_BLOCK, chunk)
    n_chunks = seq // chunk
    states = [jnp.zeros((HG_HEADS, HG_DV, HG_DK), jnp.float32) for _ in range(batch)]
    outs = {}

    routed = []

    def layer0(b, j):
        after = routed[-2][0, 0] if len(routed) >= 2 else None
        o = _attention(q, k, v, seq, blk, b, j * chunk // blk, (j + 1) * chunk // blk, after)
        hc, xp, eid, gates = _route(o, h, b * seq + j * chunk, mla_wo, ln_ffn[0], wqt0, keys0)
        routed.append(eid)
        return _peer(xp, eid, gates, hc, tables[0][0], tables[0][1], ln_final, False)

    def layer1(b, j, hc):
        o, states[b] = _hgrn(hc, states[b], ln_mix[1], hg_wi, hg_lb, hg_out_norm[0])
        hc, xp, eid, gates = _route(o, hc, 0, hg_wo, ln_ffn[1], wqt1, keys1)
        outs[(b, j)] = _peer(xp, eid, gates, hc, tables[1][0], tables[1][1], ln_final, True)

    pending = None
    for j in range(n_chunks):
        for b in range(batch):
            hc = layer0(b, j)
            if pending is not None:
                layer1(*pending)
            pending = (b, j, hc)
    layer1(*pending)
    out = jnp.concatenate([outs[(b, j)] for b in range(batch) for j in range(n_chunks)], axis=0)
    return out.reshape(batch, seq, d)
```

```python
import functools

import jax
import jax.numpy as jnp
from jax import lax
from jax.experimental import pallas as pl
from jax.experimental.pallas import tpu as pltpu
from jax.experimental.pallas import tpu_sc as plsc

D_MODEL = 1024
NORM_EPS = 1e-6
MLA_HEADS = 8
MLA_Q_LORA = 384
MLA_KV_LORA = 256
MLA_NOPE = 128
MLA_ROPE = 64
MLA_V = 128
ROPE_THETA = 10000.0
HG_HEADS = 8
HG_DK = 128
HG_DV = 128
PEER_HEADS = 8
PEER_NKEYS = 128
PEER_HALF = 128
PEER_TOPK = 16
PEER_SLOTS = PEER_HEADS * PEER_TOPK

LANES = 128
SC_CORES = 2
SC_SUBCORES = 16
VMEM_LIMIT = 48 * 1024 * 1024

ROW_WORDS = D_MODEL // 2
TOKEN_BLOCK = 256
ATTN_BLOCK = 1024
HG_BLOCK = 256
HG_CHUNK = 64
HG_SUB = 16
PIPE_TOKENS = 2048
GATHER_WINDOW = 32
DOTS_SLOTS = 5
AXPY_SLOTS = 7
GATHER_TOKENS = 1024
SC_LANES = 16
DOT_ROWS = 8
AXPY_CHUNKS = 8
NEG_INF = float("-inf")
MASKED_LOG_DECAY = -1e30


def _rms(x, g):
    return x * lax.rsqrt(jnp.mean(x * x, axis=-1, keepdims=True) + NORM_EPS) * g


def _params(*sem):
    return pltpu.CompilerParams(dimension_semantics=sem, vmem_limit_bytes=VMEM_LIMIT)


def _bf16_bits(x):
    return pltpu.bitcast(x.astype(jnp.bfloat16).astype(jnp.float32), jnp.uint32)


def _pack_pairs(t):
    return (_bf16_bits(t[:, :ROW_WORDS]) >> 16) | _bf16_bits(t[:, ROW_WORDS:])


def _pack_kernel(t_ref, o_ref):
    o_ref[...] = _pack_pairs(t_ref[...])


def _pack_table(tab):
    e, d = tab.shape
    rows = 512
    return pl.pallas_call(
        _pack_kernel,
        grid=(e // rows,),
        in_specs=[pl.BlockSpec((rows, d), lambda i: (i, 0))],
        out_specs=pl.BlockSpec((rows, d // 2), lambda i: (i, 0)),
        out_shape=jax.ShapeDtypeStruct((e, d // 2), jnp.uint32),
        compiler_params=_params("parallel"),
        name="pack_table",
    )(tab)


def _rope_kernel(pos_ref, inv_ref, sign_ref, c_ref, s_ref):
    ang = pos_ref[...].astype(jnp.float32) * inv_ref[...]
    c_ref[...] = jnp.cos(ang)
    s_ref[...] = jnp.sin(ang) * sign_ref[...]


def _rope_tables(positions):
    n = positions.size
    lane = jnp.arange(LANES)
    inv = ROPE_THETA ** (-(2 * (lane % (MLA_ROPE // 2))).astype(jnp.float32) / MLA_ROPE)
    sign = jnp.where((lane % MLA_ROPE) < MLA_ROPE // 2, -1.0, 1.0).astype(jnp.float32)
    tb = 1024
    out = jax.ShapeDtypeStruct((n, LANES), jnp.float32)
    return pl.pallas_call(
        _rope_kernel,
        grid=(n // tb,),
        in_specs=[pl.BlockSpec((tb, 1), lambda i: (i, 0)),
                  pl.BlockSpec((1, LANES), lambda i: (0, 0)),
                  pl.BlockSpec((1, LANES), lambda i: (0, 0))],
        out_specs=[pl.BlockSpec((tb, LANES), lambda i: (i, 0))] * 2,
        out_shape=[out, out],
        compiler_params=_params("parallel"),
        name="rope_tables",
    )(positions.reshape(n, 1), inv.reshape(1, LANES), sign.reshape(1, LANES))


def _mla_proj_kernel(x_ref, g_ref, c_ref, s_ref, win_ref, qn_ref, wq_ref, kvn_ref, wkv_ref,
                     q_out, k_out, v_out):
    hn = _rms(x_ref[...], g_ref[...]).astype(jnp.bfloat16)
    z = jnp.dot(hn, win_ref[...], preferred_element_type=jnp.float32)
    c = c_ref[...]
    s = s_ref[...]
    o_kv = MLA_Q_LORA
    o_kr = MLA_Q_LORA + MLA_KV_LORA
    k_rope = (z[:, o_kr:o_kr + LANES] * c + z[:, o_kr + LANES:o_kr + 2 * LANES] * s).astype(jnp.bfloat16)
    cq = _rms(z[:, :MLA_Q_LORA], qn_ref[...]).astype(jnp.bfloat16)
    q = jnp.dot(cq, wq_ref[...], preferred_element_type=jnp.float32)
    ckv = _rms(z[:, o_kv:o_kr], kvn_ref[...]).astype(jnp.bfloat16)
    kv = jnp.dot(ckv, wkv_ref[...], preferred_element_type=jnp.float32)
    scale = (MLA_NOPE + MLA_ROPE) ** -0.5
    hw = MLA_HEADS * LANES
    for h in range(MLA_HEADS):
        sl = slice(h * LANES, (h + 1) * LANES)
        q_out[h, :, :LANES] = (q[:, sl] * scale).astype(jnp.bfloat16)
        qr = q[:, hw + h * LANES:hw + (h + 1) * LANES] * c + q[:, 2 * hw + h * LANES:2 * hw + (h + 1) * LANES] * s
        q_out[h, :, LANES:] = (qr * scale).astype(jnp.bfloat16)
        k_out[h, :, :LANES] = kv[:, sl].astype(jnp.bfloat16)
        k_out[h, :, LANES:] = k_rope
        v_out[h] = kv[:, hw + h * LANES:hw + (h + 1) * LANES].astype(jnp.bfloat16)


def _swap_halves(w):
    half = w.shape[-1] // 2
    return jnp.concatenate([w[..., half:], w[..., :half]], axis=-1)


def _pad_lanes(w):
    return jnp.pad(w, [(0, 0)] * (w.ndim - 1) + [(0, LANES - w.shape[-1])])


def _mla_weights(w_in, w_q_up, w_kv_up):
    o_kr = MLA_Q_LORA + MLA_KV_LORA
    w_kr = w_in[:, o_kr:]
    win = jnp.concatenate([w_in[:, :o_kr], _pad_lanes(w_kr), _pad_lanes(_swap_halves(w_kr))], axis=1)
    wq = w_q_up.reshape(MLA_Q_LORA, MLA_HEADS, MLA_NOPE + MLA_ROPE)
    wq_n = wq[:, :, :MLA_NOPE].reshape(MLA_Q_LORA, -1)
    wq_r = _pad_lanes(wq[:, :, MLA_NOPE:]).reshape(MLA_Q_LORA, -1)
    wq_rs = _pad_lanes(_swap_halves(wq[:, :, MLA_NOPE:])).reshape(MLA_Q_LORA, -1)
    wqp = jnp.concatenate([wq_n, wq_r, wq_rs], axis=1)
    wkv = w_kv_up.reshape(MLA_KV_LORA, MLA_HEADS, MLA_NOPE + MLA_V)
    wkvp = jnp.concatenate([wkv[:, :, :MLA_NOPE].reshape(MLA_KV_LORA, -1),
                            wkv[:, :, MLA_NOPE:].reshape(MLA_KV_LORA, -1)], axis=1)
    return win.astype(jnp.bfloat16), wqp.astype(jnp.bfloat16), wkvp.astype(jnp.bfloat16)


def _mla_proj(h, g, cos_t, sin_t, win, q_norm, wq, kv_norm, wkv):
    n = h.shape[0]
    tb = TOKEN_BLOCK
    full = lambda a: pl.BlockSpec(a.shape, lambda i: (0,) * a.ndim)
    g = g.reshape(1, -1)
    q_norm = q_norm.reshape(1, -1)
    kv_norm = kv_norm.reshape(1, -1)
    qk_t = jax.ShapeDtypeStruct((MLA_HEADS, n, 2 * LANES), jnp.bfloat16)
    v_t = jax.ShapeDtypeStruct((MLA_HEADS, n, MLA_V), jnp.bfloat16)
    return pl.pallas_call(
        _mla_proj_kernel,
        grid=(n // tb,),
        in_specs=[pl.BlockSpec((tb, D_MODEL), lambda i: (i, 0)), full(g),
                  pl.BlockSpec((tb, LANES), lambda i: (i, 0)), pl.BlockSpec((tb, LANES), lambda i: (i, 0)),
                  full(win), full(q_norm), full(wq), full(kv_norm), full(wkv)],
        out_specs=[pl.BlockSpec((MLA_HEADS, tb, 2 * LANES), lambda i: (0, i, 0)),
                   pl.BlockSpec((MLA_HEADS, tb, 2 * LANES), lambda i: (0, i, 0)),
                   pl.BlockSpec((MLA_HEADS, tb, MLA_V), lambda i: (0, i, 0))],
        out_shape=[qk_t, qk_t, v_t],
        compiler_params=_params("parallel"),
        name="mla_proj",
    )(h, g, cos_t, sin_t, win, q_norm, wq, kv_norm, wkv)


def _attn_kernel(qi_ref, ki_ref, q_ref, k_ref, v_ref, o_ref, m_sc, l_sc, acc_sc):
    step = pl.program_id(1)
    qi = qi_ref[step]
    ki = ki_ref[step]

    @pl.when(ki == 0)
    def _():
        m_sc[...] = jnp.full(m_sc.shape, NEG_INF, jnp.float32)
        l_sc[...] = jnp.zeros(l_sc.shape, jnp.float32)
        acc_sc[...] = jnp.zeros(acc_sc.shape, jnp.float32)

    def update(masked):
        sc = lax.dot_general(q_ref[...], k_ref[...], (((1,), (1,)), ((), ())),
                             preferred_element_type=jnp.float32)
        if masked:
            row = lax.broadcasted_iota(jnp.int32, sc.shape, 0)
            col = lax.broadcasted_iota(jnp.int32, sc.shape, 1)
            sc = jnp.where(col <= row, sc, NEG_INF)
        m_prev = m_sc[...]
        m_next = jnp.maximum(m_prev, jnp.max(sc, axis=1, keepdims=True))
        p = jnp.exp(sc - m_next[:, :1])
        alpha = jnp.exp(m_prev - m_next)
        l_sc[...] = alpha * l_sc[...] + jnp.sum(p, axis=1, keepdims=True)
        acc_sc[...] = alpha * acc_sc[...] + jnp.dot(p.astype(jnp.bfloat16), v_ref[...],
                                                    preferred_element_type=jnp.float32)
        m_sc[...] = m_next

    @pl.when(ki < qi)
    def _():
        update(False)

    @pl.when(ki == qi)
    def _():
        update(True)
        o_ref[...] = (acc_sc[...] / l_sc[...]).astype(o_ref.dtype)


def _attention(q, k, v, seq, blk, b, q_lo, q_hi, after=None):
    nb = seq // blk
    qi = jnp.array([i for i in range(q_lo, q_hi) for _ in range(i + 1)], jnp.int32)
    ki = jnp.array([j for i in range(q_lo, q_hi) for j in range(i + 1)], jnp.int32)
    if after is not None:
        qi = qi + jnp.minimum(after, 0)
    grid_spec = pltpu.PrefetchScalarGridSpec(
        num_scalar_prefetch=2,
        grid=(MLA_HEADS, qi.shape[0]),
        in_specs=[pl.BlockSpec((None, blk, 2 * LANES), lambda h, s, qi, ki: (h, b * nb + qi[s], 0)),
                  pl.BlockSpec((None, blk, 2 * LANES), lambda h, s, qi, ki: (h, b * nb + ki[s], 0)),
                  pl.BlockSpec((None, blk, MLA_V), lambda h, s, qi, ki: (h, b * nb + ki[s], 0))],
        out_specs=pl.BlockSpec((blk, MLA_V), lambda h, s, qi, ki: (qi[s] - q_lo, h)),
        scratch_shapes=[pltpu.VMEM((blk, LANES), jnp.float32), pltpu.VMEM((blk, LANES), jnp.float32),
                        pltpu.VMEM((blk, MLA_V), jnp.float32)],
    )
    return pl.pallas_call(
        _attn_kernel,
        grid_spec=grid_spec,
        out_shape=jax.ShapeDtypeStruct(((q_hi - q_lo) * blk, MLA_HEADS * MLA_V), jnp.bfloat16),
        compiler_params=_params("parallel", "arbitrary"),
        name="mla_attention",
    )(qi, ki, q, k, v)


def _col_reduce(x, op, reduce_fn):
    slabs = [x[i:i + 8] for i in range(0, x.shape[0], 8)]
    while len(slabs) > 1:
        nxt = [op(slabs[i], slabs[i + 1]) for i in range(0, len(slabs) - 1, 2)]
        if len(slabs) % 2:
            nxt.append(slabs[-1])
        slabs = nxt
    return reduce_fn(slabs[0], axis=0, keepdims=True)


def _top_rows(vals, ids, count, out_rows):
    t = vals.shape[1]
    big = jnp.int32(2 ** 30)
    orow = lax.broadcasted_iota(jnp.int32, (out_rows, t), 0)

    def body(r, carry):
        cur, ov, oi = carry
        m = _col_reduce(cur, jnp.maximum, jnp.max)
        pick = _col_reduce(jnp.where(cur == m, ids, big), jnp.minimum, jnp.min)
        cur = jnp.where(ids == pick, NEG_INF, cur)
        ov = jnp.where(orow == r, m, ov)
        oi = jnp.where(orow == r, pick, oi)
        return cur, ov, oi

    init = (vals, jnp.zeros((out_rows, t), jnp.float32), jnp.zeros((out_rows, t), jnp.int32))
    _, ov, oi = lax.fori_loop(0, count, body, init)
    return ov, oi


_ROW_SLABS = [(0, 0, 16), (1, 0, 8)] + [(a, 0, 8) for a in range(2, 8)]
_COL_SLAB = (8, 16, 0)
_PAIR_ROWS = sum(hi - lo for _, lo, hi in _ROW_SLABS) + (_COL_SLAB[1] - _COL_SLAB[0])


def _route_kernel(o_ref, h_ref, wo_ref, g_ref, wqt_ref, keys_ref, pos_ref,
                  hn_out, xn_out, eid_out, gate_out, qt_sc, v_sc, i_sc):
    tb = h_ref.shape[0]
    hnew = h_ref[...] + jnp.dot(o_ref[...], wo_ref[...], preferred_element_type=jnp.float32)
    hn_out[...] = hnew
    xn = _rms(hnew, g_ref[...])
    xn_out[...] = _pack_pairs(xn)
    qt_sc[...] = lax.dot_general(wqt_ref[...], xn.astype(jnp.bfloat16), (((1,), (1,)), ((), ())),
                                 preferred_element_type=jnp.float32).astype(jnp.bfloat16)
    key_ids = lax.broadcasted_iota(jnp.int32, (PEER_NKEYS, tb), 0)

    def group(g, carry):
        row0 = pl.multiple_of(g * PEER_HALF, PEER_HALF)
        st = jnp.dot(keys_ref[g], qt_sc[pl.ds(row0, PEER_HALF), :], preferred_element_type=jnp.float32)
        tv, ti = _top_rows(st, key_ids, PEER_TOPK, PEER_TOPK)
        out0 = pl.multiple_of(g * PEER_TOPK, PEER_TOPK)
        v_sc[pl.ds(out0, PEER_TOPK), :] = tv
        i_sc[pl.ds(out0, PEER_TOPK), :] = ti
        return carry

    lax.fori_loop(0, 2 * PEER_HEADS, group, 0)

    pos = pos_ref[...]

    def head(hd, carry):
        base = pl.multiple_of(hd * 2 * PEER_TOPK, 2 * PEER_TOPK)
        v1 = v_sc[pl.ds(base, PEER_TOPK), :]
        i1 = i_sc[pl.ds(base, PEER_TOPK), :]
        v2 = v_sc[pl.ds(base + PEER_TOPK, PEER_TOPK), :]
        i2 = i_sc[pl.ds(base + PEER_TOPK, PEER_TOPK), :]
        cv, ce = [], []
        for a, lo, hi in _ROW_SLABS:
            cv.append(v1[a:a + 1, :] + v2[lo:hi, :])
            ce.append(i1[a:a + 1, :] * PEER_NKEYS + i2[lo:hi, :])
        a_lo, a_hi, b = _COL_SLAB
        cv.append(v1[a_lo:a_hi, :] + v2[b:b + 1, :])
        ce.append(i1[a_lo:a_hi, :] * PEER_NKEYS + i2[b:b + 1, :])
        cv = jnp.concatenate(cv, axis=0)
        ce = jnp.concatenate(ce, axis=0)
        tv, tp = _top_rows(cv, jnp.broadcast_to(pos, cv.shape), PEER_TOPK, PEER_TOPK)
        te = jnp.zeros((PEER_TOPK, tb), jnp.int32)
        orow = lax.broadcasted_iota(jnp.int32, (PEER_TOPK, tb), 0)
        for r in range(PEER_TOPK):
            e_r = jnp.sum(jnp.where(pos == tp[r:r + 1, :], ce, 0), axis=0, keepdims=True)
            te = jnp.where(orow == r, e_r, te)
        ex = jnp.exp(tv - tv[0:1, :])
        gates = ex / jnp.sum(ex, axis=0, keepdims=True)
        out0 = pl.multiple_of(hd * PEER_TOPK, PEER_TOPK)
        v_sc[pl.ds(out0, PEER_TOPK), :] = gates
        i_sc[pl.ds(out0, PEER_TOPK), :] = te
        return carry

    lax.fori_loop(0, PEER_HEADS, head, 0)

    eid_out[...] = jnp.transpose(i_sc[:PEER_SLOTS, :].astype(jnp.float32)).astype(jnp.int32)
    gate_out[...] = jnp.transpose(v_sc[:PEER_SLOTS, :])


def _pair_positions():
    pos = [a * PEER_TOPK + b for a, lo, hi in _ROW_SLABS for b in range(lo, hi)]
    a_lo, a_hi, b = _COL_SLAB
    pos += [a * PEER_TOPK + b for a in range(a_lo, a_hi)]
    return jnp.array(pos, jnp.int32).reshape(_PAIR_ROWS, 1)


def _route(o, h, h_row0, w_out, g, wqt, keys):
    n = o.shape[0]
    tb = TOKEN_BLOCK
    full = lambda a: pl.BlockSpec(a.shape, lambda i: (0,) * a.ndim)
    g = g.reshape(1, -1)
    pos = _pair_positions()
    row = pl.BlockSpec((tb, D_MODEL), lambda i: (i, 0))
    h_row = pl.BlockSpec((tb, D_MODEL), lambda i: (i + h_row0 // tb, 0))
    f32 = jnp.float32
    return pl.pallas_call(
        _route_kernel,
        grid=(n // tb,),
        in_specs=[row, h_row, full(w_out), full(g), full(wqt), full(keys), full(pos)],
        out_specs=[row, pl.BlockSpec((tb, ROW_WORDS), lambda i: (i, 0)),
                   pl.BlockSpec((tb, PEER_SLOTS), lambda i: (i, 0)),
                   pl.BlockSpec((tb, PEER_SLOTS), lambda i: (i, 0))],
        out_shape=[jax.ShapeDtypeStruct((n, D_MODEL), f32), jax.ShapeDtypeStruct((n, ROW_WORDS), jnp.uint32),
                   jax.ShapeDtypeStruct((n, PEER_SLOTS), jnp.int32),
                   jax.ShapeDtypeStruct((n, PEER_SLOTS), f32)],
        scratch_shapes=[pltpu.VMEM((2 * PEER_HEADS * PEER_HALF, tb), jnp.bfloat16),
                        pltpu.VMEM((2 * PEER_HEADS * PEER_TOPK, tb), f32),
                        pltpu.VMEM((2 * PEER_HEADS * PEER_TOPK, tb), jnp.int32)],
        compiler_params=_params("parallel"),
        name="peer_route",
    )(o, h, w_out, g, wqt, keys, pos)


def _sc_mesh():
    return plsc.VectorSubcoreMesh(core_axis_name="c", subcore_axis_name="s")


def _sc_params():
    return pltpu.CompilerParams(needs_layout_passes=False)


def _worker_id():
    return lax.axis_index("s") * SC_CORES + lax.axis_index("c")


def _gather_rows(tab_hbm, idx_ref, dst_ref, sem):
    return pltpu.make_async_copy(tab_hbm.at[idx_ref], dst_ref, sem)


def _unpack16(word):
    lo = lax.bitcast_convert_type(word << 16, jnp.float32)
    hi = lax.bitcast_convert_type(word & jnp.uint32(0xFFFF0000), jnp.float32)
    return lo, hi


def _as_bf16(word):
    return plsc.bitcast(word, jnp.bfloat16)


def _sc_dots(tab, idx, x):
    t = x.shape[0]
    workers = SC_CORES * SC_SUBCORES
    tok_w = t // workers
    per_worker = tok_w * PEER_SLOTS
    w = GATHER_WINDOW
    slots = DOTS_SLOTS
    n_win = per_worker // w
    win_per_tok = PEER_SLOTS // w
    n_chunks = ROW_WORDS // SC_LANES
    f32 = jnp.float32

    @functools.partial(
        pl.kernel, out_type=jax.ShapeDtypeStruct((t * PEER_SLOTS,), f32), mesh=_sc_mesh(),
        scratch_types=[pltpu.VMEM((per_worker,), jnp.int32),
                       pltpu.VMEM((tok_w, ROW_WORDS), jnp.uint32),
                       pltpu.VMEM((slots, w, ROW_WORDS), jnp.uint32),
                       pltpu.VMEM((per_worker,), f32),
                       pltpu.VMEM((w * SC_LANES,), f32),
                       pltpu.SemaphoreType.DMA((slots,))],
        compiler_params=_sc_params(), name="peer_dots")
    def dots(tab_hbm, i_hbm, x_hbm, act_hbm, idx_v, x_v, rows, act_v, part_v, sem):
        wid = _worker_id()
        base = pl.multiple_of(wid * per_worker, per_worker)
        tok0 = pl.multiple_of(wid * tok_w, tok_w)
        pltpu.sync_copy(i_hbm.at[pl.ds(base, per_worker)], idx_v)
        pltpu.sync_copy(x_hbm.at[pl.ds(tok0, tok_w)], x_v)

        def gather(win, slot):
            ix = idx_v.at[pl.ds(pl.multiple_of(win * w, w), w)]
            return _gather_rows(tab_hbm, ix, rows.at[slot], sem.at[slot])

        for s in range(slots - 1):
            gather(s, s).start()
        lane = lax.iota(jnp.int32, SC_LANES)

        @pl.loop(0, n_win)
        def _(win):
            slot = lax.rem(win, slots)
            nxt = win + slots - 1

            @pl.when(nxt < n_win)
            def _():
                gather(nxt, lax.rem(nxt, slots)).start()

            gather(win, slot).wait()
            tok = win // win_per_tok

            @plsc.parallel_loop(0, w // DOT_ROWS)
            def _(g):
                r0 = g * DOT_ROWS
                acc_lo = [jnp.zeros((SC_LANES,), f32) for _ in range(DOT_ROWS)]
                acc_hi = [jnp.zeros((SC_LANES,), f32) for _ in range(DOT_ROWS)]
                for c in range(0, n_chunks, 2):
                    xa = _as_bf16(x_v[tok, pl.ds(c * SC_LANES, SC_LANES)])
                    xb = _as_bf16(x_v[tok, pl.ds((c + 1) * SC_LANES, SC_LANES)])
                    for r in range(DOT_ROWS):
                        ra = _as_bf16(rows[slot, r0 + r, pl.ds(c * SC_LANES, SC_LANES)])
                        rb = _as_bf16(rows[slot, r0 + r, pl.ds((c + 1) * SC_LANES, SC_LANES)])
                        lo, hi = _unpack16(plsc.bitcast(ra * xa + rb * xb, jnp.uint32))
                        acc_lo[r] = acc_lo[r] + lo
                        acc_hi[r] = acc_hi[r] + hi
                for r in range(DOT_ROWS):
                    part_v[pl.ds(pl.multiple_of((r0 + r) * SC_LANES, SC_LANES), SC_LANES)] = acc_lo[r] + acc_hi[r]

            for blk in range(w // SC_LANES):
                res = jnp.zeros((SC_LANES,), f32)
                for l in range(SC_LANES):
                    res = res + plsc.load_gather(part_v, [lane * SC_LANES + (blk * SC_LANES * SC_LANES + l)])
                act_v[pl.ds(pl.multiple_of(win * w + blk * SC_LANES, SC_LANES), SC_LANES)] = res

        pltpu.sync_copy(act_v, act_hbm.at[pl.ds(base, per_worker)])

    return dots(tab, idx, x)


def _sc_axpy(tab, idx, wts):
    p = idx.shape[0]
    t = p // PEER_SLOTS
    workers = SC_CORES * SC_SUBCORES
    tok_w = t // workers
    per_worker = tok_w * PEER_SLOTS
    w = GATHER_WINDOW
    slots = AXPY_SLOTS
    n_win = per_worker // w
    win_per_tok = PEER_SLOTS // w
    passes = ROW_WORDS // (SC_LANES * AXPY_CHUNKS)
    f32 = jnp.float32

    @functools.partial(
        pl.kernel, out_type=jax.ShapeDtypeStruct((t, D_MODEL), f32), mesh=_sc_mesh(),
        scratch_types=[pltpu.VMEM((per_worker,), jnp.int32),
                       pltpu.VMEM((per_worker,), jnp.uint32),
                       pltpu.VMEM((slots, w, ROW_WORDS), jnp.uint32),
                       pltpu.VMEM((2, D_MODEL), f32),
                       pltpu.SemaphoreType.DMA((slots,)),
                       pltpu.SemaphoreType.DMA((2,))],
        compiler_params=_sc_params(), name="peer_axpy")
    def axpy(tab_hbm, i_hbm, w_hbm, y_hbm, idx_v, w_v, rows, y_v, sem, sem_y):
        wid = _worker_id()
        base = pl.multiple_of(wid * per_worker, per_worker)
        tok0 = wid * tok_w
        pltpu.sync_copy(i_hbm.at[pl.ds(base, per_worker)], idx_v)
        pltpu.sync_copy(w_hbm.at[pl.ds(base, per_worker)], w_v)

        def gather(win, slot):
            ix = idx_v.at[pl.ds(pl.multiple_of(win * w, w), w)]
            return _gather_rows(tab_hbm, ix, rows.at[slot], sem.at[slot])

        def y_write(tok, buf):
            return pltpu.make_async_copy(y_v.at[buf], y_hbm.at[tok0 + tok], sem_y.at[buf])

        for s in range(slots - 1):
            gather(s, s).start()

        @pl.loop(0, n_win)
        def _(win):
            slot = lax.rem(win, slots)
            nxt = win + slots - 1

            @pl.when(nxt < n_win)
            def _():
                gather(nxt, lax.rem(nxt, slots)).start()

            gather(win, slot).wait()
            tok = win // win_per_tok
            part = lax.rem(win, win_per_tok)
            buf = lax.rem(tok, 2)

            @pl.when(part == 0)
            def _():
                @pl.when(tok >= 2)
                def _():
                    y_write(tok - 2, buf).wait()

                for c in range(D_MODEL // SC_LANES):
                    y_v[buf, pl.ds(c * SC_LANES, SC_LANES)] = jnp.zeros((SC_LANES,), f32)

            for ps in range(passes):

                def group(g, accs):
                    accs = list(accs)
                    row0 = pl.multiple_of(g * SC_LANES, SC_LANES)
                    w_grp = w_v[pl.ds(pl.multiple_of(win * w + row0, SC_LANES), SC_LANES)]
                    for k in range(0, SC_LANES, 2):
                        wa = _as_bf16(jnp.take_along_axis(w_grp, jnp.full((SC_LANES,), k, jnp.int32), axis=0))
                        wb = _as_bf16(jnp.take_along_axis(w_grp, jnp.full((SC_LANES,), k + 1, jnp.int32), axis=0))
                        for c in range(AXPY_CHUNKS):
                            col = (ps * AXPY_CHUNKS + c) * SC_LANES
                            ra = _as_bf16(rows[slot, row0 + k, pl.ds(col, SC_LANES)])
                            rb = _as_bf16(rows[slot, row0 + k + 1, pl.ds(col, SC_LANES)])
                            lo, hi = _unpack16(plsc.bitcast(ra * wa + rb * wb, jnp.uint32))
                            accs[2 * c] = accs[2 * c] + lo
                            accs[2 * c + 1] = accs[2 * c + 1] + hi
                    return tuple(accs)

                zero = tuple(jnp.zeros((SC_LANES,), f32) for _ in range(2 * AXPY_CHUNKS))
                accs = lax.fori_loop(0, w // SC_LANES, group, zero)
                for c in range(AXPY_CHUNKS):
                    col = (ps * AXPY_CHUNKS + c) * SC_LANES
                    y_v[buf, pl.ds(col, SC_LANES)] = y_v[buf, pl.ds(col, SC_LANES)] + accs[2 * c]
                    y_v[buf, pl.ds(ROW_WORDS + col, SC_LANES)] = (
                        y_v[buf, pl.ds(ROW_WORDS + col, SC_LANES)] + accs[2 * c + 1])

            @pl.when(part == win_per_tok - 1)
            def _():
                y_write(tok, buf).start()

        for tok in (tok_w - 2, tok_w - 1):
            y_write(tok, tok % 2).wait()

    return axpy(tab, idx, wts)


def _slot_weight_kernel(act_ref, gate_ref, o_ref):
    act = act_ref[...]
    wts = gate_ref[...] * (0.5 * act * (1.0 + lax.erf(act * (2.0 ** -0.5))))
    hi = _bf16_bits(wts)
    o_ref[...] = hi | (hi >> 16)


def _slot_weights(act, gates):
    n = gates.shape[0]
    tb = 1024
    blk = pl.BlockSpec((tb, PEER_SLOTS), lambda i: (i, 0))
    return pl.pallas_call(
        _slot_weight_kernel, grid=(n // tb,), in_specs=[blk, blk], out_specs=blk,
        out_shape=jax.ShapeDtypeStruct((n, PEER_SLOTS), jnp.uint32),
        compiler_params=_params("parallel"), name="peer_slot_weights",
    )(act, gates)


def _residual_kernel(h_ref, y_ref, g_ref, o_ref, *, final_norm):
    out = h_ref[...] + y_ref[...]
    o_ref[...] = _rms(out, g_ref[...]) if final_norm else out


def _residual(h, y, g_final, final_norm):
    n = h.shape[0]
    tb = 512
    blk = pl.BlockSpec((tb, D_MODEL), lambda i: (i, 0))
    g_final = g_final.reshape(1, -1)
    return pl.pallas_call(
        functools.partial(_residual_kernel, final_norm=final_norm),
        grid=(n // tb,), in_specs=[blk, blk, pl.BlockSpec((1, D_MODEL), lambda i: (0, 0))], out_specs=blk,
        out_shape=jax.ShapeDtypeStruct((n, D_MODEL), jnp.float32),
        compiler_params=_params("parallel"), name="peer_residual",
    )(h, y, g_final)


def _peer(xn, eid, gates, h, tab_u, tab_v, g_final, final_norm):
    n = xn.shape[0]
    tc = min(GATHER_TOKENS, n)
    chunks = [slice(c * tc, (c + 1) * tc) for c in range(n // tc)]
    act = jnp.concatenate([_sc_dots(tab_u, eid[tok].reshape(-1), xn[tok]) for tok in chunks])
    wts = _slot_weights(act.reshape(n, PEER_SLOTS), gates)
    y = jnp.concatenate([_sc_axpy(tab_v, eid[tok].reshape(-1), wts[tok].reshape(-1)) for tok in chunks])
    return _residual(h, y, g_final, final_norm)


_N_SUB = HG_CHUNK // HG_SUB
_OFF_PAIRS = [(i, j) for i in range(_N_SUB) for j in range(i)]


def _cum_matrix():
    t = jnp.arange(HG_CHUNK)[:, None]
    r = jnp.arange(HG_CHUNK)[None, :]
    sub = t // HG_SUB
    incl = r <= t
    before = r < sub * HG_SUB
    end = r < (sub + 1) * HG_SUB
    return jnp.concatenate([incl, before, end], axis=0).astype(jnp.float32)


def _hgrn_kernel(h_ref, g_ref, w_ref, lb_ref, on_ref, cum_ref, st_in_ref, o_ref, st_out_ref,
                 z_sc, st_sc, lb_sc):
    @pl.when(pl.program_id(0) == 0)
    def _():
        st_sc[...] = st_in_ref[...]

    tb = h_ref.shape[0]
    hn = _rms(h_ref[...], g_ref[...]).astype(jnp.bfloat16)
    z_sc[...] = jnp.dot(hn, w_ref[...], preferred_element_type=jnp.float32)
    lbr = lb_ref[...]
    mx = jnp.max(lbr, axis=0, keepdims=True)
    ex = jnp.exp(lbr - mx)
    prob = ex / jnp.sum(ex, axis=0, keepdims=True)
    lb_sc[...] = jnp.broadcast_to((prob[0:1, :] + prob[1:2, :]) - prob[0:1, :], lb_sc.shape)
    wf = HG_HEADS * HG_DK
    sub_row = lax.broadcasted_iota(jnp.int32, (HG_SUB, HG_DK), 0)
    lane64 = lax.broadcasted_iota(jnp.int32, (HG_SUB, HG_CHUNK), 1)

    mxu = jnp.bfloat16
    nt = (((1,), (1,)), ((), ()))

    def head(hd, carry):
        c0 = pl.multiple_of(hd * HG_DK, HG_DK)
        lb = lb_sc[0:1, pl.ds(c0, HG_DK)]
        on = on_ref[0:1, pl.ds(c0, HG_DK)]
        st = st_sc[hd]
        for ch in range(tb // HG_CHUNK):
            rows = slice(ch * HG_CHUNK, (ch + 1) * HG_CHUNK)
            qp = z_sc[rows, pl.ds(c0, HG_DK)]
            fp = z_sc[rows, pl.ds(pl.multiple_of(wf + c0, HG_DK), HG_DK)]
            v = z_sc[rows, pl.ds(pl.multiple_of(2 * wf + c0, HG_DK), HG_DK)]
            gp = z_sc[rows, pl.ds(pl.multiple_of(2 * wf + HG_HEADS * HG_DV + c0, HG_DK), HG_DK)]
            f = lb + (1.0 - lb) * jax.nn.sigmoid(fp)
            lf = jnp.log(f)
            k = 1.0 - f
            q = qp * jax.nn.sigmoid(qp)
            cums = jnp.dot(cum_ref[...], lf, precision=lax.Precision.HIGHEST,
                           preferred_element_type=jnp.float32)
            b = cums[:HG_CHUNK]
            b_start = cums[HG_CHUNK:2 * HG_CHUNK]
            b_end = cums[2 * HG_CHUNK:]
            q_hat = q * jnp.exp(b - b_start)
            k_hat = k * jnp.exp(b_end - b)
            o_inter = lax.dot_general((q_hat * jnp.exp(b_start)).astype(mxu), st.astype(mxu), nt,
                                      preferred_element_type=jnp.float32)
            stacked = []
            for (i, j) in _OFF_PAIRS:
                d_ij = jnp.exp(b_start[i * HG_SUB:i * HG_SUB + 1, :] - b_end[j * HG_SUB:j * HG_SUB + 1, :])
                stacked.append(q_hat[i * HG_SUB:(i + 1) * HG_SUB, :] * d_ij)
            stacked = jnp.concatenate(stacked, axis=0).astype(mxu)
            off = lax.dot_general(stacked, k_hat.astype(mxu), nt,
                                  preferred_element_type=jnp.float32)
            a_rows = []
            for i in range(_N_SUB):
                blk = jnp.zeros((HG_SUB, HG_CHUNK), jnp.float32)
                for p, (pi, pj) in enumerate(_OFF_PAIRS):
                    if pi == i:
                        in_j = (lane64 >= pj * HG_SUB) & (lane64 < (pj + 1) * HG_SUB)
                        blk = jnp.where(in_j, off[p * HG_SUB:(p + 1) * HG_SUB, :], blk)
                b_blk = b[i * HG_SUB:(i + 1) * HG_SUB, :]
                q_blk = q[i * HG_SUB:(i + 1) * HG_SUB, :]
                for s in range(HG_SUB):
                    n = i * HG_SUB + s
                    e = jnp.exp(jnp.where(sub_row >= s, b_blk - b[n:n + 1, :], MASKED_LOG_DECAY))
                    col = jnp.sum(q_blk * k[n:n + 1, :] * e, axis=1, keepdims=True)
                    blk = jnp.where(lane64 == n, col, blk)
                a_rows.append(blk)
            a = jnp.concatenate(a_rows, axis=0).astype(mxu)
            o = o_inter + jnp.dot(a, v.astype(mxu), preferred_element_type=jnp.float32)
            b_last = b[HG_CHUNK - 1:HG_CHUNK, :]
            k_til = (k_hat * jnp.exp(b_last - b_end)).astype(mxu)
            upd = lax.dot_general(v.astype(mxu), k_til, (((0,), (0,)), ((), ())),
                                  preferred_element_type=jnp.float32)
            st = st * jnp.exp(b_last) + upd
            o = o * lax.rsqrt(jnp.mean(o * o, axis=-1, keepdims=True) + NORM_EPS)
            o = o * on * (gp * jax.nn.sigmoid(gp))
            o_ref[rows, pl.ds(c0, HG_DK)] = o.astype(o_ref.dtype)
        st_sc[hd] = st
        return carry

    lax.fori_loop(0, HG_HEADS, head, 0)

    @pl.when(pl.program_id(0) == pl.num_programs(0) - 1)
    def _():
        st_out_ref[...] = st_sc[...]


def _hgrn(h, state, g, w, lb_raw, out_norm):
    n = h.shape[0]
    tb = min(HG_BLOCK, n)
    g = g.reshape(1, -1)
    out_norm = out_norm.reshape(1, -1)
    cum = _cum_matrix()
    full = lambda a: pl.BlockSpec(a.shape, lambda i: (0,) * a.ndim)
    f32 = jnp.float32
    return pl.pallas_call(
        _hgrn_kernel,
        grid=(n // tb,),
        in_specs=[pl.BlockSpec((tb, D_MODEL), lambda i: (i, 0)), full(g), full(w),
                  full(lb_raw), full(out_norm), full(cum), full(state)],
        out_specs=[pl.BlockSpec((tb, HG_HEADS * HG_DV), lambda i: (i, 0)), full(state)],
        out_shape=[jax.ShapeDtypeStruct((n, HG_HEADS * HG_DV), jnp.bfloat16),
                   jax.ShapeDtypeStruct(state.shape, f32)],
        scratch_shapes=[pltpu.VMEM((tb, w.shape[1]), f32),
                        pltpu.VMEM((HG_HEADS, HG_DV, HG_DK), f32),
                        pltpu.VMEM((8, HG_HEADS * HG_DK), f32)],
        compiler_params=_params("arbitrary"),
        name="hgrn2",
    )(h, g, w, lb_raw, out_norm, cum, state)


def kernel(x, positions, ln_mix, ln_ffn, ln_final, mla_w_in, mla_q_norm, mla_w_q_up, mla_kv_norm,
           mla_w_kv_up, mla_w_out, hg_w_in, hg_lb, hg_out_norm, hg_w_out, peer_w_q, peer_sub_keys,
           peer_u, peer_v):
    batch, seq, d = x.shape
    n = batch * seq
    bf16 = jnp.bfloat16
    h = x.reshape(n, d)

    def route_weights(i):
        keys = peer_sub_keys[i].reshape(2 * PEER_HEADS, PEER_NKEYS, PEER_HALF).astype(bf16)
        return peer_w_q[i].T.astype(bf16), keys

    tables = [(_pack_table(peer_u[i]), _pack_table(peer_v[i])) for i in range(2)]
    cos_t, sin_t = _rope_tables(positions)
    win, wq, wkv = _mla_weights(mla_w_in[0], mla_w_q_up[0], mla_w_kv_up[0])
    q, k, v = _mla_proj(h, ln_mix[0], cos_t, sin_t, win, mla_q_norm[0], wq, mla_kv_norm[0], wkv)
    wqt0, keys0 = route_weights(0)
    wqt1, keys1 = route_weights(1)
    mla_wo = mla_w_out[0].astype(bf16)
    hg_wi = hg_w_in[0].astype(bf16)
    hg_wo = hg_w_out[0].astype(bf16)

    chunk = min(PIPE_TOKENS, seq)
    blk = min(ATTN_BLOCK, chunk)
    n_chunks = seq // chunk
    states = [jnp.zeros((HG_HEADS, HG_DV, HG_DK), jnp.float32) for _ in range(batch)]
    outs = {}

    routed0, routed1 = [], []

    def layer0(b, j):
        after = None
        if len(routed1) >= 2:
            after = routed1[-2][0, 0]
        elif len(routed0) >= 2:
            after = routed0[-2][0, 0]
        o = _attention(q, k, v, seq, blk, b, j * chunk // blk, (j + 1) * chunk // blk, after)
        hc, xp, eid, gates = _route(o, h, b * seq + j * chunk, mla_wo, ln_ffn[0], wqt0, keys0)
        routed0.append(eid)
        return _peer(xp, eid, gates, hc, tables[0][0], tables[0][1], ln_final, False)

    def layer1(b, j, hc):
        o, states[b] = _hgrn(hc, states[b], ln_mix[1], hg_wi, hg_lb, hg_out_norm[0])
        hc, xp, eid, gates = _route(o, hc, 0, hg_wo, ln_ffn[1], wqt1, keys1)
        routed1.append(eid)
        outs[(b, j)] = _peer(xp, eid, gates, hc, tables[1][0], tables[1][1], ln_final, True)

    pending = None
    for j in range(n_chunks):
        for b in range(batch):
            hc = layer0(b, j)
            if pending is not None:
                layer1(*pending)
            pending = (b, j, hc)
    layer1(*pending)
    out = jnp.concatenate([outs[(b, j)] for b in range(batch) for j in range(n_chunks)], axis=0)
    return out.reshape(batch, seq, d)
```

```python
import functools

import jax
import jax.numpy as jnp
from jax import lax
from jax.experimental import pallas as pl
from jax.experimental.pallas import tpu as pltpu
from jax.experimental.pallas import tpu_sc as plsc

D_MODEL = 1024
NORM_EPS = 1e-6
MLA_HEADS = 8
MLA_Q_LORA = 384
MLA_KV_LORA = 256
MLA_NOPE = 128
MLA_ROPE = 64
MLA_V = 128
ROPE_THETA = 10000.0
HG_HEADS = 8
HG_DK = 128
HG_DV = 128
PEER_HEADS = 8
PEER_NKEYS = 128
PEER_HALF = 128
PEER_TOPK = 16
PEER_SLOTS = PEER_HEADS * PEER_TOPK

LANES = 128
SC_CORES = 2
SC_SUBCORES = 16
VMEM_LIMIT = 48 * 1024 * 1024

ROW_WORDS = D_MODEL // 2
TOKEN_BLOCK = 256
ATTN_BLOCK = 1024
HG_BLOCK = 256
HG_CHUNK = 64
HG_SUB = 16
PIPE_TOKENS = 2048
GATHER_WINDOW = 32
DOTS_SLOTS = 5
AXPY_SLOTS = 7
GATHER_TOKENS = 1024
SC_LANES = 16
DOT_ROWS = 8
AXPY_CHUNKS = 8
NEG_INF = float("-inf")
MASKED_LOG_DECAY = -1e30


def _rms(x, g):
    return x * lax.rsqrt(jnp.mean(x * x, axis=-1, keepdims=True) + NORM_EPS) * g


def _params(*sem):
    return pltpu.CompilerParams(dimension_semantics=sem, vmem_limit_bytes=VMEM_LIMIT)


def _bf16_bits(x):
    return pltpu.bitcast(x.astype(jnp.bfloat16).astype(jnp.float32), jnp.uint32)


def _pack_pairs(t):
    return (_bf16_bits(t[:, :ROW_WORDS]) >> 16) | _bf16_bits(t[:, ROW_WORDS:])


def _pack_kernel(t_ref, o_ref):
    o_ref[...] = _pack_pairs(t_ref[...])


def _pack_table(tab):
    e, d = tab.shape
    rows = 512
    return pl.pallas_call(
        _pack_kernel,
        grid=(e // rows,),
        in_specs=[pl.BlockSpec((rows, d), lambda i: (i, 0))],
        out_specs=pl.BlockSpec((rows, d // 2), lambda i: (i, 0)),
        out_shape=jax.ShapeDtypeStruct((e, d // 2), jnp.uint32),
        compiler_params=_params("parallel"),
        name="pack_table",
    )(tab)


def _rope_kernel(pos_ref, inv_ref, sign_ref, c_ref, s_ref):
    ang = pos_ref[...].astype(jnp.float32) * inv_ref[...]
    c_ref[...] = jnp.cos(ang)
    s_ref[...] = jnp.sin(ang) * sign_ref[...]


def _rope_tables(positions):
    n = positions.size
    lane = jnp.arange(LANES)
    inv = ROPE_THETA ** (-(2 * (lane % (MLA_ROPE // 2))).astype(jnp.float32) / MLA_ROPE)
    sign = jnp.where((lane % MLA_ROPE) < MLA_ROPE // 2, -1.0, 1.0).astype(jnp.float32)
    tb = 1024
    out = jax.ShapeDtypeStruct((n, LANES), jnp.float32)
    return pl.pallas_call(
        _rope_kernel,
        grid=(n // tb,),
        in_specs=[pl.BlockSpec((tb, 1), lambda i: (i, 0)),
                  pl.BlockSpec((1, LANES), lambda i: (0, 0)),
                  pl.BlockSpec((1, LANES), lambda i: (0, 0))],
        out_specs=[pl.BlockSpec((tb, LANES), lambda i: (i, 0))] * 2,
        out_shape=[out, out],
        compiler_params=_params("parallel"),
        name="rope_tables",
    )(positions.reshape(n, 1), inv.reshape(1, LANES), sign.reshape(1, LANES))


def _mla_proj_kernel(x_ref, g_ref, c_ref, s_ref, win_ref, qn_ref, wq_ref, kvn_ref, wkv_ref,
                     q_out, k_out, v_out):
    hn = _rms(x_ref[...], g_ref[...]).astype(jnp.bfloat16)
    z = jnp.dot(hn, win_ref[...], preferred_element_type=jnp.float32)
    c = c_ref[...]
    s = s_ref[...]
    o_kv = MLA_Q_LORA
    o_kr = MLA_Q_LORA + MLA_KV_LORA
    k_rope = (z[:, o_kr:o_kr + LANES] * c + z[:, o_kr + LANES:o_kr + 2 * LANES] * s).astype(jnp.bfloat16)
    cq = _rms(z[:, :MLA_Q_LORA], qn_ref[...]).astype(jnp.bfloat16)
    q = jnp.dot(cq, wq_ref[...], preferred_element_type=jnp.float32)
    ckv = _rms(z[:, o_kv:o_kr], kvn_ref[...]).astype(jnp.bfloat16)
    kv = jnp.dot(ckv, wkv_ref[...], preferred_element_type=jnp.float32)
    scale = (MLA_NOPE + MLA_ROPE) ** -0.5
    hw = MLA_HEADS * LANES
    for h in range(MLA_HEADS):
        sl = slice(h * LANES, (h + 1) * LANES)
        q_out[h, :, :LANES] = (q[:, sl] * scale).astype(jnp.bfloat16)
        qr = q[:, hw + h * LANES:hw + (h + 1) * LANES] * c + q[:, 2 * hw + h * LANES:2 * hw + (h + 1) * LANES] * s
        q_out[h, :, LANES:] = (qr * scale).astype(jnp.bfloat16)
        k_out[h, :, :LANES] = kv[:, sl].astype(jnp.bfloat16)
        k_out[h, :, LANES:] = k_rope
        v_out[h] = kv[:, hw + h * LANES:hw + (h + 1) * LANES].astype(jnp.bfloat16)


def _swap_halves(w):
    half = w.shape[-1] // 2
    return jnp.concatenate([w[..., half:], w[..., :half]], axis=-1)


def _pad_lanes(w):
    return jnp.pad(w, [(0, 0)] * (w.ndim - 1) + [(0, LANES - w.shape[-1])])


def _mla_weights(w_in, w_q_up, w_kv_up):
    o_kr = MLA_Q_LORA + MLA_KV_LORA
    w_kr = w_in[:, o_kr:]
    win = jnp.concatenate([w_in[:, :o_kr], _pad_lanes(w_kr), _pad_lanes(_swap_halves(w_kr))], axis=1)
    wq = w_q_up.reshape(MLA_Q_LORA, MLA_HEADS, MLA_NOPE + MLA_ROPE)
    wq_n = wq[:, :, :MLA_NOPE].reshape(MLA_Q_LORA, -1)
    wq_r = _pad_lanes(wq[:, :, MLA_NOPE:]).reshape(MLA_Q_LORA, -1)
    wq_rs = _pad_lanes(_swap_halves(wq[:, :, MLA_NOPE:])).reshape(MLA_Q_LORA, -1)
    wqp = jnp.concatenate([wq_n, wq_r, wq_rs], axis=1)
    wkv = w_kv_up.reshape(MLA_KV_LORA, MLA_HEADS, MLA_NOPE + MLA_V)
    wkvp = jnp.concatenate([wkv[:, :, :MLA_NOPE].reshape(MLA_KV_LORA, -1),
                            wkv[:, :, MLA_NOPE:].reshape(MLA_KV_LORA, -1)], axis=1)
    return win.astype(jnp.bfloat16), wqp.astype(jnp.bfloat16), wkvp.astype(jnp.bfloat16)


def _mla_proj(h, g, cos_t, sin_t, win, q_norm, wq, kv_norm, wkv):
    n = h.shape[0]
    tb = TOKEN_BLOCK
    full = lambda a: pl.BlockSpec(a.shape, lambda i: (0,) * a.ndim)
    g = g.reshape(1, -1)
    q_norm = q_norm.reshape(1, -1)
    kv_norm = kv_norm.reshape(1, -1)
    qk_t = jax.ShapeDtypeStruct((MLA_HEADS, n, 2 * LANES), jnp.bfloat16)
    v_t = jax.ShapeDtypeStruct((MLA_HEADS, n, MLA_V), jnp.bfloat16)
    return pl.pallas_call(
        _mla_proj_kernel,
        grid=(n // tb,),
        in_specs=[pl.BlockSpec((tb, D_MODEL), lambda i: (i, 0)), full(g),
                  pl.BlockSpec((tb, LANES), lambda i: (i, 0)), pl.BlockSpec((tb, LANES), lambda i: (i, 0)),
                  full(win), full(q_norm), full(wq), full(kv_norm), full(wkv)],
        out_specs=[pl.BlockSpec((MLA_HEADS, tb, 2 * LANES), lambda i: (0, i, 0)),
                   pl.BlockSpec((MLA_HEADS, tb, 2 * LANES), lambda i: (0, i, 0)),
                   pl.BlockSpec((MLA_HEADS, tb, MLA_V), lambda i: (0, i, 0))],
        out_shape=[qk_t, qk_t, v_t],
        compiler_params=_params("parallel"),
        name="mla_proj",
    )(h, g, cos_t, sin_t, win, q_norm, wq, kv_norm, wkv)


def _attn_kernel(qi_ref, ki_ref, q_ref, k_ref, v_ref, o_ref, m_sc, l_sc, acc_sc):
    step = pl.program_id(1)
    qi = qi_ref[step]
    ki = ki_ref[step]

    @pl.when(ki == 0)
    def _():
        m_sc[...] = jnp.full(m_sc.shape, NEG_INF, jnp.float32)
        l_sc[...] = jnp.zeros(l_sc.shape, jnp.float32)
        acc_sc[...] = jnp.zeros(acc_sc.shape, jnp.float32)

    def update(masked):
        sc = lax.dot_general(q_ref[...], k_ref[...], (((1,), (1,)), ((), ())),
                             preferred_element_type=jnp.float32)
        if masked:
            row = lax.broadcasted_iota(jnp.int32, sc.shape, 0)
            col = lax.broadcasted_iota(jnp.int32, sc.shape, 1)
            sc = jnp.where(col <= row, sc, NEG_INF)
        m_prev = m_sc[...]
        m_next = jnp.maximum(m_prev, jnp.max(sc, axis=1, keepdims=True))
        p = jnp.exp(sc - m_next[:, :1])
        alpha = jnp.exp(m_prev - m_next)
        l_sc[...] = alpha * l_sc[...] + jnp.sum(p, axis=1, keepdims=True)
        acc_sc[...] = alpha * acc_sc[...] + jnp.dot(p.astype(jnp.bfloat16), v_ref[...],
                                                    preferred_element_type=jnp.float32)
        m_sc[...] = m_next

    @pl.when(ki < qi)
    def _():
        update(False)

    @pl.when(ki == qi)
    def _():
        update(True)
        o_ref[...] = (acc_sc[...] / l_sc[...]).astype(o_ref.dtype)


def _attention(q, k, v, seq, blk, b, q_lo, q_hi, after=None):
    nb = seq // blk
    qi = jnp.array([i for i in range(q_lo, q_hi) for _ in range(i + 1)], jnp.int32)
    ki = jnp.array([j for i in range(q_lo, q_hi) for j in range(i + 1)], jnp.int32)
    if after is not None:
        qi = qi + jnp.minimum(after, 0)
    grid_spec = pltpu.PrefetchScalarGridSpec(
        num_scalar_prefetch=2,
        grid=(MLA_HEADS, qi.shape[0]),
        in_specs=[pl.BlockSpec((None, blk, 2 * LANES), lambda h, s, qi, ki: (h, b * nb + qi[s], 0)),
                  pl.BlockSpec((None, blk, 2 * LANES), lambda h, s, qi, ki: (h, b * nb + ki[s], 0)),
                  pl.BlockSpec((None, blk, MLA_V), lambda h, s, qi, ki: (h, b * nb + ki[s], 0))],
        out_specs=pl.BlockSpec((blk, MLA_V), lambda h, s, qi, ki: (qi[s] - q_lo, h)),
        scratch_shapes=[pltpu.VMEM((blk, LANES), jnp.float32), pltpu.VMEM((blk, LANES), jnp.float32),
                        pltpu.VMEM((blk, MLA_V), jnp.float32)],
    )
    return pl.pallas_call(
        _attn_kernel,
        grid_spec=grid_spec,
        out_shape=jax.ShapeDtypeStruct(((q_hi - q_lo) * blk, MLA_HEADS * MLA_V), jnp.bfloat16),
        compiler_params=_params("parallel", "arbitrary"),
        name="mla_attention",
    )(qi, ki, q, k, v)


def _col_reduce(x, op, reduce_fn):
    slabs = [x[i:i + 8] for i in range(0, x.shape[0], 8)]
    while len(slabs) > 1:
        nxt = [op(slabs[i], slabs[i + 1]) for i in range(0, len(slabs) - 1, 2)]
        if len(slabs) % 2:
            nxt.append(slabs[-1])
        slabs = nxt
    return reduce_fn(slabs[0], axis=0, keepdims=True)


def _top_rows(vals, ids, count, out_rows):
    t = vals.shape[1]
    big = jnp.int32(2 ** 30)
    orow = lax.broadcasted_iota(jnp.int32, (out_rows, t), 0)

    def body(r, carry):
        cur, ov, oi = carry
        m = _col_reduce(cur, jnp.maximum, jnp.max)
        pick = _col_reduce(jnp.where(cur == m, ids, big), jnp.minimum, jnp.min)
        cur = jnp.where(ids == pick, NEG_INF, cur)
        ov = jnp.where(orow == r, m, ov)
        oi = jnp.where(orow == r, pick, oi)
        return cur, ov, oi

    init = (vals, jnp.zeros((out_rows, t), jnp.float32), jnp.zeros((out_rows, t), jnp.int32))
    _, ov, oi = lax.fori_loop(0, count, body, init)
    return ov, oi


_ROW_SLABS = [(0, 0, 16), (1, 0, 8)] + [(a, 0, 8) for a in range(2, 8)]
_COL_SLAB = (8, 16, 0)
_PAIR_ROWS = sum(hi - lo for _, lo, hi in _ROW_SLABS) + (_COL_SLAB[1] - _COL_SLAB[0])


def _route_kernel(o_ref, h_ref, wo_ref, g_ref, wqt_ref, keys_ref, pos_ref,
                  hn_out, xn_out, eid_out, gate_out, qt_sc, v_sc, i_sc):
    tb = h_ref.shape[0]
    hnew = h_ref[...] + jnp.dot(o_ref[...], wo_ref[...], preferred_element_type=jnp.float32)
    hn_out[...] = hnew
    xn = _rms(hnew, g_ref[...])
    xn_out[...] = _pack_pairs(xn)
    qt_sc[...] = lax.dot_general(wqt_ref[...], xn.astype(jnp.bfloat16), (((1,), (1,)), ((), ())),
                                 preferred_element_type=jnp.float32).astype(jnp.bfloat16)
    key_ids = lax.broadcasted_iota(jnp.int32, (PEER_NKEYS, tb), 0)

    def group(g, carry):
        row0 = pl.multiple_of(g * PEER_HALF, PEER_HALF)
        st = jnp.dot(keys_ref[g], qt_sc[pl.ds(row0, PEER_HALF), :], preferred_element_type=jnp.float32)
        tv, ti = _top_rows(st, key_ids, PEER_TOPK, PEER_TOPK)
        out0 = pl.multiple_of(g * PEER_TOPK, PEER_TOPK)
        v_sc[pl.ds(out0, PEER_TOPK), :] = tv
        i_sc[pl.ds(out0, PEER_TOPK), :] = ti
        return carry

    lax.fori_loop(0, 2 * PEER_HEADS, group, 0)

    pos = pos_ref[...]

    def head(hd, carry):
        base = pl.multiple_of(hd * 2 * PEER_TOPK, 2 * PEER_TOPK)
        v1 = v_sc[pl.ds(base, PEER_TOPK), :]
        i1 = i_sc[pl.ds(base, PEER_TOPK), :]
        v2 = v_sc[pl.ds(base + PEER_TOPK, PEER_TOPK), :]
        i2 = i_sc[pl.ds(base + PEER_TOPK, PEER_TOPK), :]
        cv, ce = [], []
        for a, lo, hi in _ROW_SLABS:
            cv.append(v1[a:a + 1, :] + v2[lo:hi, :])
            ce.append(i1[a:a + 1, :] * PEER_NKEYS + i2[lo:hi, :])
        a_lo, a_hi, b = _COL_SLAB
        cv.append(v1[a_lo:a_hi, :] + v2[b:b + 1, :])
        ce.append(i1[a_lo:a_hi, :] * PEER_NKEYS + i2[b:b + 1, :])
        cv = jnp.concatenate(cv, axis=0)
        ce = jnp.concatenate(ce, axis=0)
        tv, tp = _top_rows(cv, jnp.broadcast_to(pos, cv.shape), PEER_TOPK, PEER_TOPK)
        te = jnp.zeros((PEER_TOPK, tb), jnp.int32)
        orow = lax.broadcasted_iota(jnp.int32, (PEER_TOPK, tb), 0)
        for r in range(PEER_TOPK):
            e_r = jnp.sum(jnp.where(pos == tp[r:r + 1, :], ce, 0), axis=0, keepdims=True)
            te = jnp.where(orow == r, e_r, te)
        ex = jnp.exp(tv - tv[0:1, :])
        gates = ex / jnp.sum(ex, axis=0, keepdims=True)
        out0 = pl.multiple_of(hd * PEER_TOPK, PEER_TOPK)
        v_sc[pl.ds(out0, PEER_TOPK), :] = gates
        i_sc[pl.ds(out0, PEER_TOPK), :] = te
        return carry

    lax.fori_loop(0, PEER_HEADS, head, 0)

    eid_out[...] = jnp.transpose(i_sc[:PEER_SLOTS, :].astype(jnp.float32)).astype(jnp.int32)
    gate_out[...] = jnp.transpose(v_sc[:PEER_SLOTS, :])


def _pair_positions():
    pos = [a * PEER_TOPK + b for a, lo, hi in _ROW_SLABS for b in range(lo, hi)]
    a_lo, a_hi, b = _COL_SLAB
    pos += [a * PEER_TOPK + b for a in range(a_lo, a_hi)]
    return jnp.array(pos, jnp.int32).reshape(_PAIR_ROWS, 1)


def _route(o, h, h_row0, w_out, g, wqt, keys):
    n = o.shape[0]
    tb = TOKEN_BLOCK
    full = lambda a: pl.BlockSpec(a.shape, lambda i: (0,) * a.ndim)
    g = g.reshape(1, -1)
    pos = _pair_positions()
    row = pl.BlockSpec((tb, D_MODEL), lambda i: (i, 0))
    h_row = pl.BlockSpec((tb, D_MODEL), lambda i: (i + h_row0 // tb, 0))
    f32 = jnp.float32
    return pl.pallas_call(
        _route_kernel,
        grid=(n // tb,),
        in_specs=[row, h_row, full(w_out), full(g), full(wqt), full(keys), full(pos)],
        out_specs=[row, pl.BlockSpec((tb, ROW_WORDS), lambda i: (i, 0)),
                   pl.BlockSpec((tb, PEER_SLOTS), lambda i: (i, 0)),
                   pl.BlockSpec((tb, PEER_SLOTS), lambda i: (i, 0))],
        out_shape=[jax.ShapeDtypeStruct((n, D_MODEL), f32), jax.ShapeDtypeStruct((n, ROW_WORDS), jnp.uint32),
                   jax.ShapeDtypeStruct((n, PEER_SLOTS), jnp.int32),
                   jax.ShapeDtypeStruct((n, PEER_SLOTS), f32)],
        scratch_shapes=[pltpu.VMEM((2 * PEER_HEADS * PEER_HALF, tb), jnp.bfloat16),
                        pltpu.VMEM((2 * PEER_HEADS * PEER_TOPK, tb), f32),
                        pltpu.VMEM((2 * PEER_HEADS * PEER_TOPK, tb), jnp.int32)],
        compiler_params=_params("parallel"),
        name="peer_route",
    )(o, h, w_out, g, wqt, keys, pos)


def _sc_mesh():
    return plsc.VectorSubcoreMesh(core_axis_name="c", subcore_axis_name="s")


def _sc_params():
    return pltpu.CompilerParams(needs_layout_passes=False)


def _worker_id():
    return lax.axis_index("s") * SC_CORES + lax.axis_index("c")


def _gather_rows(tab_hbm, idx_ref, dst_ref, sem):
    return pltpu.make_async_copy(tab_hbm.at[idx_ref], dst_ref, sem)


def _unpack16(word):
    lo = lax.bitcast_convert_type(word << 16, jnp.float32)
    hi = lax.bitcast_convert_type(word & jnp.uint32(0xFFFF0000), jnp.float32)
    return lo, hi


def _as_bf16(word):
    return plsc.bitcast(word, jnp.bfloat16)


def _sc_dots(tab, idx, x):
    t = x.shape[0]
    workers = SC_CORES * SC_SUBCORES
    tok_w = t // workers
    per_worker = tok_w * PEER_SLOTS
    w = GATHER_WINDOW
    slots = DOTS_SLOTS
    n_win = per_worker // w
    win_per_tok = PEER_SLOTS // w
    n_chunks = ROW_WORDS // SC_LANES
    f32 = jnp.float32

    @functools.partial(
        pl.kernel, out_type=jax.ShapeDtypeStruct((t * PEER_SLOTS,), f32), mesh=_sc_mesh(),
        scratch_types=[pltpu.VMEM((per_worker,), jnp.int32),
                       pltpu.VMEM((tok_w, ROW_WORDS), jnp.uint32),
                       pltpu.VMEM((slots, w, ROW_WORDS), jnp.uint32),
                       pltpu.VMEM((per_worker,), f32),
                       pltpu.VMEM((w * SC_LANES,), f32),
                       pltpu.SemaphoreType.DMA((slots,))],
        compiler_params=_sc_params(), name="peer_dots")
    def dots(tab_hbm, i_hbm, x_hbm, act_hbm, idx_v, x_v, rows, act_v, part_v, sem):
        wid = _worker_id()
        base = pl.multiple_of(wid * per_worker, per_worker)
        tok0 = pl.multiple_of(wid * tok_w, tok_w)
        pltpu.sync_copy(i_hbm.at[pl.ds(base, per_worker)], idx_v)
        pltpu.sync_copy(x_hbm.at[pl.ds(tok0, tok_w)], x_v)

        def gather(win, slot):
            ix = idx_v.at[pl.ds(pl.multiple_of(win * w, w), w)]
            return _gather_rows(tab_hbm, ix, rows.at[slot], sem.at[slot])

        for s in range(slots - 1):
            gather(s, s).start()
        lane = lax.iota(jnp.int32, SC_LANES)

        @pl.loop(0, n_win)
        def _(win):
            slot = lax.rem(win, slots)
            nxt = win + slots - 1

            @pl.when(nxt < n_win)
            def _():
                gather(nxt, lax.rem(nxt, slots)).start()

            gather(win, slot).wait()
            tok = win // win_per_tok

            @plsc.parallel_loop(0, w // DOT_ROWS)
            def _(g):
                r0 = g * DOT_ROWS
                acc_lo = [jnp.zeros((SC_LANES,), f32) for _ in range(DOT_ROWS)]
                acc_hi = [jnp.zeros((SC_LANES,), f32) for _ in range(DOT_ROWS)]
                for c in range(0, n_chunks, 2):
                    xa = _as_bf16(x_v[tok, pl.ds(c * SC_LANES, SC_LANES)])
                    xb = _as_bf16(x_v[tok, pl.ds((c + 1) * SC_LANES, SC_LANES)])
                    for r in range(DOT_ROWS):
                        ra = _as_bf16(rows[slot, r0 + r, pl.ds(c * SC_LANES, SC_LANES)])
                        rb = _as_bf16(rows[slot, r0 + r, pl.ds((c + 1) * SC_LANES, SC_LANES)])
                        lo, hi = _unpack16(plsc.bitcast(ra * xa + rb * xb, jnp.uint32))
                        acc_lo[r] = acc_lo[r] + lo
                        acc_hi[r] = acc_hi[r] + hi
                for r in range(DOT_ROWS):
                    part_v[pl.ds(pl.multiple_of((r0 + r) * SC_LANES, SC_LANES), SC_LANES)] = acc_lo[r] + acc_hi[r]

            for blk in range(w // SC_LANES):
                res = jnp.zeros((SC_LANES,), f32)
                for l in range(SC_LANES):
                    res = res + plsc.load_gather(part_v, [lane * SC_LANES + (blk * SC_LANES * SC_LANES + l)])
                act_v[pl.ds(pl.multiple_of(win * w + blk * SC_LANES, SC_LANES), SC_LANES)] = res

        pltpu.sync_copy(act_v, act_hbm.at[pl.ds(base, per_worker)])

    return dots(tab, idx, x)


def _sc_axpy(tab, idx, wts):
    p = idx.shape[0]
    t = p // PEER_SLOTS
    workers = SC_CORES * SC_SUBCORES
    tok_w = t // workers
    per_worker = tok_w * PEER_SLOTS
    w = GATHER_WINDOW
    slots = AXPY_SLOTS
    n_win = per_worker // w
    win_per_tok = PEER_SLOTS // w
    passes = ROW_WORDS // (SC_LANES * AXPY_CHUNKS)
    f32 = jnp.float32

    @functools.partial(
        pl.kernel, out_type=jax.ShapeDtypeStruct((t, D_MODEL), f32), mesh=_sc_mesh(),
        scratch_types=[pltpu.VMEM((per_worker,), jnp.int32),
                       pltpu.VMEM((per_worker,), jnp.uint32),
                       pltpu.VMEM((slots, w, ROW_WORDS), jnp.uint32),
                       pltpu.VMEM((2, D_MODEL), f32),
                       pltpu.SemaphoreType.DMA((slots,)),
                       pltpu.SemaphoreType.DMA((2,))],
        compiler_params=_sc_params(), name="peer_axpy")
    def axpy(tab_hbm, i_hbm, w_hbm, y_hbm, idx_v, w_v, rows, y_v, sem, sem_y):
        wid = _worker_id()
        base = pl.multiple_of(wid * per_worker, per_worker)
        tok0 = wid * tok_w
        pltpu.sync_copy(i_hbm.at[pl.ds(base, per_worker)], idx_v)
        pltpu.sync_copy(w_hbm.at[pl.ds(base, per_worker)], w_v)

        def gather(win, slot):
            ix = idx_v.at[pl.ds(pl.multiple_of(win * w, w), w)]
            return _gather_rows(tab_hbm, ix, rows.at[slot], sem.at[slot])

        def y_write(tok, buf):
            return pltpu.make_async_copy(y_v.at[buf], y_hbm.at[tok0 + tok], sem_y.at[buf])

        for s in range(slots - 1):
            gather(s, s).start()

        @pl.loop(0, n_win)
        def _(win):
            slot = lax.rem(win, slots)
            nxt = win + slots - 1

            @pl.when(nxt < n_win)
            def _():
                gather(nxt, lax.rem(nxt, slots)).start()

            gather(win, slot).wait()
            tok = win // win_per_tok
            part = lax.rem(win, win_per_tok)
            buf = lax.rem(tok, 2)

            @pl.when(part == 0)
            def _():
                @pl.when(tok >= 2)
                def _():
                    y_write(tok - 2, buf).wait()

                for c in range(D_MODEL // SC_LANES):
                    y_v[buf, pl.ds(c * SC_LANES, SC_LANES)] = jnp.zeros((SC_LANES,), f32)

            for ps in range(passes):

                def group(g, accs):
                    accs = list(accs)
                    row0 = pl.multiple_of(g * SC_LANES, SC_LANES)
                    w_grp = w_v[pl.ds(pl.multiple_of(win * w + row0, SC_LANES), SC_LANES)]
                    for k in range(0, SC_LANES, 2):
                        wa = _as_bf16(jnp.take_along_axis(w_grp, jnp.full((SC_LANES,), k, jnp.int32), axis=0))
                        wb = _as_bf16(jnp.take_along_axis(w_grp, jnp.full((SC_LANES,), k + 1, jnp.int32), axis=0))
                        for c in range(AXPY_CHUNKS):
                            col = (ps * AXPY_CHUNKS + c) * SC_LANES
                            ra = _as_bf16(rows[slot, row0 + k, pl.ds(col, SC_LANES)])
                            rb = _as_bf16(rows[slot, row0 + k + 1, pl.ds(col, SC_LANES)])
                            lo, hi = _unpack16(plsc.bitcast(ra * wa + rb * wb, jnp.uint32))
                            accs[2 * c] = accs[2 * c] + lo
                            accs[2 * c + 1] = accs[2 * c + 1] + hi
                    return tuple(accs)

                zero = tuple(jnp.zeros((SC_LANES,), f32) for _ in range(2 * AXPY_CHUNKS))
                accs = lax.fori_loop(0, w // SC_LANES, group, zero)
                for c in range(AXPY_CHUNKS):
                    col = (ps * AXPY_CHUNKS + c) * SC_LANES
                    y_v[buf, pl.ds(col, SC_LANES)] = y_v[buf, pl.ds(col, SC_LANES)] + accs[2 * c]
                    y_v[buf, pl.ds(ROW_WORDS + col, SC_LANES)] = (
                        y_v[buf, pl.ds(ROW_WORDS + col, SC_LANES)] + accs[2 * c + 1])

            @pl.when(part == win_per_tok - 1)
            def _():
                y_write(tok, buf).start()

        for tok in (tok_w - 2, tok_w - 1):
            y_write(tok, tok % 2).wait()

    return axpy(tab, idx, wts)


def _slot_weight_kernel(act_ref, gate_ref, o_ref):
    act = act_ref[...]
    wts = gate_ref[...] * (0.5 * act * (1.0 + lax.erf(act * (2.0 ** -0.5))))
    hi = _bf16_bits(wts)
    o_ref[...] = hi | (hi >> 16)


def _slot_weights(act, gates):
    n = gates.shape[0]
    tb = 1024
    blk = pl.BlockSpec((tb, PEER_SLOTS), lambda i: (i, 0))
    return pl.pallas_call(
        _slot_weight_kernel, grid=(n // tb,), in_specs=[blk, blk], out_specs=blk,
        out_shape=jax.ShapeDtypeStruct((n, PEER_SLOTS), jnp.uint32),
        compiler_params=_params("parallel"), name="peer_slot_weights",
    )(act, gates)


def _residual_kernel(h_ref, y_ref, g_ref, o_ref, *, final_norm):
    out = h_ref[...] + y_ref[...]
    o_ref[...] = _rms(out, g_ref[...]) if final_norm else out


def _residual(h, y, g_final, final_norm):
    n = h.shape[0]
    tb = 512
    blk = pl.BlockSpec((tb, D_MODEL), lambda i: (i, 0))
    g_final = g_final.reshape(1, -1)
    return pl.pallas_call(
        functools.partial(_residual_kernel, final_norm=final_norm),
        grid=(n // tb,), in_specs=[blk, blk, pl.BlockSpec((1, D_MODEL), lambda i: (0, 0))], out_specs=blk,
        out_shape=jax.ShapeDtypeStruct((n, D_MODEL), jnp.float32),
        compiler_params=_params("parallel"), name="peer_residual",
    )(h, y, g_final)


def _peer(xn, eid, gates, h, tab_u, tab_v, g_final, final_norm):
    n = xn.shape[0]
    tc = min(GATHER_TOKENS, n)
    chunks = [slice(c * tc, (c + 1) * tc) for c in range(n // tc)]
    act = jnp.concatenate([_sc_dots(tab_u, eid[tok].reshape(-1), xn[tok]) for tok in chunks])
    wts = _slot_weights(act.reshape(n, PEER_SLOTS), gates)
    y = jnp.concatenate([_sc_axpy(tab_v, eid[tok].reshape(-1), wts[tok].reshape(-1)) for tok in chunks])
    return _residual(h, y, g_final, final_norm)


_N_SUB = HG_CHUNK // HG_SUB
_OFF_PAIRS = [(i, j) for i in range(_N_SUB) for j in range(i)]


def _cum_matrix():
    t = jnp.arange(HG_CHUNK)[:, None]
    r = jnp.arange(HG_CHUNK)[None, :]
    sub = t // HG_SUB
    incl = r <= t
    before = r < sub * HG_SUB
    end = r < (sub + 1) * HG_SUB
    return jnp.concatenate([incl, before, end], axis=0).astype(jnp.float32)


def _hgrn_kernel(h_ref, g_ref, w_ref, lb_ref, on_ref, cum_ref, st_in_ref, o_ref, st_out_ref,
                 z_sc, st_sc, lb_sc):
    @pl.when(pl.program_id(0) == 0)
    def _():
        st_sc[...] = st_in_ref[...]

    tb = h_ref.shape[0]
    hn = _rms(h_ref[...], g_ref[...]).astype(jnp.bfloat16)
    z_sc[...] = jnp.dot(hn, w_ref[...], preferred_element_type=jnp.float32)
    lbr = lb_ref[...]
    mx = jnp.max(lbr, axis=0, keepdims=True)
    ex = jnp.exp(lbr - mx)
    prob = ex / jnp.sum(ex, axis=0, keepdims=True)
    lb_sc[...] = jnp.broadcast_to((prob[0:1, :] + prob[1:2, :]) - prob[0:1, :], lb_sc.shape)
    wf = HG_HEADS * HG_DK
    sub_row = lax.broadcasted_iota(jnp.int32, (HG_SUB, HG_DK), 0)
    lane64 = lax.broadcasted_iota(jnp.int32, (HG_SUB, HG_CHUNK), 1)

    mxu = jnp.bfloat16
    nt = (((1,), (1,)), ((), ()))

    def head(hd, carry):
        c0 = pl.multiple_of(hd * HG_DK, HG_DK)
        lb = lb_sc[0:1, pl.ds(c0, HG_DK)]
        on = on_ref[0:1, pl.ds(c0, HG_DK)]
        st = st_sc[hd]
        for ch in range(tb // HG_CHUNK):
            rows = slice(ch * HG_CHUNK, (ch + 1) * HG_CHUNK)
            qp = z_sc[rows, pl.ds(c0, HG_DK)]
            fp = z_sc[rows, pl.ds(pl.multiple_of(wf + c0, HG_DK), HG_DK)]
            v = z_sc[rows, pl.ds(pl.multiple_of(2 * wf + c0, HG_DK), HG_DK)]
            gp = z_sc[rows, pl.ds(pl.multiple_of(2 * wf + HG_HEADS * HG_DV + c0, HG_DK), HG_DK)]
            f = lb + (1.0 - lb) * jax.nn.sigmoid(fp)
            lf = jnp.log(f)
            k = 1.0 - f
            q = qp * jax.nn.sigmoid(qp)
            cums = jnp.dot(cum_ref[...], lf, precision=lax.Precision.HIGHEST,
                           preferred_element_type=jnp.float32)
            b = cums[:HG_CHUNK]
            b_start = cums[HG_CHUNK:2 * HG_CHUNK]
            b_end = cums[2 * HG_CHUNK:]
            q_hat = q * jnp.exp(b - b_start)
            k_hat = k * jnp.exp(b_end - b)
            o_inter = lax.dot_general((q_hat * jnp.exp(b_start)).astype(mxu), st.astype(mxu), nt,
                                      preferred_element_type=jnp.float32)
            stacked = []
            for (i, j) in _OFF_PAIRS:
                d_ij = jnp.exp(b_start[i * HG_SUB:i * HG_SUB + 1, :] - b_end[j * HG_SUB:j * HG_SUB + 1, :])
                stacked.append(q_hat[i * HG_SUB:(i + 1) * HG_SUB, :] * d_ij)
            stacked = jnp.concatenate(stacked, axis=0).astype(mxu)
            off = lax.dot_general(stacked, k_hat.astype(mxu), nt,
                                  preferred_element_type=jnp.float32)
            a_rows = []
            for i in range(_N_SUB):
                blk = jnp.zeros((HG_SUB, HG_CHUNK), jnp.float32)
                for p, (pi, pj) in enumerate(_OFF_PAIRS):
                    if pi == i:
                        in_j = (lane64 >= pj * HG_SUB) & (lane64 < (pj + 1) * HG_SUB)
                        blk = jnp.where(in_j, off[p * HG_SUB:(p + 1) * HG_SUB, :], blk)
                b_blk = b[i * HG_SUB:(i + 1) * HG_SUB, :]
                q_blk = q[i * HG_SUB:(i + 1) * HG_SUB, :]
                for s in range(HG_SUB):
                    n = i * HG_SUB + s
                    e = jnp.exp(jnp.where(sub_row >= s, b_blk - b[n:n + 1, :], MASKED_LOG_DECAY))
                    col = jnp.sum(q_blk * k[n:n + 1, :] * e, axis=1, keepdims=True)
                    blk = jnp.where(lane64 == n, col, blk)
                a_rows.append(blk)
            a = jnp.concatenate(a_rows, axis=0).astype(mxu)
            o = o_inter + jnp.dot(a, v.astype(mxu), preferred_element_type=jnp.float32)
            b_last = b[HG_CHUNK - 1:HG_CHUNK, :]
            k_til = (k_hat * jnp.exp(b_last - b_end)).astype(mxu)
            upd = lax.dot_general(v.astype(mxu), k_til, (((0,), (0,)), ((), ())),
                                  preferred_element_type=jnp.float32)
            st = st * jnp.exp(b_last) + upd
            o = o * lax.rsqrt(jnp.mean(o * o, axis=-1, keepdims=True) + NORM_EPS)
            o = o * on * (gp * jax.nn.sigmoid(gp))
            o_ref[rows, pl.ds(c0, HG_DK)] = o.astype(o_ref.dtype)
        st_sc[hd] = st
        return carry

    lax.fori_loop(0, HG_HEADS, head, 0)

    @pl.when(pl.program_id(0) == pl.num_programs(0) - 1)
    def _():
        st_out_ref[...] = st_sc[...]


def _hgrn(h, state, g, w, lb_raw, out_norm):
    n = h.shape[0]
    tb = min(HG_BLOCK, n)
    g = g.reshape(1, -1)
    out_norm = out_norm.reshape(1, -1)
    cum = _cum_matrix()
    full = lambda a: pl.BlockSpec(a.shape, lambda i: (0,) * a.ndim)
    f32 = jnp.float32
    return pl.pallas_call(
        _hgrn_kernel,
        grid=(n // tb,),
        in_specs=[pl.BlockSpec((tb, D_MODEL), lambda i: (i, 0)), full(g), full(w),
                  full(lb_raw), full(out_norm), full(cum), full(state)],
        out_specs=[pl.BlockSpec((tb, HG_HEADS * HG_DV), lambda i: (i, 0)), full(state)],
        out_shape=[jax.ShapeDtypeStruct((n, HG_HEADS * HG_DV), jnp.bfloat16),
                   jax.ShapeDtypeStruct(state.shape, f32)],
        scratch_shapes=[pltpu.VMEM((tb, w.shape[1]), f32),
                        pltpu.VMEM((HG_HEADS, HG_DV, HG_DK), f32),
                        pltpu.VMEM((8, HG_HEADS * HG_DK), f32)],
        compiler_params=_params("arbitrary"),
        name="hgrn2",
    )(h, g, w, lb_raw, out_norm, cum, state)


def kernel(x, positions, ln_mix, ln_ffn, ln_final, mla_w_in, mla_q_norm, mla_w_q_up, mla_kv_norm,
           mla_w_kv_up, mla_w_out, hg_w_in, hg_lb, hg_out_norm, hg_w_out, peer_w_q, peer_sub_keys,
           peer_u, peer_v):
    batch, seq, d = x.shape
    n = batch * seq
    bf16 = jnp.bfloat16
    h = x.reshape(n, d)

    def route_weights(i):
        keys = peer_sub_keys[i].reshape(2 * PEER_HEADS, PEER_NKEYS, PEER_HALF).astype(bf16)
        return peer_w_q[i].T.astype(bf16), keys

    tables = [(_pack_table(peer_u[i]), _pack_table(peer_v[i])) for i in range(2)]
    cos_t, sin_t = _rope_tables(positions)
    win, wq, wkv = _mla_weights(mla_w_in[0], mla_w_q_up[0], mla_w_kv_up[0])
    q, k, v = _mla_proj(h, ln_mix[0], cos_t, sin_t, win, mla_q_norm[0], wq, mla_kv_norm[0], wkv)
    wqt0, keys0 = route_weights(0)
    wqt1, keys1 = route_weights(1)
    mla_wo = mla_w_out[0].astype(bf16)
    hg_wi = hg_w_in[0].astype(bf16)
    hg_wo = hg_w_out[0].astype(bf16)

    chunk = min(PIPE_TOKENS, seq)
    blk = min(ATTN_BLOCK, chunk)
    n_chunks = seq // chunk
    states = [jnp.zeros((HG_HEADS, HG_DV, HG_DK), jnp.float32) for _ in range(batch)]
    outs = {}

    routed0, done0 = [], []

    def layer0(b, j):
        after = None
        if len(done0) >= 3:
            sign = lax.shift_right_logical(lax.bitcast_convert_type(done0[-3][0, 0], jnp.int32), jnp.int32(31))
            after = jnp.minimum(sign, 0)
        elif len(routed0) >= 2:
            after = routed0[-2][0, 0]
        o = _attention(q, k, v, seq, blk, b, j * chunk // blk, (j + 1) * chunk // blk, after)
        hc, xp, eid, gates = _route(o, h, b * seq + j * chunk, mla_wo, ln_ffn[0], wqt0, keys0)
        routed0.append(eid)
        done0.append(_peer(xp, eid, gates, hc, tables[0][0], tables[0][1], ln_final, False))
        return done0[-1]

    def layer1(b, j, hc):
        o, states[b] = _hgrn(hc, states[b], ln_mix[1], hg_wi, hg_lb, hg_out_norm[0])
        hc, xp, eid, gates = _route(o, hc, 0, hg_wo, ln_ffn[1], wqt1, keys1)
        outs[(b, j)] = _peer(xp, eid, gates, hc, tables[1][0], tables[1][1], ln_final, True)

    pending = None
    for j in range(n_chunks):
        for b in range(batch):
            hc = layer0(b, j)
            if pending is not None:
                layer1(*pending)
            pending = (b, j, hc)
    layer1(*pending)
    out = jnp.concatenate([outs[(b, j)] for b in range(batch) for j in range(n_chunks)], axis=0)
    return out.reshape(batch, seq, d)
```

```python
import functools

import jax
import jax.numpy as jnp
from jax import lax
from jax.experimental import pallas as pl
from jax.experimental.pallas import tpu as pltpu
from jax.experimental.pallas import tpu_sc as plsc

D_MODEL = 1024
NORM_EPS = 1e-6
MLA_HEADS = 8
MLA_Q_LORA = 384
MLA_KV_LORA = 256
MLA_NOPE = 128
MLA_ROPE = 64
MLA_V = 128
ROPE_THETA = 10000.0
HG_HEADS = 8
HG_DK = 128
HG_DV = 128
PEER_HEADS = 8
PEER_NKEYS = 128
PEER_HALF = 128
PEER_TOPK = 16
PEER_SLOTS = PEER_HEADS * PEER_TOPK

LANES = 128
SC_CORES = 2
SC_SUBCORES = 16
VMEM_LIMIT = 48 * 1024 * 1024

ROW_WORDS = D_MODEL // 2
TOKEN_BLOCK = 256
ATTN_BLOCK = 1024
HG_BLOCK = 256
HG_CHUNK = 64
HG_SUB = 16
PIPE_TOKENS = 2048
GATHER_WINDOW = 32
DOTS_SLOTS = 5
AXPY_SLOTS = 7
GATHER_TOKENS = 1024
SC_LANES = 16
DOT_ROWS = 8
AXPY_CHUNKS = 8
NEG_INF = float("-inf")
MASKED_LOG_DECAY = -1e30


def _rms(x, g):
    return x * lax.rsqrt(jnp.mean(x * x, axis=-1, keepdims=True) + NORM_EPS) * g


def _params(*sem):
    return pltpu.CompilerParams(dimension_semantics=sem, vmem_limit_bytes=VMEM_LIMIT)


def _bf16_bits(x):
    return pltpu.bitcast(x.astype(jnp.bfloat16).astype(jnp.float32), jnp.uint32)


def _pack_pairs(t):
    return (_bf16_bits(t[:, :ROW_WORDS]) >> 16) | _bf16_bits(t[:, ROW_WORDS:])


def _pack_kernel(t_ref, o_ref):
    o_ref[...] = _pack_pairs(t_ref[...])


def _pack_table(tab):
    e, d = tab.shape
    rows = 512
    return pl.pallas_call(
        _pack_kernel,
        grid=(e // rows,),
        in_specs=[pl.BlockSpec((rows, d), lambda i: (i, 0))],
        out_specs=pl.BlockSpec((rows, d // 2), lambda i: (i, 0)),
        out_shape=jax.ShapeDtypeStruct((e, d // 2), jnp.uint32),
        compiler_params=_params("parallel"),
        name="pack_table",
    )(tab)


def _rope_kernel(pos_ref, inv_ref, sign_ref, c_ref, s_ref):
    ang = pos_ref[...].astype(jnp.float32) * inv_ref[...]
    c_ref[...] = jnp.cos(ang)
    s_ref[...] = jnp.sin(ang) * sign_ref[...]


def _rope_tables(positions):
    n = positions.size
    lane = jnp.arange(LANES)
    inv = ROPE_THETA ** (-(2 * (lane % (MLA_ROPE // 2))).astype(jnp.float32) / MLA_ROPE)
    sign = jnp.where((lane % MLA_ROPE) < MLA_ROPE // 2, -1.0, 1.0).astype(jnp.float32)
    tb = 1024
    out = jax.ShapeDtypeStruct((n, LANES), jnp.float32)
    return pl.pallas_call(
        _rope_kernel,
        grid=(n // tb,),
        in_specs=[pl.BlockSpec((tb, 1), lambda i: (i, 0)),
                  pl.BlockSpec((1, LANES), lambda i: (0, 0)),
                  pl.BlockSpec((1, LANES), lambda i: (0, 0))],
        out_specs=[pl.BlockSpec((tb, LANES), lambda i: (i, 0))] * 2,
        out_shape=[out, out],
        compiler_params=_params("parallel"),
        name="rope_tables",
    )(positions.reshape(n, 1), inv.reshape(1, LANES), sign.reshape(1, LANES))


def _mla_proj_kernel(x_ref, g_ref, c_ref, s_ref, win_ref, qn_ref, wq_ref, kvn_ref, wkv_ref,
                     q_out, k_out, v_out):
    hn = _rms(x_ref[...], g_ref[...]).astype(jnp.bfloat16)
    z = jnp.dot(hn, win_ref[...], preferred_element_type=jnp.float32)
    c = c_ref[...]
    s = s_ref[...]
    o_kv = MLA_Q_LORA
    o_kr = MLA_Q_LORA + MLA_KV_LORA
    k_rope = (z[:, o_kr:o_kr + LANES] * c + z[:, o_kr + LANES:o_kr + 2 * LANES] * s).astype(jnp.bfloat16)
    cq = _rms(z[:, :MLA_Q_LORA], qn_ref[...]).astype(jnp.bfloat16)
    q = jnp.dot(cq, wq_ref[...], preferred_element_type=jnp.float32)
    ckv = _rms(z[:, o_kv:o_kr], kvn_ref[...]).astype(jnp.bfloat16)
    kv = jnp.dot(ckv, wkv_ref[...], preferred_element_type=jnp.float32)
    scale = (MLA_NOPE + MLA_ROPE) ** -0.5
    hw = MLA_HEADS * LANES
    for h in range(MLA_HEADS):
        sl = slice(h * LANES, (h + 1) * LANES)
        q_out[h, :, :LANES] = (q[:, sl] * scale).astype(jnp.bfloat16)
        qr = q[:, hw + h * LANES:hw + (h + 1) * LANES] * c + q[:, 2 * hw + h * LANES:2 * hw + (h + 1) * LANES] * s
        q_out[h, :, LANES:] = (qr * scale).astype(jnp.bfloat16)
        k_out[h, :, :LANES] = kv[:, sl].astype(jnp.bfloat16)
        k_out[h, :, LANES:] = k_rope
        v_out[h] = kv[:, hw + h * LANES:hw + (h + 1) * LANES].astype(jnp.bfloat16)


def _swap_halves(w):
    half = w.shape[-1] // 2
    return jnp.concatenate([w[..., half:], w[..., :half]], axis=-1)


def _pad_lanes(w):
    return jnp.pad(w, [(0, 0)] * (w.ndim - 1) + [(0, LANES - w.shape[-1])])


def _mla_weights(w_in, w_q_up, w_kv_up):
    o_kr = MLA_Q_LORA + MLA_KV_LORA
    w_kr = w_in[:, o_kr:]
    win = jnp.concatenate([w_in[:, :o_kr], _pad_lanes(w_kr), _pad_lanes(_swap_halves(w_kr))], axis=1)
    wq = w_q_up.reshape(MLA_Q_LORA, MLA_HEADS, MLA_NOPE + MLA_ROPE)
    wq_n = wq[:, :, :MLA_NOPE].reshape(MLA_Q_LORA, -1)
    wq_r = _pad_lanes(wq[:, :, MLA_NOPE:]).reshape(MLA_Q_LORA, -1)
    wq_rs = _pad_lanes(_swap_halves(wq[:, :, MLA_NOPE:])).reshape(MLA_Q_LORA, -1)
    wqp = jnp.concatenate([wq_n, wq_r, wq_rs], axis=1)
    wkv = w_kv_up.reshape(MLA_KV_LORA, MLA_HEADS, MLA_NOPE + MLA_V)
    wkvp = jnp.concatenate([wkv[:, :, :MLA_NOPE].reshape(MLA_KV_LORA, -1),
                            wkv[:, :, MLA_NOPE:].reshape(MLA_KV_LORA, -1)], axis=1)
    return win.astype(jnp.bfloat16), wqp.astype(jnp.bfloat16), wkvp.astype(jnp.bfloat16)


def _mla_proj(h, g, cos_t, sin_t, win, q_norm, wq, kv_norm, wkv):
    n = h.shape[0]
    tb = TOKEN_BLOCK
    full = lambda a: pl.BlockSpec(a.shape, lambda i: (0,) * a.ndim)
    g = g.reshape(1, -1)
    q_norm = q_norm.reshape(1, -1)
    kv_norm = kv_norm.reshape(1, -1)
    qk_t = jax.ShapeDtypeStruct((MLA_HEADS, n, 2 * LANES), jnp.bfloat16)
    v_t = jax.ShapeDtypeStruct((MLA_HEADS, n, MLA_V), jnp.bfloat16)
    return pl.pallas_call(
        _mla_proj_kernel,
        grid=(n // tb,),
        in_specs=[pl.BlockSpec((tb, D_MODEL), lambda i: (i, 0)), full(g),
                  pl.BlockSpec((tb, LANES), lambda i: (i, 0)), pl.BlockSpec((tb, LANES), lambda i: (i, 0)),
                  full(win), full(q_norm), full(wq), full(kv_norm), full(wkv)],
        out_specs=[pl.BlockSpec((MLA_HEADS, tb, 2 * LANES), lambda i: (0, i, 0)),
                   pl.BlockSpec((MLA_HEADS, tb, 2 * LANES), lambda i: (0, i, 0)),
                   pl.BlockSpec((MLA_HEADS, tb, MLA_V), lambda i: (0, i, 0))],
        out_shape=[qk_t, qk_t, v_t],
        compiler_params=_params("parallel"),
        name="mla_proj",
    )(h, g, cos_t, sin_t, win, q_norm, wq, kv_norm, wkv)


def _attn_kernel(qi_ref, ki_ref, q_ref, k_ref, v_ref, o_ref, m_sc, l_sc, acc_sc):
    step = pl.program_id(1)
    qi = qi_ref[step]
    ki = ki_ref[step]

    @pl.when(ki == 0)
    def _():
        m_sc[...] = jnp.full(m_sc.shape, NEG_INF, jnp.float32)
        l_sc[...] = jnp.zeros(l_sc.shape, jnp.float32)
        acc_sc[...] = jnp.zeros(acc_sc.shape, jnp.float32)

    def update(masked):
        sc = lax.dot_general(q_ref[...], k_ref[...], (((1,), (1,)), ((), ())),
                             preferred_element_type=jnp.float32)
        if masked:
            row = lax.broadcasted_iota(jnp.int32, sc.shape, 0)
            col = lax.broadcasted_iota(jnp.int32, sc.shape, 1)
            sc = jnp.where(col <= row, sc, NEG_INF)
        m_prev = m_sc[...]
        m_next = jnp.maximum(m_prev, jnp.max(sc, axis=1, keepdims=True))
        p = jnp.exp(sc - m_next[:, :1])
        alpha = jnp.exp(m_prev - m_next)
        l_sc[...] = alpha * l_sc[...] + jnp.sum(p, axis=1, keepdims=True)
        acc_sc[...] = alpha * acc_sc[...] + jnp.dot(p.astype(jnp.bfloat16), v_ref[...],
                                                    preferred_element_type=jnp.float32)
        m_sc[...] = m_next

    @pl.when(ki < qi)
    def _():
        update(False)

    @pl.when(ki == qi)
    def _():
        update(True)
        o_ref[...] = (acc_sc[...] / l_sc[...]).astype(o_ref.dtype)


def _attention(q, k, v, seq, blk, b, q_lo, q_hi, after=None):
    nb = seq // blk
    qi = jnp.array([i for i in range(q_lo, q_hi) for _ in range(i + 1)], jnp.int32)
    ki = jnp.array([j for i in range(q_lo, q_hi) for j in range(i + 1)], jnp.int32)
    if after is not None:
        qi = qi + jnp.minimum(after, 0)
    grid_spec = pltpu.PrefetchScalarGridSpec(
        num_scalar_prefetch=2,
        grid=(MLA_HEADS, qi.shape[0]),
        in_specs=[pl.BlockSpec((None, blk, 2 * LANES), lambda h, s, qi, ki: (h, b * nb + qi[s], 0)),
                  pl.BlockSpec((None, blk, 2 * LANES), lambda h, s, qi, ki: (h, b * nb + ki[s], 0)),
                  pl.BlockSpec((None, blk, MLA_V), lambda h, s, qi, ki: (h, b * nb + ki[s], 0))],
        out_specs=pl.BlockSpec((blk, MLA_V), lambda h, s, qi, ki: (qi[s] - q_lo, h)),
        scratch_shapes=[pltpu.VMEM((blk, LANES), jnp.float32), pltpu.VMEM((blk, LANES), jnp.float32),
                        pltpu.VMEM((blk, MLA_V), jnp.float32)],
    )
    return pl.pallas_call(
        _attn_kernel,
        grid_spec=grid_spec,
        out_shape=jax.ShapeDtypeStruct(((q_hi - q_lo) * blk, MLA_HEADS * MLA_V), jnp.bfloat16),
        compiler_params=_params("parallel", "arbitrary"),
        name="mla_attention",
    )(qi, ki, q, k, v)


def _col_reduce(x, op, reduce_fn):
    slabs = [x[i:i + 8] for i in range(0, x.shape[0], 8)]
    while len(slabs) > 1:
        nxt = [op(slabs[i], slabs[i + 1]) for i in range(0, len(slabs) - 1, 2)]
        if len(slabs) % 2:
            nxt.append(slabs[-1])
        slabs = nxt
    return reduce_fn(slabs[0], axis=0, keepdims=True)


def _top_rows(vals, ids, count, out_rows):
    t = vals.shape[1]
    big = jnp.int32(2 ** 30)
    orow = lax.broadcasted_iota(jnp.int32, (out_rows, t), 0)

    def body(r, carry):
        cur, ov, oi = carry
        m = _col_reduce(cur, jnp.maximum, jnp.max)
        pick = _col_reduce(jnp.where(cur == m, ids, big), jnp.minimum, jnp.min)
        cur = jnp.where(ids == pick, NEG_INF, cur)
        ov = jnp.where(orow == r, m, ov)
        oi = jnp.where(orow == r, pick, oi)
        return cur, ov, oi

    init = (vals, jnp.zeros((out_rows, t), jnp.float32), jnp.zeros((out_rows, t), jnp.int32))
    _, ov, oi = lax.fori_loop(0, count, body, init)
    return ov, oi


_ROW_SLABS = [(0, 0, 16), (1, 0, 8)] + [(a, 0, 8) for a in range(2, 8)]
_COL_SLAB = (8, 16, 0)
_PAIR_ROWS = sum(hi - lo for _, lo, hi in _ROW_SLABS) + (_COL_SLAB[1] - _COL_SLAB[0])


def _slot_weight_words(act, gate):
    hi = _bf16_bits(gate * (0.5 * act * (1.0 + lax.erf(act * (2.0 ** -0.5)))))
    return hi | (hi >> 16)


def _route_kernel(*refs, with_prev):
    if with_prev:
        (o_ref, h_ref, wo_ref, g_ref, wqt_ref, keys_ref, pos_ref, act_p_ref, gate_p_ref,
         hn_out, xn_out, eid_out, gate_out, wts_p_out, qt_sc, v_sc, i_sc) = refs
        wts_p_out[...] = _slot_weight_words(act_p_ref[...], gate_p_ref[...])
    else:
        (o_ref, h_ref, wo_ref, g_ref, wqt_ref, keys_ref, pos_ref,
         hn_out, xn_out, eid_out, gate_out, qt_sc, v_sc, i_sc) = refs
    tb = h_ref.shape[0]
    hnew = h_ref[...] + jnp.dot(o_ref[...], wo_ref[...], preferred_element_type=jnp.float32)
    hn_out[...] = hnew
    xn = _rms(hnew, g_ref[...])
    xn_out[...] = _pack_pairs(xn)
    qt_sc[...] = lax.dot_general(wqt_ref[...], xn.astype(jnp.bfloat16), (((1,), (1,)), ((), ())),
                                 preferred_element_type=jnp.float32).astype(jnp.bfloat16)
    key_ids = lax.broadcasted_iota(jnp.int32, (PEER_NKEYS, tb), 0)

    def group(g, carry):
        row0 = pl.multiple_of(g * PEER_HALF, PEER_HALF)
        st = jnp.dot(keys_ref[g], qt_sc[pl.ds(row0, PEER_HALF), :], preferred_element_type=jnp.float32)
        tv, ti = _top_rows(st, key_ids, PEER_TOPK, PEER_TOPK)
        out0 = pl.multiple_of(g * PEER_TOPK, PEER_TOPK)
        v_sc[pl.ds(out0, PEER_TOPK), :] = tv
        i_sc[pl.ds(out0, PEER_TOPK), :] = ti
        return carry

    lax.fori_loop(0, 2 * PEER_HEADS, group, 0)

    pos = pos_ref[...]

    def head(hd, carry):
        base = pl.multiple_of(hd * 2 * PEER_TOPK, 2 * PEER_TOPK)
        v1 = v_sc[pl.ds(base, PEER_TOPK), :]
        i1 = i_sc[pl.ds(base, PEER_TOPK), :]
        v2 = v_sc[pl.ds(base + PEER_TOPK, PEER_TOPK), :]
        i2 = i_sc[pl.ds(base + PEER_TOPK, PEER_TOPK), :]
        cv, ce = [], []
        for a, lo, hi in _ROW_SLABS:
            cv.append(v1[a:a + 1, :] + v2[lo:hi, :])
            ce.append(i1[a:a + 1, :] * PEER_NKEYS + i2[lo:hi, :])
        a_lo, a_hi, b = _COL_SLAB
        cv.append(v1[a_lo:a_hi, :] + v2[b:b + 1, :])
        ce.append(i1[a_lo:a_hi, :] * PEER_NKEYS + i2[b:b + 1, :])
        cv = jnp.concatenate(cv, axis=0)
        ce = jnp.concatenate(ce, axis=0)
        tv, tp = _top_rows(cv, jnp.broadcast_to(pos, cv.shape), PEER_TOPK, PEER_TOPK)
        te = jnp.zeros((PEER_TOPK, tb), jnp.int32)
        orow = lax.broadcasted_iota(jnp.int32, (PEER_TOPK, tb), 0)
        for r in range(PEER_TOPK):
            e_r = jnp.sum(jnp.where(pos == tp[r:r + 1, :], ce, 0), axis=0, keepdims=True)
            te = jnp.where(orow == r, e_r, te)
        ex = jnp.exp(tv - tv[0:1, :])
        gates = ex / jnp.sum(ex, axis=0, keepdims=True)
        out0 = pl.multiple_of(hd * PEER_TOPK, PEER_TOPK)
        v_sc[pl.ds(out0, PEER_TOPK), :] = gates
        i_sc[pl.ds(out0, PEER_TOPK), :] = te
        return carry

    lax.fori_loop(0, PEER_HEADS, head, 0)

    eid_out[...] = jnp.transpose(i_sc[:PEER_SLOTS, :].astype(jnp.float32)).astype(jnp.int32)
    gate_out[...] = jnp.transpose(v_sc[:PEER_SLOTS, :])


def _pair_positions():
    pos = [a * PEER_TOPK + b for a, lo, hi in _ROW_SLABS for b in range(lo, hi)]
    a_lo, a_hi, b = _COL_SLAB
    pos += [a * PEER_TOPK + b for a in range(a_lo, a_hi)]
    return jnp.array(pos, jnp.int32).reshape(_PAIR_ROWS, 1)


def _route(o, h, h_row0, w_out, g, wqt, keys, prev=None):
    n = o.shape[0]
    tb = TOKEN_BLOCK
    full = lambda a: pl.BlockSpec(a.shape, lambda i: (0,) * a.ndim)
    g = g.reshape(1, -1)
    pos = _pair_positions()
    row = pl.BlockSpec((tb, D_MODEL), lambda i: (i, 0))
    slot = pl.BlockSpec((tb, PEER_SLOTS), lambda i: (i, 0))
    h_row = pl.BlockSpec((tb, D_MODEL), lambda i: (i + h_row0 // tb, 0))
    f32 = jnp.float32
    args = [o, h, w_out, g, wqt, keys, pos]
    in_specs = [row, h_row, full(w_out), full(g), full(wqt), full(keys), full(pos)]
    out_specs = [row, pl.BlockSpec((tb, ROW_WORDS), lambda i: (i, 0)), slot, slot]
    out_shape = [jax.ShapeDtypeStruct((n, D_MODEL), f32), jax.ShapeDtypeStruct((n, ROW_WORDS), jnp.uint32),
                 jax.ShapeDtypeStruct((n, PEER_SLOTS), jnp.int32), jax.ShapeDtypeStruct((n, PEER_SLOTS), f32)]
    if prev is not None:
        args += list(prev)
        in_specs += [slot, slot]
        out_specs.append(slot)
        out_shape.append(jax.ShapeDtypeStruct((n, PEER_SLOTS), jnp.uint32))
    return pl.pallas_call(
        functools.partial(_route_kernel, with_prev=prev is not None),
        grid=(n // tb,),
        in_specs=in_specs,
        out_specs=out_specs,
        out_shape=out_shape,
        scratch_shapes=[pltpu.VMEM((2 * PEER_HEADS * PEER_HALF, tb), jnp.bfloat16),
                        pltpu.VMEM((2 * PEER_HEADS * PEER_TOPK, tb), f32),
                        pltpu.VMEM((2 * PEER_HEADS * PEER_TOPK, tb), jnp.int32)],
        compiler_params=_params("parallel"),
        name="peer_route",
    )(*args)


def _sc_mesh():
    return plsc.VectorSubcoreMesh(core_axis_name="c", subcore_axis_name="s")


def _sc_params():
    return pltpu.CompilerParams(needs_layout_passes=False)


def _worker_id():
    return lax.axis_index("s") * SC_CORES + lax.axis_index("c")


def _gather_rows(tab_hbm, idx_ref, dst_ref, sem):
    return pltpu.make_async_copy(tab_hbm.at[idx_ref], dst_ref, sem)


def _unpack16(word):
    lo = lax.bitcast_convert_type(word << 16, jnp.float32)
    hi = lax.bitcast_convert_type(word & jnp.uint32(0xFFFF0000), jnp.float32)
    return lo, hi


def _as_bf16(word):
    return plsc.bitcast(word, jnp.bfloat16)


def _sc_dots(tab, idx, x):
    t = x.shape[0]
    workers = SC_CORES * SC_SUBCORES
    tok_w = t // workers
    per_worker = tok_w * PEER_SLOTS
    w = GATHER_WINDOW
    slots = DOTS_SLOTS
    n_win = per_worker // w
    win_per_tok = PEER_SLOTS // w
    n_chunks = ROW_WORDS // SC_LANES
    f32 = jnp.float32

    @functools.partial(
        pl.kernel, out_type=jax.ShapeDtypeStruct((t * PEER_SLOTS,), f32), mesh=_sc_mesh(),
        scratch_types=[pltpu.VMEM((per_worker,), jnp.int32),
                       pltpu.VMEM((tok_w, ROW_WORDS), jnp.uint32),
                       pltpu.VMEM((slots, w, ROW_WORDS), jnp.uint32),
                       pltpu.VMEM((per_worker,), f32),
                       pltpu.VMEM((w * SC_LANES,), f32),
                       pltpu.SemaphoreType.DMA((slots,))],
        compiler_params=_sc_params(), name="peer_dots")
    def dots(tab_hbm, i_hbm, x_hbm, act_hbm, idx_v, x_v, rows, act_v, part_v, sem):
        wid = _worker_id()
        base = pl.multiple_of(wid * per_worker, per_worker)
        tok0 = pl.multiple_of(wid * tok_w, tok_w)
        pltpu.sync_copy(i_hbm.at[pl.ds(base, per_worker)], idx_v)
        pltpu.sync_copy(x_hbm.at[pl.ds(tok0, tok_w)], x_v)

        def gather(win, slot):
            ix = idx_v.at[pl.ds(pl.multiple_of(win * w, w), w)]
            return _gather_rows(tab_hbm, ix, rows.at[slot], sem.at[slot])

        for s in range(slots - 1):
            gather(s, s).start()
        lane = lax.iota(jnp.int32, SC_LANES)

        @pl.loop(0, n_win)
        def _(win):
            slot = lax.rem(win, slots)
            nxt = win + slots - 1

            @pl.when(nxt < n_win)
            def _():
                gather(nxt, lax.rem(nxt, slots)).start()

            gather(win, slot).wait()
            tok = win // win_per_tok

            @plsc.parallel_loop(0, w // DOT_ROWS)
            def _(g):
                r0 = g * DOT_ROWS
                acc_lo = [jnp.zeros((SC_LANES,), f32) for _ in range(DOT_ROWS)]
                acc_hi = [jnp.zeros((SC_LANES,), f32) for _ in range(DOT_ROWS)]
                for c in range(0, n_chunks, 2):
                    xa = _as_bf16(x_v[tok, pl.ds(c * SC_LANES, SC_LANES)])
                    xb = _as_bf16(x_v[tok, pl.ds((c + 1) * SC_LANES, SC_LANES)])
                    for r in range(DOT_ROWS):
                        ra = _as_bf16(rows[slot, r0 + r, pl.ds(c * SC_LANES, SC_LANES)])
                        rb = _as_bf16(rows[slot, r0 + r, pl.ds((c + 1) * SC_LANES, SC_LANES)])
                        lo, hi = _unpack16(plsc.bitcast(ra * xa + rb * xb, jnp.uint32))
                        acc_lo[r] = acc_lo[r] + lo
                        acc_hi[r] = acc_hi[r] + hi
                for r in range(DOT_ROWS):
                    part_v[pl.ds(pl.multiple_of((r0 + r) * SC_LANES, SC_LANES), SC_LANES)] = acc_lo[r] + acc_hi[r]

            for blk in range(w // SC_LANES):
                res = jnp.zeros((SC_LANES,), f32)
                for l in range(SC_LANES):
                    res = res + plsc.load_gather(part_v, [lane * SC_LANES + (blk * SC_LANES * SC_LANES + l)])
                act_v[pl.ds(pl.multiple_of(win * w + blk * SC_LANES, SC_LANES), SC_LANES)] = res

        pltpu.sync_copy(act_v, act_hbm.at[pl.ds(base, per_worker)])

    return dots(tab, idx, x)


def _sc_axpy(tab, idx, wts):
    p = idx.shape[0]
    t = p // PEER_SLOTS
    workers = SC_CORES * SC_SUBCORES
    tok_w = t // workers
    per_worker = tok_w * PEER_SLOTS
    w = GATHER_WINDOW
    slots = AXPY_SLOTS
    n_win = per_worker // w
    win_per_tok = PEER_SLOTS // w
    passes = ROW_WORDS // (SC_LANES * AXPY_CHUNKS)
    f32 = jnp.float32

    @functools.partial(
        pl.kernel, out_type=jax.ShapeDtypeStruct((t, D_MODEL), f32), mesh=_sc_mesh(),
        scratch_types=[pltpu.VMEM((per_worker,), jnp.int32),
                       pltpu.VMEM((per_worker,), jnp.uint32),
                       pltpu.VMEM((slots, w, ROW_WORDS), jnp.uint32),
                       pltpu.VMEM((2, D_MODEL), f32),
                       pltpu.SemaphoreType.DMA((slots,)),
                       pltpu.SemaphoreType.DMA((2,))],
        compiler_params=_sc_params(), name="peer_axpy")
    def axpy(tab_hbm, i_hbm, w_hbm, y_hbm, idx_v, w_v, rows, y_v, sem, sem_y):
        wid = _worker_id()
        base = pl.multiple_of(wid * per_worker, per_worker)
        tok0 = wid * tok_w
        pltpu.sync_copy(i_hbm.at[pl.ds(base, per_worker)], idx_v)
        pltpu.sync_copy(w_hbm.at[pl.ds(base, per_worker)], w_v)

        def gather(win, slot):
            ix = idx_v.at[pl.ds(pl.multiple_of(win * w, w), w)]
            return _gather_rows(tab_hbm, ix, rows.at[slot], sem.at[slot])

        def y_write(tok, buf):
            return pltpu.make_async_copy(y_v.at[buf], y_hbm.at[tok0 + tok], sem_y.at[buf])

        for s in range(slots - 1):
            gather(s, s).start()

        @pl.loop(0, n_win)
        def _(win):
            slot = lax.rem(win, slots)
            nxt = win + slots - 1

            @pl.when(nxt < n_win)
            def _():
                gather(nxt, lax.rem(nxt, slots)).start()

            gather(win, slot).wait()
            tok = win // win_per_tok
            part = lax.rem(win, win_per_tok)
            buf = lax.rem(tok, 2)

            @pl.when(part == 0)
            def _():
                @pl.when(tok >= 2)
                def _():
                    y_write(tok - 2, buf).wait()

                for c in range(D_MODEL // SC_LANES):
                    y_v[buf, pl.ds(c * SC_LANES, SC_LANES)] = jnp.zeros((SC_LANES,), f32)

            for ps in range(passes):

                def group(g, accs):
                    accs = list(accs)
                    row0 = pl.multiple_of(g * SC_LANES, SC_LANES)
                    w_grp = w_v[pl.ds(pl.multiple_of(win * w + row0, SC_LANES), SC_LANES)]
                    for k in range(0, SC_LANES, 2):
                        wa = _as_bf16(jnp.take_along_axis(w_grp, jnp.full((SC_LANES,), k, jnp.int32), axis=0))
                        wb = _as_bf16(jnp.take_along_axis(w_grp, jnp.full((SC_LANES,), k + 1, jnp.int32), axis=0))
                        for c in range(AXPY_CHUNKS):
                            col = (ps * AXPY_CHUNKS + c) * SC_LANES
                            ra = _as_bf16(rows[slot, row0 + k, pl.ds(col, SC_LANES)])
                            rb = _as_bf16(rows[slot, row0 + k + 1, pl.ds(col, SC_LANES)])
                            lo, hi = _unpack16(plsc.bitcast(ra * wa + rb * wb, jnp.uint32))
                            accs[2 * c] = accs[2 * c] + lo
                            accs[2 * c + 1] = accs[2 * c + 1] + hi
                    return tuple(accs)

                zero = tuple(jnp.zeros((SC_LANES,), f32) for _ in range(2 * AXPY_CHUNKS))
                accs = lax.fori_loop(0, w // SC_LANES, group, zero)
                for c in range(AXPY_CHUNKS):
                    col = (ps * AXPY_CHUNKS + c) * SC_LANES
                    y_v[buf, pl.ds(col, SC_LANES)] = y_v[buf, pl.ds(col, SC_LANES)] + accs[2 * c]
                    y_v[buf, pl.ds(ROW_WORDS + col, SC_LANES)] = (
                        y_v[buf, pl.ds(ROW_WORDS + col, SC_LANES)] + accs[2 * c + 1])

            @pl.when(part == win_per_tok - 1)
            def _():
                y_write(tok, buf).start()

        for tok in (tok_w - 2, tok_w - 1):
            y_write(tok, tok % 2).wait()

    return axpy(tab, idx, wts)


def _slot_weight_kernel(act_ref, gate_ref, o_ref):
    o_ref[...] = _slot_weight_words(act_ref[...], gate_ref[...])


def _slot_weights(act, gates):
    n = gates.shape[0]
    tb = 1024
    blk = pl.BlockSpec((tb, PEER_SLOTS), lambda i: (i, 0))
    return pl.pallas_call(
        _slot_weight_kernel, grid=(n // tb,), in_specs=[blk, blk], out_specs=blk,
        out_shape=jax.ShapeDtypeStruct((n, PEER_SLOTS), jnp.uint32),
        compiler_params=_params("parallel"), name="peer_slot_weights",
    )(act, gates)


def _residual_kernel(h_ref, y_ref, g_ref, o_ref, *, final_norm):
    out = h_ref[...] + y_ref[...]
    o_ref[...] = _rms(out, g_ref[...]) if final_norm else out


def _residual(h, y, g_final, final_norm):
    n = h.shape[0]
    tb = 512
    blk = pl.BlockSpec((tb, D_MODEL), lambda i: (i, 0))
    g_final = g_final.reshape(1, -1)
    return pl.pallas_call(
        functools.partial(_residual_kernel, final_norm=final_norm),
        grid=(n // tb,), in_specs=[blk, blk, pl.BlockSpec((1, D_MODEL), lambda i: (0, 0))], out_specs=blk,
        out_shape=jax.ShapeDtypeStruct((n, D_MODEL), jnp.float32),
        compiler_params=_params("parallel"), name="peer_residual",
    )(h, y, g_final)


def _launch_slices(n):
    tc = min(GATHER_TOKENS, n)
    return [slice(c * tc, (c + 1) * tc) for c in range(n // tc)]


def _peer_dots(xp, eid, tab_u):
    n = xp.shape[0]
    act = jnp.concatenate([_sc_dots(tab_u, eid[tok].reshape(-1), xp[tok]) for tok in _launch_slices(n)])
    return act.reshape(n, PEER_SLOTS)


def _peer_finish(wts, eid, h, tab_v, g_final, final_norm):
    n = h.shape[0]
    y = jnp.concatenate([_sc_axpy(tab_v, eid[tok].reshape(-1), wts[tok].reshape(-1))
                         for tok in _launch_slices(n)])
    return _residual(h, y, g_final, final_norm)


_N_SUB = HG_CHUNK // HG_SUB
_OFF_PAIRS = [(i, j) for i in range(_N_SUB) for j in range(i)]


def _cum_matrix():
    t = jnp.arange(HG_CHUNK)[:, None]
    r = jnp.arange(HG_CHUNK)[None, :]
    sub = t // HG_SUB
    incl = r <= t
    before = r < sub * HG_SUB
    end = r < (sub + 1) * HG_SUB
    return jnp.concatenate([incl, before, end], axis=0).astype(jnp.float32)


def _hgrn_kernel(h_ref, g_ref, w_ref, lb_ref, on_ref, cum_ref, st_in_ref, o_ref, st_out_ref,
                 z_sc, st_sc, lb_sc):
    @pl.when(pl.program_id(0) == 0)
    def _():
        st_sc[...] = st_in_ref[...]

    tb = h_ref.shape[0]
    hn = _rms(h_ref[...], g_ref[...]).astype(jnp.bfloat16)
    z_sc[...] = jnp.dot(hn, w_ref[...], preferred_element_type=jnp.float32)
    lbr = lb_ref[...]
    mx = jnp.max(lbr, axis=0, keepdims=True)
    ex = jnp.exp(lbr - mx)
    prob = ex / jnp.sum(ex, axis=0, keepdims=True)
    lb_sc[...] = jnp.broadcast_to((prob[0:1, :] + prob[1:2, :]) - prob[0:1, :], lb_sc.shape)
    wf = HG_HEADS * HG_DK
    sub_row = lax.broadcasted_iota(jnp.int32, (HG_SUB, HG_DK), 0)
    lane64 = lax.broadcasted_iota(jnp.int32, (HG_SUB, HG_CHUNK), 1)

    mxu = jnp.bfloat16
    nt = (((1,), (1,)), ((), ()))

    def head(hd, carry):
        c0 = pl.multiple_of(hd * HG_DK, HG_DK)
        lb = lb_sc[0:1, pl.ds(c0, HG_DK)]
        on = on_ref[0:1, pl.ds(c0, HG_DK)]
        st = st_sc[hd]
        for ch in range(tb // HG_CHUNK):
            rows = slice(ch * HG_CHUNK, (ch + 1) * HG_CHUNK)
            qp = z_sc[rows, pl.ds(c0, HG_DK)]
            fp = z_sc[rows, pl.ds(pl.multiple_of(wf + c0, HG_DK), HG_DK)]
            v = z_sc[rows, pl.ds(pl.multiple_of(2 * wf + c0, HG_DK), HG_DK)]
            gp = z_sc[rows, pl.ds(pl.multiple_of(2 * wf + HG_HEADS * HG_DV + c0, HG_DK), HG_DK)]
            f = lb + (1.0 - lb) * jax.nn.sigmoid(fp)
            lf = jnp.log(f)
            k = 1.0 - f
            q = qp * jax.nn.sigmoid(qp)
            cums = jnp.dot(cum_ref[...], lf, precision=lax.Precision.HIGHEST,
                           preferred_element_type=jnp.float32)
            b = cums[:HG_CHUNK]
            b_start = cums[HG_CHUNK:2 * HG_CHUNK]
            b_end = cums[2 * HG_CHUNK:]
            q_hat = q * jnp.exp(b - b_start)
            k_hat = k * jnp.exp(b_end - b)
            o_inter = lax.dot_general((q_hat * jnp.exp(b_start)).astype(mxu), st.astype(mxu), nt,
                                      preferred_element_type=jnp.float32)
            stacked = []
            for (i, j) in _OFF_PAIRS:
                d_ij = jnp.exp(b_start[i * HG_SUB:i * HG_SUB + 1, :] - b_end[j * HG_SUB:j * HG_SUB + 1, :])
                stacked.append(q_hat[i * HG_SUB:(i + 1) * HG_SUB, :] * d_ij)
            stacked = jnp.concatenate(stacked, axis=0).astype(mxu)
            off = lax.dot_general(stacked, k_hat.astype(mxu), nt,
                                  preferred_element_type=jnp.float32)
            a_rows = []
            for i in range(_N_SUB):
                blk = jnp.zeros((HG_SUB, HG_CHUNK), jnp.float32)
                for p, (pi, pj) in enumerate(_OFF_PAIRS):
                    if pi == i:
                        in_j = (lane64 >= pj * HG_SUB) & (lane64 < (pj + 1) * HG_SUB)
                        blk = jnp.where(in_j, off[p * HG_SUB:(p + 1) * HG_SUB, :], blk)
                b_blk = b[i * HG_SUB:(i + 1) * HG_SUB, :]
                q_blk = q[i * HG_SUB:(i + 1) * HG_SUB, :]
                for s in range(HG_SUB):
                    n = i * HG_SUB + s
                    e = jnp.exp(jnp.where(sub_row >= s, b_blk - b[n:n + 1, :], MASKED_LOG_DECAY))
                    col = jnp.sum(q_blk * k[n:n + 1, :] * e, axis=1, keepdims=True)
                    blk = jnp.where(lane64 == n, col, blk)
                a_rows.append(blk)
            a = jnp.concatenate(a_rows, axis=0).astype(mxu)
            o = o_inter + jnp.dot(a, v.astype(mxu), preferred_element_type=jnp.float32)
            b_last = b[HG_CHUNK - 1:HG_CHUNK, :]
            k_til = (k_hat * jnp.exp(b_last - b_end)).astype(mxu)
            upd = lax.dot_general(v.astype(mxu), k_til, (((0,), (0,)), ((), ())),
                                  preferred_element_type=jnp.float32)
            st = st * jnp.exp(b_last) + upd
            o = o * lax.rsqrt(jnp.mean(o * o, axis=-1, keepdims=True) + NORM_EPS)
            o = o * on * (gp * jax.nn.sigmoid(gp))
            o_ref[rows, pl.ds(c0, HG_DK)] = o.astype(o_ref.dtype)
        st_sc[hd] = st
        return carry

    lax.fori_loop(0, HG_HEADS, head, 0)

    @pl.when(pl.program_id(0) == pl.num_programs(0) - 1)
    def _():
        st_out_ref[...] = st_sc[...]


def _hgrn(h, state, g, w, lb_raw, out_norm):
    n = h.shape[0]
    tb = min(HG_BLOCK, n)
    g = g.reshape(1, -1)
    out_norm = out_norm.reshape(1, -1)
    cum = _cum_matrix()
    full = lambda a: pl.BlockSpec(a.shape, lambda i: (0,) * a.ndim)
    f32 = jnp.float32
    return pl.pallas_call(
        _hgrn_kernel,
        grid=(n // tb,),
        in_specs=[pl.BlockSpec((tb, D_MODEL), lambda i: (i, 0)), full(g), full(w),
                  full(lb_raw), full(out_norm), full(cum), full(state)],
        out_specs=[pl.BlockSpec((tb, HG_HEADS * HG_DV), lambda i: (i, 0)), full(state)],
        out_shape=[jax.ShapeDtypeStruct((n, HG_HEADS * HG_DV), jnp.bfloat16),
                   jax.ShapeDtypeStruct(state.shape, f32)],
        scratch_shapes=[pltpu.VMEM((tb, w.shape[1]), f32),
                        pltpu.VMEM((HG_HEADS, HG_DV, HG_DK), f32),
                        pltpu.VMEM((8, HG_HEADS * HG_DK), f32)],
        compiler_params=_params("arbitrary"),
        name="hgrn2",
    )(h, g, w, lb_raw, out_norm, cum, state)


def kernel(x, positions, ln_mix, ln_ffn, ln_final, mla_w_in, mla_q_norm, mla_w_q_up, mla_kv_norm,
           mla_w_kv_up, mla_w_out, hg_w_in, hg_lb, hg_out_norm, hg_w_out, peer_w_q, peer_sub_keys,
           peer_u, peer_v):
    batch, seq, d = x.shape
    n = batch * seq
    bf16 = jnp.bfloat16
    h = x.reshape(n, d)

    def route_weights(i):
        keys = peer_sub_keys[i].reshape(2 * PEER_HEADS, PEER_NKEYS, PEER_HALF).astype(bf16)
        return peer_w_q[i].T.astype(bf16), keys

    tables = [(_pack_table(peer_u[i]), _pack_table(peer_v[i])) for i in range(2)]
    cos_t, sin_t = _rope_tables(positions)
    win, wq, wkv = _mla_weights(mla_w_in[0], mla_w_q_up[0], mla_w_kv_up[0])
    q, k, v = _mla_proj(h, ln_mix[0], cos_t, sin_t, win, mla_q_norm[0], wq, mla_kv_norm[0], wkv)
    wqt0, keys0 = route_weights(0)
    wqt1, keys1 = route_weights(1)
    mla_wo = mla_w_out[0].astype(bf16)
    hg_wi = hg_w_in[0].astype(bf16)
    hg_wo = hg_w_out[0].astype(bf16)

    chunk = min(PIPE_TOKENS, seq)
    blk = min(ATTN_BLOCK, chunk)
    n_chunks = seq // chunk
    states = [jnp.zeros((HG_HEADS, HG_DV, HG_DK), jnp.float32) for _ in range(batch)]
    outs = {}

    open_chunk = [None, None]
    routed0 = []

    def route_and_dot(layer, b, j, o, h_src, row0):
        w_o, wqt, keys = ((mla_wo, wqt0, keys0), (hg_wo, wqt1, keys1))[layer]
        prev = open_chunk[layer]
        res = _route(o, h_src, row0, w_o, ln_ffn[layer], wqt, keys,
                     prev=None if prev is None else (prev[2], prev[3]))
        hc, xp, eid, gates = res[:4]
        act = _peer_dots(xp, eid, tables[layer][0])
        open_chunk[layer] = (b, j, act, gates, eid, hc)
        if layer == 0:
            routed0.append(eid)
        if prev is not None:
            finish(layer, prev, res[4])

    def finish(layer, rec, wts):
        b, j, _, _, eid, hc = rec
        out = _peer_finish(wts, eid, hc, tables[layer][1], ln_final, layer == 1)
        if layer == 0:
            layer1(b, j, out)
        else:
            outs[(b, j)] = out

    def layer0(b, j):
        after = routed0[-2][0, 0] if len(routed0) >= 2 else None
        o = _attention(q, k, v, seq, blk, b, j * chunk // blk, (j + 1) * chunk // blk, after)
        route_and_dot(0, b, j, o, h, b * seq + j * chunk)

    def layer1(b, j, hc):
        o, states[b] = _hgrn(hc, states[b], ln_mix[1], hg_wi, hg_lb, hg_out_norm[0])
        route_and_dot(1, b, j, o, hc, 0)

    for j in range(n_chunks):
        for b in range(batch):
            layer0(b, j)
    for layer in (0, 1):
        rec = open_chunk[layer]
        finish(layer, rec, _slot_weights(rec[2], rec[3]))
    out = jnp.concatenate([outs[(b, j)] for b in range(batch) for j in range(n_chunks)], axis=0)
    return out.reshape(batch, seq, d)
```

```python
import functools

import jax
import jax.numpy as jnp
from jax import lax
from jax.experimental import pallas as pl
from jax.experimental.pallas import tpu as pltpu
from jax.experimental.pallas import tpu_sc as plsc

D_MODEL = 1024
NORM_EPS = 1e-6
MLA_HEADS = 8
MLA_Q_LORA = 384
MLA_KV_LORA = 256
MLA_NOPE = 128
MLA_ROPE = 64
MLA_V = 128
ROPE_THETA = 10000.0
HG_HEADS = 8
HG_DK = 128
HG_DV = 128
PEER_HEADS = 8
PEER_NKEYS = 128
PEER_HALF = 128
PEER_TOPK = 16
PEER_SLOTS = PEER_HEADS * PEER_TOPK

LANES = 128
SC_CORES = 2
SC_SUBCORES = 16
VMEM_LIMIT = 48 * 1024 * 1024

ROW_WORDS = D_MODEL // 2
TOKEN_BLOCK = 256
ATTN_BLOCK = 1024
HG_BLOCK = 256
HG_CHUNK = 64
HG_SUB = 16
PIPE_TOKENS = 1024
GATHER_WINDOW = 32
DOTS_SLOTS = 5
AXPY_SLOTS = 7
GATHER_TOKENS = 1024
SC_LANES = 16
DOT_ROWS = 8
AXPY_CHUNKS = 8
NEG_INF = float("-inf")
MASKED_LOG_DECAY = -1e30


def _rms(x, g):
    return x * lax.rsqrt(jnp.mean(x * x, axis=-1, keepdims=True) + NORM_EPS) * g


def _params(*sem):
    return pltpu.CompilerParams(dimension_semantics=sem, vmem_limit_bytes=VMEM_LIMIT)


def _bf16_bits(x):
    return pltpu.bitcast(x.astype(jnp.bfloat16).astype(jnp.float32), jnp.uint32)


def _pack_pairs(t):
    return (_bf16_bits(t[:, :ROW_WORDS]) >> 16) | _bf16_bits(t[:, ROW_WORDS:])


def _pack_kernel(t_ref, o_ref):
    o_ref[...] = _pack_pairs(t_ref[...])


def _pack_table(tab):
    e, d = tab.shape
    rows = 512
    return pl.pallas_call(
        _pack_kernel,
        grid=(e // rows,),
        in_specs=[pl.BlockSpec((rows, d), lambda i: (i, 0))],
        out_specs=pl.BlockSpec((rows, d // 2), lambda i: (i, 0)),
        out_shape=jax.ShapeDtypeStruct((e, d // 2), jnp.uint32),
        compiler_params=_params("parallel"),
        name="pack_table",
    )(tab)


def _rope_kernel(pos_ref, inv_ref, sign_ref, c_ref, s_ref):
    ang = pos_ref[...].astype(jnp.float32) * inv_ref[...]
    c_ref[...] = jnp.cos(ang)
    s_ref[...] = jnp.sin(ang) * sign_ref[...]


def _rope_tables(positions):
    n = positions.size
    lane = jnp.arange(LANES)
    inv = ROPE_THETA ** (-(2 * (lane % (MLA_ROPE // 2))).astype(jnp.float32) / MLA_ROPE)
    sign = jnp.where((lane % MLA_ROPE) < MLA_ROPE // 2, -1.0, 1.0).astype(jnp.float32)
    tb = 1024
    out = jax.ShapeDtypeStruct((n, LANES), jnp.float32)
    return pl.pallas_call(
        _rope_kernel,
        grid=(n // tb,),
        in_specs=[pl.BlockSpec((tb, 1), lambda i: (i, 0)),
                  pl.BlockSpec((1, LANES), lambda i: (0, 0)),
                  pl.BlockSpec((1, LANES), lambda i: (0, 0))],
        out_specs=[pl.BlockSpec((tb, LANES), lambda i: (i, 0))] * 2,
        out_shape=[out, out],
        compiler_params=_params("parallel"),
        name="rope_tables",
    )(positions.reshape(n, 1), inv.reshape(1, LANES), sign.reshape(1, LANES))


def _mla_proj_kernel(x_ref, g_ref, c_ref, s_ref, win_ref, qn_ref, wq_ref, kvn_ref, wkv_ref,
                     q_out, k_out, v_out):
    hn = _rms(x_ref[...], g_ref[...]).astype(jnp.bfloat16)
    z = jnp.dot(hn, win_ref[...], preferred_element_type=jnp.float32)
    c = c_ref[...]
    s = s_ref[...]
    o_kv = MLA_Q_LORA
    o_kr = MLA_Q_LORA + MLA_KV_LORA
    k_rope = (z[:, o_kr:o_kr + LANES] * c + z[:, o_kr + LANES:o_kr + 2 * LANES] * s).astype(jnp.bfloat16)
    cq = _rms(z[:, :MLA_Q_LORA], qn_ref[...]).astype(jnp.bfloat16)
    q = jnp.dot(cq, wq_ref[...], preferred_element_type=jnp.float32)
    ckv = _rms(z[:, o_kv:o_kr], kvn_ref[...]).astype(jnp.bfloat16)
    kv = jnp.dot(ckv, wkv_ref[...], preferred_element_type=jnp.float32)
    scale = (MLA_NOPE + MLA_ROPE) ** -0.5
    hw = MLA_HEADS * LANES
    for h in range(MLA_HEADS):
        sl = slice(h * LANES, (h + 1) * LANES)
        q_out[h, :, :LANES] = (q[:, sl] * scale).astype(jnp.bfloat16)
        qr = q[:, hw + h * LANES:hw + (h + 1) * LANES] * c + q[:, 2 * hw + h * LANES:2 * hw + (h + 1) * LANES] * s
        q_out[h, :, LANES:] = (qr * scale).astype(jnp.bfloat16)
        k_out[h, :, :LANES] = kv[:, sl].astype(jnp.bfloat16)
        k_out[h, :, LANES:] = k_rope
        v_out[h] = kv[:, hw + h * LANES:hw + (h + 1) * LANES].astype(jnp.bfloat16)


def _swap_halves(w):
    half = w.shape[-1] // 2
    return jnp.concatenate([w[..., half:], w[..., :half]], axis=-1)


def _pad_lanes(w):
    return jnp.pad(w, [(0, 0)] * (w.ndim - 1) + [(0, LANES - w.shape[-1])])


def _mla_weights(w_in, w_q_up, w_kv_up):
    o_kr = MLA_Q_LORA + MLA_KV_LORA
    w_kr = w_in[:, o_kr:]
    win = jnp.concatenate([w_in[:, :o_kr], _pad_lanes(w_kr), _pad_lanes(_swap_halves(w_kr))], axis=1)
    wq = w_q_up.reshape(MLA_Q_LORA, MLA_HEADS, MLA_NOPE + MLA_ROPE)
    wq_n = wq[:, :, :MLA_NOPE].reshape(MLA_Q_LORA, -1)
    wq_r = _pad_lanes(wq[:, :, MLA_NOPE:]).reshape(MLA_Q_LORA, -1)
    wq_rs = _pad_lanes(_swap_halves(wq[:, :, MLA_NOPE:])).reshape(MLA_Q_LORA, -1)
    wqp = jnp.concatenate([wq_n, wq_r, wq_rs], axis=1)
    wkv = w_kv_up.reshape(MLA_KV_LORA, MLA_HEADS, MLA_NOPE + MLA_V)
    wkvp = jnp.concatenate([wkv[:, :, :MLA_NOPE].reshape(MLA_KV_LORA, -1),
                            wkv[:, :, MLA_NOPE:].reshape(MLA_KV_LORA, -1)], axis=1)
    return win.astype(jnp.bfloat16), wqp.astype(jnp.bfloat16), wkvp.astype(jnp.bfloat16)


def _mla_proj(h, g, cos_t, sin_t, win, q_norm, wq, kv_norm, wkv):
    n = h.shape[0]
    tb = TOKEN_BLOCK
    full = lambda a: pl.BlockSpec(a.shape, lambda i: (0,) * a.ndim)
    g = g.reshape(1, -1)
    q_norm = q_norm.reshape(1, -1)
    kv_norm = kv_norm.reshape(1, -1)
    qk_t = jax.ShapeDtypeStruct((MLA_HEADS, n, 2 * LANES), jnp.bfloat16)
    v_t = jax.ShapeDtypeStruct((MLA_HEADS, n, MLA_V), jnp.bfloat16)
    return pl.pallas_call(
        _mla_proj_kernel,
        grid=(n // tb,),
        in_specs=[pl.BlockSpec((tb, D_MODEL), lambda i: (i, 0)), full(g),
                  pl.BlockSpec((tb, LANES), lambda i: (i, 0)), pl.BlockSpec((tb, LANES), lambda i: (i, 0)),
                  full(win), full(q_norm), full(wq), full(kv_norm), full(wkv)],
        out_specs=[pl.BlockSpec((MLA_HEADS, tb, 2 * LANES), lambda i: (0, i, 0)),
                   pl.BlockSpec((MLA_HEADS, tb, 2 * LANES), lambda i: (0, i, 0)),
                   pl.BlockSpec((MLA_HEADS, tb, MLA_V), lambda i: (0, i, 0))],
        out_shape=[qk_t, qk_t, v_t],
        compiler_params=_params("parallel"),
        name="mla_proj",
    )(h, g, cos_t, sin_t, win, q_norm, wq, kv_norm, wkv)


def _attn_kernel(qi_ref, ki_ref, q_ref, k_ref, v_ref, o_ref, m_sc, l_sc, acc_sc):
    step = pl.program_id(1)
    qi = qi_ref[step]
    ki = ki_ref[step]

    @pl.when(ki == 0)
    def _():
        m_sc[...] = jnp.full(m_sc.shape, NEG_INF, jnp.float32)
        l_sc[...] = jnp.zeros(l_sc.shape, jnp.float32)
        acc_sc[...] = jnp.zeros(acc_sc.shape, jnp.float32)

    def update(masked):
        sc = lax.dot_general(q_ref[...], k_ref[...], (((1,), (1,)), ((), ())),
                             preferred_element_type=jnp.float32)
        if masked:
            row = lax.broadcasted_iota(jnp.int32, sc.shape, 0)
            col = lax.broadcasted_iota(jnp.int32, sc.shape, 1)
            sc = jnp.where(col <= row, sc, NEG_INF)
        m_prev = m_sc[...]
        m_next = jnp.maximum(m_prev, jnp.max(sc, axis=1, keepdims=True))
        p = jnp.exp(sc - m_next[:, :1])
        alpha = jnp.exp(m_prev - m_next)
        l_sc[...] = alpha * l_sc[...] + jnp.sum(p, axis=1, keepdims=True)
        acc_sc[...] = alpha * acc_sc[...] + jnp.dot(p.astype(jnp.bfloat16), v_ref[...],
                                                    preferred_element_type=jnp.float32)
        m_sc[...] = m_next

    @pl.when(ki < qi)
    def _():
        update(False)

    @pl.when(ki == qi)
    def _():
        update(True)
        o_ref[...] = (acc_sc[...] / l_sc[...]).astype(o_ref.dtype)


def _attention(q, k, v, seq, blk, b, q_lo, q_hi, after=None):
    nb = seq // blk
    qi = jnp.array([i for i in range(q_lo, q_hi) for _ in range(i + 1)], jnp.int32)
    ki = jnp.array([j for i in range(q_lo, q_hi) for j in range(i + 1)], jnp.int32)
    if after is not None:
        qi = qi + jnp.minimum(after, 0)
    grid_spec = pltpu.PrefetchScalarGridSpec(
        num_scalar_prefetch=2,
        grid=(MLA_HEADS, qi.shape[0]),
        in_specs=[pl.BlockSpec((None, blk, 2 * LANES), lambda h, s, qi, ki: (h, b * nb + qi[s], 0)),
                  pl.BlockSpec((None, blk, 2 * LANES), lambda h, s, qi, ki: (h, b * nb + ki[s], 0)),
                  pl.BlockSpec((None, blk, MLA_V), lambda h, s, qi, ki: (h, b * nb + ki[s], 0))],
        out_specs=pl.BlockSpec((blk, MLA_V), lambda h, s, qi, ki: (qi[s] - q_lo, h)),
        scratch_shapes=[pltpu.VMEM((blk, LANES), jnp.float32), pltpu.VMEM((blk, LANES), jnp.float32),
                        pltpu.VMEM((blk, MLA_V), jnp.float32)],
    )
    return pl.pallas_call(
        _attn_kernel,
        grid_spec=grid_spec,
        out_shape=jax.ShapeDtypeStruct(((q_hi - q_lo) * blk, MLA_HEADS * MLA_V), jnp.bfloat16),
        compiler_params=_params("parallel", "arbitrary"),
        name="mla_attention",
    )(qi, ki, q, k, v)


def _col_reduce(x, op, reduce_fn):
    slabs = [x[i:i + 8] for i in range(0, x.shape[0], 8)]
    while len(slabs) > 1:
        nxt = [op(slabs[i], slabs[i + 1]) for i in range(0, len(slabs) - 1, 2)]
        if len(slabs) % 2:
            nxt.append(slabs[-1])
        slabs = nxt
    return reduce_fn(slabs[0], axis=0, keepdims=True)


def _top_rows(vals, ids, count, out_rows):
    t = vals.shape[1]
    big = jnp.int32(2 ** 30)
    orow = lax.broadcasted_iota(jnp.int32, (out_rows, t), 0)

    def body(r, carry):
        cur, ov, oi = carry
        m = _col_reduce(cur, jnp.maximum, jnp.max)
        pick = _col_reduce(jnp.where(cur == m, ids, big), jnp.minimum, jnp.min)
        cur = jnp.where(ids == pick, NEG_INF, cur)
        ov = jnp.where(orow == r, m, ov)
        oi = jnp.where(orow == r, pick, oi)
        return cur, ov, oi

    init = (vals, jnp.zeros((out_rows, t), jnp.float32), jnp.zeros((out_rows, t), jnp.int32))
    _, ov, oi = lax.fori_loop(0, count, body, init)
    return ov, oi


_ROW_SLABS = [(0, 0, 16), (1, 0, 8)] + [(a, 0, 8) for a in range(2, 8)]
_COL_SLAB = (8, 16, 0)
_PAIR_ROWS = sum(hi - lo for _, lo, hi in _ROW_SLABS) + (_COL_SLAB[1] - _COL_SLAB[0])


def _slot_weight_words(act, gate):
    hi = _bf16_bits(gate * (0.5 * act * (1.0 + lax.erf(act * (2.0 ** -0.5)))))
    return hi | (hi >> 16)


def _route_kernel(*refs, with_prev):
    if with_prev:
        (o_ref, h_ref, wo_ref, g_ref, wqt_ref, keys_ref, pos_ref, act_p_ref, gate_p_ref,
         hn_out, xn_out, eid_out, gate_out, wts_p_out, qt_sc, v_sc, i_sc) = refs
        wts_p_out[...] = _slot_weight_words(act_p_ref[...], gate_p_ref[...])
    else:
        (o_ref, h_ref, wo_ref, g_ref, wqt_ref, keys_ref, pos_ref,
         hn_out, xn_out, eid_out, gate_out, qt_sc, v_sc, i_sc) = refs
    tb = h_ref.shape[0]
    hnew = h_ref[...] + jnp.dot(o_ref[...], wo_ref[...], preferred_element_type=jnp.float32)
    hn_out[...] = hnew
    xn = _rms(hnew, g_ref[...])
    xn_out[...] = _pack_pairs(xn)
    qt_sc[...] = lax.dot_general(wqt_ref[...], xn.astype(jnp.bfloat16), (((1,), (1,)), ((), ())),
                                 preferred_element_type=jnp.float32).astype(jnp.bfloat16)
    key_ids = lax.broadcasted_iota(jnp.int32, (PEER_NKEYS, tb), 0)

    def group(g, carry):
        row0 = pl.multiple_of(g * PEER_HALF, PEER_HALF)
        st = jnp.dot(keys_ref[g], qt_sc[pl.ds(row0, PEER_HALF), :], preferred_element_type=jnp.float32)
        tv, ti = _top_rows(st, key_ids, PEER_TOPK, PEER_TOPK)
        out0 = pl.multiple_of(g * PEER_TOPK, PEER_TOPK)
        v_sc[pl.ds(out0, PEER_TOPK), :] = tv
        i_sc[pl.ds(out0, PEER_TOPK), :] = ti
        return carry

    lax.fori_loop(0, 2 * PEER_HEADS, group, 0)

    pos = pos_ref[...]

    def head(hd, carry):
        base = pl.multiple_of(hd * 2 * PEER_TOPK, 2 * PEER_TOPK)
        v1 = v_sc[pl.ds(base, PEER_TOPK), :]
        i1 = i_sc[pl.ds(base, PEER_TOPK), :]
        v2 = v_sc[pl.ds(base + PEER_TOPK, PEER_TOPK), :]
        i2 = i_sc[pl.ds(base + PEER_TOPK, PEER_TOPK), :]
        cv, ce = [], []
        for a, lo, hi in _ROW_SLABS:
            cv.append(v1[a:a + 1, :] + v2[lo:hi, :])
            ce.append(i1[a:a + 1, :] * PEER_NKEYS + i2[lo:hi, :])
        a_lo, a_hi, b = _COL_SLAB
        cv.append(v1[a_lo:a_hi, :] + v2[b:b + 1, :])
        ce.append(i1[a_lo:a_hi, :] * PEER_NKEYS + i2[b:b + 1, :])
        cv = jnp.concatenate(cv, axis=0)
        ce = jnp.concatenate(ce, axis=0)
        tv, tp = _top_rows(cv, jnp.broadcast_to(pos, cv.shape), PEER_TOPK, PEER_TOPK)
        te = jnp.zeros((PEER_TOPK, tb), jnp.int32)
        orow = lax.broadcasted_iota(jnp.int32, (PEER_TOPK, tb), 0)
        for r in range(PEER_TOPK):
            e_r = jnp.sum(jnp.where(pos == tp[r:r + 1, :], ce, 0), axis=0, keepdims=True)
            te = jnp.where(orow == r, e_r, te)
        ex = jnp.exp(tv - tv[0:1, :])
        gates = ex / jnp.sum(ex, axis=0, keepdims=True)
        out0 = pl.multiple_of(hd * PEER_TOPK, PEER_TOPK)
        v_sc[pl.ds(out0, PEER_TOPK), :] = gates
        i_sc[pl.ds(out0, PEER_TOPK), :] = te
        return carry

    lax.fori_loop(0, PEER_HEADS, head, 0)

    eid_out[...] = jnp.transpose(i_sc[:PEER_SLOTS, :].astype(jnp.float32)).astype(jnp.int32)
    gate_out[...] = jnp.transpose(v_sc[:PEER_SLOTS, :])


def _pair_positions():
    pos = [a * PEER_TOPK + b for a, lo, hi in _ROW_SLABS for b in range(lo, hi)]
    a_lo, a_hi, b = _COL_SLAB
    pos += [a * PEER_TOPK + b for a in range(a_lo, a_hi)]
    return jnp.array(pos, jnp.int32).reshape(_PAIR_ROWS, 1)


def _route(o, h, h_row0, w_out, g, wqt, keys, prev=None):
    n = o.shape[0]
    tb = TOKEN_BLOCK
    full = lambda a: pl.BlockSpec(a.shape, lambda i: (0,) * a.ndim)
    g = g.reshape(1, -1)
    pos = _pair_positions()
    row = pl.BlockSpec((tb, D_MODEL), lambda i: (i, 0))
    slot = pl.BlockSpec((tb, PEER_SLOTS), lambda i: (i, 0))
    h_row = pl.BlockSpec((tb, D_MODEL), lambda i: (i + h_row0 // tb, 0))
    f32 = jnp.float32
    args = [o, h, w_out, g, wqt, keys, pos]
    in_specs = [row, h_row, full(w_out), full(g), full(wqt), full(keys), full(pos)]
    out_specs = [row, pl.BlockSpec((tb, ROW_WORDS), lambda i: (i, 0)), slot, slot]
    out_shape = [jax.ShapeDtypeStruct((n, D_MODEL), f32), jax.ShapeDtypeStruct((n, ROW_WORDS), jnp.uint32),
                 jax.ShapeDtypeStruct((n, PEER_SLOTS), jnp.int32), jax.ShapeDtypeStruct((n, PEER_SLOTS), f32)]
    if prev is not None:
        args += list(prev)
        in_specs += [slot, slot]
        out_specs.append(slot)
        out_shape.append(jax.ShapeDtypeStruct((n, PEER_SLOTS), jnp.uint32))
    return pl.pallas_call(
        functools.partial(_route_kernel, with_prev=prev is not None),
        grid=(n // tb,),
        in_specs=in_specs,
        out_specs=out_specs,
        out_shape=out_shape,
        scratch_shapes=[pltpu.VMEM((2 * PEER_HEADS * PEER_HALF, tb), jnp.bfloat16),
                        pltpu.VMEM((2 * PEER_HEADS * PEER_TOPK, tb), f32),
                        pltpu.VMEM((2 * PEER_HEADS * PEER_TOPK, tb), jnp.int32)],
        compiler_params=_params("parallel"),
        name="peer_route",
    )(*args)


def _sc_mesh():
    return plsc.VectorSubcoreMesh(core_axis_name="c", subcore_axis_name="s")


def _sc_params():
    return pltpu.CompilerParams(needs_layout_passes=False)


def _worker_id():
    return lax.axis_index("s") * SC_CORES + lax.axis_index("c")


def _gather_rows(tab_hbm, idx_ref, dst_ref, sem):
    return pltpu.make_async_copy(tab_hbm.at[idx_ref], dst_ref, sem)


def _unpack16(word):
    lo = lax.bitcast_convert_type(word << 16, jnp.float32)
    hi = lax.bitcast_convert_type(word & jnp.uint32(0xFFFF0000), jnp.float32)
    return lo, hi


def _as_bf16(word):
    return plsc.bitcast(word, jnp.bfloat16)


def _sc_dots(tab, idx, x):
    t = x.shape[0]
    workers = SC_CORES * SC_SUBCORES
    tok_w = t // workers
    per_worker = tok_w * PEER_SLOTS
    w = GATHER_WINDOW
    slots = DOTS_SLOTS
    n_win = per_worker // w
    win_per_tok = PEER_SLOTS // w
    n_chunks = ROW_WORDS // SC_LANES
    f32 = jnp.float32

    @functools.partial(
        pl.kernel, out_type=jax.ShapeDtypeStruct((t * PEER_SLOTS,), f32), mesh=_sc_mesh(),
        scratch_types=[pltpu.VMEM((per_worker,), jnp.int32),
                       pltpu.VMEM((tok_w, ROW_WORDS), jnp.uint32),
                       pltpu.VMEM((slots, w, ROW_WORDS), jnp.uint32),
                       pltpu.VMEM((per_worker,), f32),
                       pltpu.VMEM((w * SC_LANES,), f32),
                       pltpu.SemaphoreType.DMA((slots,))],
        compiler_params=_sc_params(), name="peer_dots")
    def dots(tab_hbm, i_hbm, x_hbm, act_hbm, idx_v, x_v, rows, act_v, part_v, sem):
        wid = _worker_id()
        base = pl.multiple_of(wid * per_worker, per_worker)
        tok0 = pl.multiple_of(wid * tok_w, tok_w)
        pltpu.sync_copy(i_hbm.at[pl.ds(base, per_worker)], idx_v)
        pltpu.sync_copy(x_hbm.at[pl.ds(tok0, tok_w)], x_v)

        def gather(win, slot):
            ix = idx_v.at[pl.ds(pl.multiple_of(win * w, w), w)]
            return _gather_rows(tab_hbm, ix, rows.at[slot], sem.at[slot])

        for s in range(slots - 1):
            gather(s, s).start()
        lane = lax.iota(jnp.int32, SC_LANES)

        @pl.loop(0, n_win)
        def _(win):
            slot = lax.rem(win, slots)
            nxt = win + slots - 1

            @pl.when(nxt < n_win)
            def _():
                gather(nxt, lax.rem(nxt, slots)).start()

            gather(win, slot).wait()
            tok = win // win_per_tok

            @plsc.parallel_loop(0, w // DOT_ROWS)
            def _(g):
                r0 = g * DOT_ROWS
                acc_lo = [jnp.zeros((SC_LANES,), f32) for _ in range(DOT_ROWS)]
                acc_hi = [jnp.zeros((SC_LANES,), f32) for _ in range(DOT_ROWS)]
                for c in range(0, n_chunks, 2):
                    xa = _as_bf16(x_v[tok, pl.ds(c * SC_LANES, SC_LANES)])
                    xb = _as_bf16(x_v[tok, pl.ds((c + 1) * SC_LANES, SC_LANES)])
                    for r in range(DOT_ROWS):
                        ra = _as_bf16(rows[slot, r0 + r, pl.ds(c * SC_LANES, SC_LANES)])
                        rb = _as_bf16(rows[slot, r0 + r, pl.ds((c + 1) * SC_LANES, SC_LANES)])
                        lo, hi = _unpack16(plsc.bitcast(ra * xa + rb * xb, jnp.uint32))
                        acc_lo[r] = acc_lo[r] + lo
                        acc_hi[r] = acc_hi[r] + hi
                for r in range(DOT_ROWS):
                    part_v[pl.ds(pl.multiple_of((r0 + r) * SC_LANES, SC_LANES), SC_LANES)] = acc_lo[r] + acc_hi[r]

            for blk in range(w // SC_LANES):
                res = jnp.zeros((SC_LANES,), f32)
                for l in range(SC_LANES):
                    res = res + plsc.load_gather(part_v, [lane * SC_LANES + (blk * SC_LANES * SC_LANES + l)])
                act_v[pl.ds(pl.multiple_of(win * w + blk * SC_LANES, SC_LANES), SC_LANES)] = res

        pltpu.sync_copy(act_v, act_hbm.at[pl.ds(base, per_worker)])

    return dots(tab, idx, x)


def _sc_axpy(tab, idx, wts):
    p = idx.shape[0]
    t = p // PEER_SLOTS
    workers = SC_CORES * SC_SUBCORES
    tok_w = t // workers
    per_worker = tok_w * PEER_SLOTS
    w = GATHER_WINDOW
    slots = AXPY_SLOTS
    n_win = per_worker // w
    win_per_tok = PEER_SLOTS // w
    passes = ROW_WORDS // (SC_LANES * AXPY_CHUNKS)
    f32 = jnp.float32

    @functools.partial(
        pl.kernel, out_type=jax.ShapeDtypeStruct((t, D_MODEL), f32), mesh=_sc_mesh(),
        scratch_types=[pltpu.VMEM((per_worker,), jnp.int32),
                       pltpu.VMEM((per_worker,), jnp.uint32),
                       pltpu.VMEM((slots, w, ROW_WORDS), jnp.uint32),
                       pltpu.VMEM((2, D_MODEL), f32),
                       pltpu.SemaphoreType.DMA((slots,)),
                       pltpu.SemaphoreType.DMA((2,))],
        compiler_params=_sc_params(), name="peer_axpy")
    def axpy(tab_hbm, i_hbm, w_hbm, y_hbm, idx_v, w_v, rows, y_v, sem, sem_y):
        wid = _worker_id()
        base = pl.multiple_of(wid * per_worker, per_worker)
        tok0 = wid * tok_w
        pltpu.sync_copy(i_hbm.at[pl.ds(base, per_worker)], idx_v)
        pltpu.sync_copy(w_hbm.at[pl.ds(base, per_worker)], w_v)

        def gather(win, slot):
            ix = idx_v.at[pl.ds(pl.multiple_of(win * w, w), w)]
            return _gather_rows(tab_hbm, ix, rows.at[slot], sem.at[slot])

        def y_write(tok, buf):
            return pltpu.make_async_copy(y_v.at[buf], y_hbm.at[tok0 + tok], sem_y.at[buf])

        for s in range(slots - 1):
            gather(s, s).start()

        @pl.loop(0, n_win)
        def _(win):
            slot = lax.rem(win, slots)
            nxt = win + slots - 1

            @pl.when(nxt < n_win)
            def _():
                gather(nxt, lax.rem(nxt, slots)).start()

            gather(win, slot).wait()
            tok = win // win_per_tok
            part = lax.rem(win, win_per_tok)
            buf = lax.rem(tok, 2)

            @pl.when(part == 0)
            def _():
                @pl.when(tok >= 2)
                def _():
                    y_write(tok - 2, buf).wait()

                for c in range(D_MODEL // SC_LANES):
                    y_v[buf, pl.ds(c * SC_LANES, SC_LANES)] = jnp.zeros((SC_LANES,), f32)

            for ps in range(passes):

                def group(g, accs):
                    accs = list(accs)
                    row0 = pl.multiple_of(g * SC_LANES, SC_LANES)
                    w_grp = w_v[pl.ds(pl.multiple_of(win * w + row0, SC_LANES), SC_LANES)]
                    for k in range(0, SC_LANES, 2):
                        wa = _as_bf16(jnp.take_along_axis(w_grp, jnp.full((SC_LANES,), k, jnp.int32), axis=0))
                        wb = _as_bf16(jnp.take_along_axis(w_grp, jnp.full((SC_LANES,), k + 1, jnp.int32), axis=0))
                        for c in range(AXPY_CHUNKS):
                            col = (ps * AXPY_CHUNKS + c) * SC_LANES
                            ra = _as_bf16(rows[slot, row0 + k, pl.ds(col, SC_LANES)])
                            rb = _as_bf16(rows[slot, row0 + k + 1, pl.ds(col, SC_LANES)])
                            lo, hi = _unpack16(plsc.bitcast(ra * wa + rb * wb, jnp.uint32))
                            accs[2 * c] = accs[2 * c] + lo
                            accs[2 * c + 1] = accs[2 * c + 1] + hi
                    return tuple(accs)

                zero = tuple(jnp.zeros((SC_LANES,), f32) for _ in range(2 * AXPY_CHUNKS))
                accs = lax.fori_loop(0, w // SC_LANES, group, zero)
                for c in range(AXPY_CHUNKS):
                    col = (ps * AXPY_CHUNKS + c) * SC_LANES
                    y_v[buf, pl.ds(col, SC_LANES)] = y_v[buf, pl.ds(col, SC_LANES)] + accs[2 * c]
                    y_v[buf, pl.ds(ROW_WORDS + col, SC_LANES)] = (
                        y_v[buf, pl.ds(ROW_WORDS + col, SC_LANES)] + accs[2 * c + 1])

            @pl.when(part == win_per_tok - 1)
            def _():
                y_write(tok, buf).start()

        for tok in (tok_w - 2, tok_w - 1):
            y_write(tok, tok % 2).wait()

    return axpy(tab, idx, wts)


def _slot_weight_kernel(act_ref, gate_ref, o_ref):
    o_ref[...] = _slot_weight_words(act_ref[...], gate_ref[...])


def _slot_weights(act, gates):
    n = gates.shape[0]
    tb = 1024
    blk = pl.BlockSpec((tb, PEER_SLOTS), lambda i: (i, 0))
    return pl.pallas_call(
        _slot_weight_kernel, grid=(n // tb,), in_specs=[blk, blk], out_specs=blk,
        out_shape=jax.ShapeDtypeStruct((n, PEER_SLOTS), jnp.uint32),
        compiler_params=_params("parallel"), name="peer_slot_weights",
    )(act, gates)


def _residual_kernel(h_ref, y_ref, g_ref, o_ref, *, final_norm):
    out = h_ref[...] + y_ref[...]
    o_ref[...] = _rms(out, g_ref[...]) if final_norm else out


def _residual(h, y, g_final, final_norm):
    n = h.shape[0]
    tb = 512
    blk = pl.BlockSpec((tb, D_MODEL), lambda i: (i, 0))
    g_final = g_final.reshape(1, -1)
    return pl.pallas_call(
        functools.partial(_residual_kernel, final_norm=final_norm),
        grid=(n // tb,), in_specs=[blk, blk, pl.BlockSpec((1, D_MODEL), lambda i: (0, 0))], out_specs=blk,
        out_shape=jax.ShapeDtypeStruct((n, D_MODEL), jnp.float32),
        compiler_params=_params("parallel"), name="peer_residual",
    )(h, y, g_final)


def _launch_slices(n):
    tc = min(GATHER_TOKENS, n)
    return [slice(c * tc, (c + 1) * tc) for c in range(n // tc)]


def _peer_dots(xp, eid, tab_u):
    n = xp.shape[0]
    act = jnp.concatenate([_sc_dots(tab_u, eid[tok].reshape(-1), xp[tok]) for tok in _launch_slices(n)])
    return act.reshape(n, PEER_SLOTS)


def _peer_finish(wts, eid, h, tab_v, g_final, final_norm):
    n = h.shape[0]
    y = jnp.concatenate([_sc_axpy(tab_v, eid[tok].reshape(-1), wts[tok].reshape(-1))
                         for tok in _launch_slices(n)])
    return _residual(h, y, g_final, final_norm)


_N_SUB = HG_CHUNK // HG_SUB
_OFF_PAIRS = [(i, j) for i in range(_N_SUB) for j in range(i)]


def _cum_matrix():
    t = jnp.arange(HG_CHUNK)[:, None]
    r = jnp.arange(HG_CHUNK)[None, :]
    sub = t // HG_SUB
    incl = r <= t
    before = r < sub * HG_SUB
    end = r < (sub + 1) * HG_SUB
    return jnp.concatenate([incl, before, end], axis=0).astype(jnp.float32)


def _hgrn_kernel(h_ref, g_ref, w_ref, lb_ref, on_ref, cum_ref, st_in_ref, o_ref, st_out_ref,
                 z_sc, st_sc, lb_sc):
    @pl.when(pl.program_id(0) == 0)
    def _():
        st_sc[...] = st_in_ref[...]

    tb = h_ref.shape[0]
    hn = _rms(h_ref[...], g_ref[...]).astype(jnp.bfloat16)
    z_sc[...] = jnp.dot(hn, w_ref[...], preferred_element_type=jnp.float32)
    lbr = lb_ref[...]
    mx = jnp.max(lbr, axis=0, keepdims=True)
    ex = jnp.exp(lbr - mx)
    prob = ex / jnp.sum(ex, axis=0, keepdims=True)
    lb_sc[...] = jnp.broadcast_to((prob[0:1, :] + prob[1:2, :]) - prob[0:1, :], lb_sc.shape)
    wf = HG_HEADS * HG_DK
    sub_row = lax.broadcasted_iota(jnp.int32, (HG_SUB, HG_DK), 0)
    lane64 = lax.broadcasted_iota(jnp.int32, (HG_SUB, HG_CHUNK), 1)

    mxu = jnp.bfloat16
    nt = (((1,), (1,)), ((), ()))

    def head(hd, carry):
        c0 = pl.multiple_of(hd * HG_DK, HG_DK)
        lb = lb_sc[0:1, pl.ds(c0, HG_DK)]
        on = on_ref[0:1, pl.ds(c0, HG_DK)]
        st = st_sc[hd]
        for ch in range(tb // HG_CHUNK):
            rows = slice(ch * HG_CHUNK, (ch + 1) * HG_CHUNK)
            qp = z_sc[rows, pl.ds(c0, HG_DK)]
            fp = z_sc[rows, pl.ds(pl.multiple_of(wf + c0, HG_DK), HG_DK)]
            v = z_sc[rows, pl.ds(pl.multiple_of(2 * wf + c0, HG_DK), HG_DK)]
            gp = z_sc[rows, pl.ds(pl.multiple_of(2 * wf + HG_HEADS * HG_DV + c0, HG_DK), HG_DK)]
            f = lb + (1.0 - lb) * jax.nn.sigmoid(fp)
            lf = jnp.log(f)
            k = 1.0 - f
            q = qp * jax.nn.sigmoid(qp)
            cums = jnp.dot(cum_ref[...], lf, precision=lax.Precision.HIGHEST,
                           preferred_element_type=jnp.float32)
            b = cums[:HG_CHUNK]
            b_start = cums[HG_CHUNK:2 * HG_CHUNK]
            b_end = cums[2 * HG_CHUNK:]
            q_hat = q * jnp.exp(b - b_start)
            k_hat = k * jnp.exp(b_end - b)
            o_inter = lax.dot_general((q_hat * jnp.exp(b_start)).astype(mxu), st.astype(mxu), nt,
                                      preferred_element_type=jnp.float32)
            stacked = []
            for (i, j) in _OFF_PAIRS:
                d_ij = jnp.exp(b_start[i * HG_SUB:i * HG_SUB + 1, :] - b_end[j * HG_SUB:j * HG_SUB + 1, :])
                stacked.append(q_hat[i * HG_SUB:(i + 1) * HG_SUB, :] * d_ij)
            stacked = jnp.concatenate(stacked, axis=0).astype(mxu)
            off = lax.dot_general(stacked, k_hat.astype(mxu), nt,
                                  preferred_element_type=jnp.float32)
            a_rows = []
            for i in range(_N_SUB):
                blk = jnp.zeros((HG_SUB, HG_CHUNK), jnp.float32)
                for p, (pi, pj) in enumerate(_OFF_PAIRS):
                    if pi == i:
                        in_j = (lane64 >= pj * HG_SUB) & (lane64 < (pj + 1) * HG_SUB)
                        blk = jnp.where(in_j, off[p * HG_SUB:(p + 1) * HG_SUB, :], blk)
                b_blk = b[i * HG_SUB:(i + 1) * HG_SUB, :]
                q_blk = q[i * HG_SUB:(i + 1) * HG_SUB, :]
                for s in range(HG_SUB):
                    n = i * HG_SUB + s
                    e = jnp.exp(jnp.where(sub_row >= s, b_blk - b[n:n + 1, :], MASKED_LOG_DECAY))
                    col = jnp.sum(q_blk * k[n:n + 1, :] * e, axis=1, keepdims=True)
                    blk = jnp.where(lane64 == n, col, blk)
                a_rows.append(blk)
            a = jnp.concatenate(a_rows, axis=0).astype(mxu)
            o = o_inter + jnp.dot(a, v.astype(mxu), preferred_element_type=jnp.float32)
            b_last = b[HG_CHUNK - 1:HG_CHUNK, :]
            k_til = (k_hat * jnp.exp(b_last - b_end)).astype(mxu)
            upd = lax.dot_general(v.astype(mxu), k_til, (((0,), (0,)), ((), ())),
                                  preferred_element_type=jnp.float32)
            st = st * jnp.exp(b_last) + upd
            o = o * lax.rsqrt(jnp.mean(o * o, axis=-1, keepdims=True) + NORM_EPS)
            o = o * on * (gp * jax.nn.sigmoid(gp))
            o_ref[rows, pl.ds(c0, HG_DK)] = o.astype(o_ref.dtype)
        st_sc[hd] = st
        return carry

    lax.fori_loop(0, HG_HEADS, head, 0)

    @pl.when(pl.program_id(0) == pl.num_programs(0) - 1)
    def _():
        st_out_ref[...] = st_sc[...]


def _hgrn(h, state, g, w, lb_raw, out_norm):
    n = h.shape[0]
    tb = min(HG_BLOCK, n)
    g = g.reshape(1, -1)
    out_norm = out_norm.reshape(1, -1)
    cum = _cum_matrix()
    full = lambda a: pl.BlockSpec(a.shape, lambda i: (0,) * a.ndim)
    f32 = jnp.float32
    return pl.pallas_call(
        _hgrn_kernel,
        grid=(n // tb,),
        in_specs=[pl.BlockSpec((tb, D_MODEL), lambda i: (i, 0)), full(g), full(w),
                  full(lb_raw), full(out_norm), full(cum), full(state)],
        out_specs=[pl.BlockSpec((tb, HG_HEADS * HG_DV), lambda i: (i, 0)), full(state)],
        out_shape=[jax.ShapeDtypeStruct((n, HG_HEADS * HG_DV), jnp.bfloat16),
                   jax.ShapeDtypeStruct(state.shape, f32)],
        scratch_shapes=[pltpu.VMEM((tb, w.shape[1]), f32),
                        pltpu.VMEM((HG_HEADS, HG_DV, HG_DK), f32),
                        pltpu.VMEM((8, HG_HEADS * HG_DK), f32)],
        compiler_params=_params("arbitrary"),
        name="hgrn2",
    )(h, g, w, lb_raw, out_norm, cum, state)


def kernel(x, positions, ln_mix, ln_ffn, ln_final, mla_w_in, mla_q_norm, mla_w_q_up, mla_kv_norm,
           mla_w_kv_up, mla_w_out, hg_w_in, hg_lb, hg_out_norm, hg_w_out, peer_w_q, peer_sub_keys,
           peer_u, peer_v):
    batch, seq, d = x.shape
    n = batch * seq
    bf16 = jnp.bfloat16
    h = x.reshape(n, d)

    def route_weights(i):
        keys = peer_sub_keys[i].reshape(2 * PEER_HEADS, PEER_NKEYS, PEER_HALF).astype(bf16)
        return peer_w_q[i].T.astype(bf16), keys

    tables = [(_pack_table(peer_u[i]), _pack_table(peer_v[i])) for i in range(2)]
    cos_t, sin_t = _rope_tables(positions)
    win, wq, wkv = _mla_weights(mla_w_in[0], mla_w_q_up[0], mla_w_kv_up[0])
    q, k, v = _mla_proj(h, ln_mix[0], cos_t, sin_t, win, mla_q_norm[0], wq, mla_kv_norm[0], wkv)
    wqt0, keys0 = route_weights(0)
    wqt1, keys1 = route_weights(1)
    mla_wo = mla_w_out[0].astype(bf16)
    hg_wi = hg_w_in[0].astype(bf16)
    hg_wo = hg_w_out[0].astype(bf16)

    chunk = min(PIPE_TOKENS, seq)
    blk = min(ATTN_BLOCK, chunk)
    n_chunks = seq // chunk
    states = [jnp.zeros((HG_HEADS, HG_DV, HG_DK), jnp.float32) for _ in range(batch)]
    outs = {}

    open_chunk = [None, None]
    routed0 = []

    def route_and_dot(layer, b, j, o, h_src, row0):
        w_o, wqt, keys = ((mla_wo, wqt0, keys0), (hg_wo, wqt1, keys1))[layer]
        prev = open_chunk[layer]
        res = _route(o, h_src, row0, w_o, ln_ffn[layer], wqt, keys,
                     prev=None if prev is None else (prev[2], prev[3]))
        hc, xp, eid, gates = res[:4]
        act = _peer_dots(xp, eid, tables[layer][0])
        open_chunk[layer] = (b, j, act, gates, eid, hc)
        if layer == 0:
            routed0.append(eid)
        if prev is not None:
            finish(layer, prev, res[4])

    def finish(layer, rec, wts):
        b, j, _, _, eid, hc = rec
        out = _peer_finish(wts, eid, hc, tables[layer][1], ln_final, layer == 1)
        if layer == 0:
            layer1(b, j, out)
        else:
            outs[(b, j)] = out

    def layer0(b, j):
        after = routed0[-2][0, 0] if len(routed0) >= 2 else None
        o = _attention(q, k, v, seq, blk, b, j * chunk // blk, (j + 1) * chunk // blk, after)
        route_and_dot(0, b, j, o, h, b * seq + j * chunk)

    def layer1(b, j, hc):
        o, states[b] = _hgrn(hc, states[b], ln_mix[1], hg_wi, hg_lb, hg_out_norm[0])
        route_and_dot(1, b, j, o, hc, 0)

    for j in range(n_chunks):
        for b in range(batch):
            layer0(b, j)
    for layer in (0, 1):
        rec = open_chunk[layer]
        finish(layer, rec, _slot_weights(rec[2], rec[3]))
    out = jnp.concatenate([outs[(b, j)] for b in range(batch) for j in range(n_chunks)], axis=0)
    return out.reshape(batch, seq, d)
```

```python
import functools

import jax
import jax.numpy as jnp
from jax import lax
from jax.experimental import pallas as pl
from jax.experimental.pallas import tpu as pltpu
from jax.experimental.pallas import tpu_sc as plsc

D_MODEL = 1024
NORM_EPS = 1e-6
MLA_HEADS = 8
MLA_Q_LORA = 384
MLA_KV_LORA = 256
MLA_NOPE = 128
MLA_ROPE = 64
MLA_V = 128
ROPE_THETA = 10000.0
HG_HEADS = 8
HG_DK = 128
HG_DV = 128
PEER_HEADS = 8
PEER_NKEYS = 128
PEER_HALF = 128
PEER_TOPK = 16
PEER_SLOTS = PEER_HEADS * PEER_TOPK

LANES = 128
SC_CORES = 2
SC_SUBCORES = 16
VMEM_LIMIT = 48 * 1024 * 1024

ROW_WORDS = D_MODEL // 2
TOKEN_BLOCK = 256
ATTN_BLOCK = 1024
HG_BLOCK = 256
HG_CHUNK = 64
HG_SUB = 16
PIPE_TOKENS = 1024
GATHER_WINDOW = 32
DOTS_SLOTS = 5
AXPY_SLOTS = 7
GATHER_TOKENS = 1024
SC_LANES = 16
DOT_ROWS = 8
AXPY_CHUNKS = 8
NEG_INF = float("-inf")
MASKED_LOG_DECAY = -1e30


def _rms(x, g):
    return x * lax.rsqrt(jnp.mean(x * x, axis=-1, keepdims=True) + NORM_EPS) * g


def _params(*sem):
    return pltpu.CompilerParams(dimension_semantics=sem, vmem_limit_bytes=VMEM_LIMIT)


def _bf16_bits(x):
    return pltpu.bitcast(x.astype(jnp.bfloat16).astype(jnp.float32), jnp.uint32)


def _pack_pairs(t):
    return (_bf16_bits(t[:, :ROW_WORDS]) >> 16) | _bf16_bits(t[:, ROW_WORDS:])


def _pack_kernel(t_ref, o_ref):
    o_ref[...] = _pack_pairs(t_ref[...])


def _pack_table(tab):
    e, d = tab.shape
    rows = 512
    return pl.pallas_call(
        _pack_kernel,
        grid=(e // rows,),
        in_specs=[pl.BlockSpec((rows, d), lambda i: (i, 0))],
        out_specs=pl.BlockSpec((rows, d // 2), lambda i: (i, 0)),
        out_shape=jax.ShapeDtypeStruct((e, d // 2), jnp.uint32),
        compiler_params=_params("parallel"),
        name="pack_table",
    )(tab)


def _rope_kernel(pos_ref, inv_ref, sign_ref, c_ref, s_ref):
    ang = pos_ref[...].astype(jnp.float32) * inv_ref[...]
    c_ref[...] = jnp.cos(ang)
    s_ref[...] = jnp.sin(ang) * sign_ref[...]


def _rope_tables(positions):
    n = positions.size
    lane = jnp.arange(LANES)
    inv = ROPE_THETA ** (-(2 * (lane % (MLA_ROPE // 2))).astype(jnp.float32) / MLA_ROPE)
    sign = jnp.where((lane % MLA_ROPE) < MLA_ROPE // 2, -1.0, 1.0).astype(jnp.float32)
    tb = 1024
    out = jax.ShapeDtypeStruct((n, LANES), jnp.float32)
    return pl.pallas_call(
        _rope_kernel,
        grid=(n // tb,),
        in_specs=[pl.BlockSpec((tb, 1), lambda i: (i, 0)),
                  pl.BlockSpec((1, LANES), lambda i: (0, 0)),
                  pl.BlockSpec((1, LANES), lambda i: (0, 0))],
        out_specs=[pl.BlockSpec((tb, LANES), lambda i: (i, 0))] * 2,
        out_shape=[out, out],
        compiler_params=_params("parallel"),
        name="rope_tables",
    )(positions.reshape(n, 1), inv.reshape(1, LANES), sign.reshape(1, LANES))


def _mla_proj_kernel(x_ref, g_ref, c_ref, s_ref, win_ref, qn_ref, wq_ref, kvn_ref, wkv_ref,
                     q_out, k_out, v_out):
    hn = _rms(x_ref[...], g_ref[...]).astype(jnp.bfloat16)
    z = jnp.dot(hn, win_ref[...], preferred_element_type=jnp.float32)
    c = c_ref[...]
    s = s_ref[...]
    o_kv = MLA_Q_LORA
    o_kr = MLA_Q_LORA + MLA_KV_LORA
    k_rope = (z[:, o_kr:o_kr + LANES] * c + z[:, o_kr + LANES:o_kr + 2 * LANES] * s).astype(jnp.bfloat16)
    cq = _rms(z[:, :MLA_Q_LORA], qn_ref[...]).astype(jnp.bfloat16)
    q = jnp.dot(cq, wq_ref[...], preferred_element_type=jnp.float32)
    ckv = _rms(z[:, o_kv:o_kr], kvn_ref[...]).astype(jnp.bfloat16)
    kv = jnp.dot(ckv, wkv_ref[...], preferred_element_type=jnp.float32)
    scale = (MLA_NOPE + MLA_ROPE) ** -0.5
    hw = MLA_HEADS * LANES
    for h in range(MLA_HEADS):
        sl = slice(h * LANES, (h + 1) * LANES)
        q_out[h, :, :LANES] = (q[:, sl] * scale).astype(jnp.bfloat16)
        qr = q[:, hw + h * LANES:hw + (h + 1) * LANES] * c + q[:, 2 * hw + h * LANES:2 * hw + (h + 1) * LANES] * s
        q_out[h, :, LANES:] = (qr * scale).astype(jnp.bfloat16)
        k_out[h, :, :LANES] = kv[:, sl].astype(jnp.bfloat16)
        k_out[h, :, LANES:] = k_rope
        v_out[h] = kv[:, hw + h * LANES:hw + (h + 1) * LANES].astype(jnp.bfloat16)


def _swap_halves(w):
    half = w.shape[-1] // 2
    return jnp.concatenate([w[..., half:], w[..., :half]], axis=-1)


def _pad_lanes(w):
    return jnp.pad(w, [(0, 0)] * (w.ndim - 1) + [(0, LANES - w.shape[-1])])


def _mla_weights(w_in, w_q_up, w_kv_up):
    o_kr = MLA_Q_LORA + MLA_KV_LORA
    w_kr = w_in[:, o_kr:]
    win = jnp.concatenate([w_in[:, :o_kr], _pad_lanes(w_kr), _pad_lanes(_swap_halves(w_kr))], axis=1)
    wq = w_q_up.reshape(MLA_Q_LORA, MLA_HEADS, MLA_NOPE + MLA_ROPE)
    wq_n = wq[:, :, :MLA_NOPE].reshape(MLA_Q_LORA, -1)
    wq_r = _pad_lanes(wq[:, :, MLA_NOPE:]).reshape(MLA_Q_LORA, -1)
    wq_rs = _pad_lanes(_swap_halves(wq[:, :, MLA_NOPE:])).reshape(MLA_Q_LORA, -1)
    wqp = jnp.concatenate([wq_n, wq_r, wq_rs], axis=1)
    wkv = w_kv_up.reshape(MLA_KV_LORA, MLA_HEADS, MLA_NOPE + MLA_V)
    wkvp = jnp.concatenate([wkv[:, :, :MLA_NOPE].reshape(MLA_KV_LORA, -1),
                            wkv[:, :, MLA_NOPE:].reshape(MLA_KV_LORA, -1)], axis=1)
    return win.astype(jnp.bfloat16), wqp.astype(jnp.bfloat16), wkvp.astype(jnp.bfloat16)


def _mla_proj(h, g, cos_t, sin_t, win, q_norm, wq, kv_norm, wkv):
    n = h.shape[0]
    tb = TOKEN_BLOCK
    full = lambda a: pl.BlockSpec(a.shape, lambda i: (0,) * a.ndim)
    g = g.reshape(1, -1)
    q_norm = q_norm.reshape(1, -1)
    kv_norm = kv_norm.reshape(1, -1)
    qk_t = jax.ShapeDtypeStruct((MLA_HEADS, n, 2 * LANES), jnp.bfloat16)
    v_t = jax.ShapeDtypeStruct((MLA_HEADS, n, MLA_V), jnp.bfloat16)
    return pl.pallas_call(
        _mla_proj_kernel,
        grid=(n // tb,),
        in_specs=[pl.BlockSpec((tb, D_MODEL), lambda i: (i, 0)), full(g),
                  pl.BlockSpec((tb, LANES), lambda i: (i, 0)), pl.BlockSpec((tb, LANES), lambda i: (i, 0)),
                  full(win), full(q_norm), full(wq), full(kv_norm), full(wkv)],
        out_specs=[pl.BlockSpec((MLA_HEADS, tb, 2 * LANES), lambda i: (0, i, 0)),
                   pl.BlockSpec((MLA_HEADS, tb, 2 * LANES), lambda i: (0, i, 0)),
                   pl.BlockSpec((MLA_HEADS, tb, MLA_V), lambda i: (0, i, 0))],
        out_shape=[qk_t, qk_t, v_t],
        compiler_params=_params("parallel"),
        name="mla_proj",
    )(h, g, cos_t, sin_t, win, q_norm, wq, kv_norm, wkv)


def _attn_kernel(qi_ref, ki_ref, q_ref, k_ref, v_ref, o_ref, m_sc, l_sc, acc_sc):
    step = pl.program_id(1)
    qi = qi_ref[step]
    ki = ki_ref[step]

    @pl.when(ki == 0)
    def _():
        m_sc[...] = jnp.full(m_sc.shape, NEG_INF, jnp.float32)
        l_sc[...] = jnp.zeros(l_sc.shape, jnp.float32)
        acc_sc[...] = jnp.zeros(acc_sc.shape, jnp.float32)

    def update(masked):
        sc = lax.dot_general(q_ref[...], k_ref[...], (((1,), (1,)), ((), ())),
                             preferred_element_type=jnp.float32)
        if masked:
            row = lax.broadcasted_iota(jnp.int32, sc.shape, 0)
            col = lax.broadcasted_iota(jnp.int32, sc.shape, 1)
            sc = jnp.where(col <= row, sc, NEG_INF)
        m_prev = m_sc[...]
        m_next = jnp.maximum(m_prev, jnp.max(sc, axis=1, keepdims=True))
        p = jnp.exp(sc - m_next[:, :1])
        alpha = jnp.exp(m_prev - m_next)
        l_sc[...] = alpha * l_sc[...] + jnp.sum(p, axis=1, keepdims=True)
        acc_sc[...] = alpha * acc_sc[...] + jnp.dot(p.astype(jnp.bfloat16), v_ref[...],
                                                    preferred_element_type=jnp.float32)
        m_sc[...] = m_next

    @pl.when(ki < qi)
    def _():
        update(False)

    @pl.when(ki == qi)
    def _():
        update(True)
        o_ref[...] = (acc_sc[...] / l_sc[...]).astype(o_ref.dtype)


def _attention(q, k, v, seq, blk, b, q_lo, q_hi, after=None):
    nb = seq // blk
    qi = jnp.array([i for i in range(q_lo, q_hi) for _ in range(i + 1)], jnp.int32)
    ki = jnp.array([j for i in range(q_lo, q_hi) for j in range(i + 1)], jnp.int32)
    if after is not None:
        qi = qi + jnp.minimum(after, 0)
    grid_spec = pltpu.PrefetchScalarGridSpec(
        num_scalar_prefetch=2,
        grid=(MLA_HEADS, qi.shape[0]),
        in_specs=[pl.BlockSpec((None, blk, 2 * LANES), lambda h, s, qi, ki: (h, b * nb + qi[s], 0)),
                  pl.BlockSpec((None, blk, 2 * LANES), lambda h, s, qi, ki: (h, b * nb + ki[s], 0)),
                  pl.BlockSpec((None, blk, MLA_V), lambda h, s, qi, ki: (h, b * nb + ki[s], 0))],
        out_specs=pl.BlockSpec((blk, MLA_V), lambda h, s, qi, ki: (qi[s] - q_lo, h)),
        scratch_shapes=[pltpu.VMEM((blk, LANES), jnp.float32), pltpu.VMEM((blk, LANES), jnp.float32),
                        pltpu.VMEM((blk, MLA_V), jnp.float32)],
    )
    return pl.pallas_call(
        _attn_kernel,
        grid_spec=grid_spec,
        out_shape=jax.ShapeDtypeStruct(((q_hi - q_lo) * blk, MLA_HEADS * MLA_V), jnp.bfloat16),
        compiler_params=_params("parallel", "arbitrary"),
        name="mla_attention",
    )(qi, ki, q, k, v)


def _col_reduce(x, op, reduce_fn):
    slabs = [x[i:i + 8] for i in range(0, x.shape[0], 8)]
    while len(slabs) > 1:
        nxt = [op(slabs[i], slabs[i + 1]) for i in range(0, len(slabs) - 1, 2)]
        if len(slabs) % 2:
            nxt.append(slabs[-1])
        slabs = nxt
    return reduce_fn(slabs[0], axis=0, keepdims=True)


def _top_rows(vals, ids, count, out_rows):
    t = vals.shape[1]
    big = jnp.int32(2 ** 30)
    orow = lax.broadcasted_iota(jnp.int32, (out_rows, t), 0)

    def body(r, carry):
        cur, ov, oi = carry
        m = _col_reduce(cur, jnp.maximum, jnp.max)
        pick = _col_reduce(jnp.where(cur == m, ids, big), jnp.minimum, jnp.min)
        cur = jnp.where(ids == pick, NEG_INF, cur)
        ov = jnp.where(orow == r, m, ov)
        oi = jnp.where(orow == r, pick, oi)
        return cur, ov, oi

    init = (vals, jnp.zeros((out_rows, t), jnp.float32), jnp.zeros((out_rows, t), jnp.int32))
    _, ov, oi = lax.fori_loop(0, count, body, init)
    return ov, oi


_ROW_SLABS = [(0, 0, 16), (1, 0, 8)] + [(a, 0, 8) for a in range(2, 8)]
_COL_SLAB = (8, 16, 0)
_PAIR_ROWS = sum(hi - lo for _, lo, hi in _ROW_SLABS) + (_COL_SLAB[1] - _COL_SLAB[0])


def _slot_weight_words(act, gate):
    hi = _bf16_bits(gate * (0.5 * act * (1.0 + lax.erf(act * (2.0 ** -0.5)))))
    return hi | (hi >> 16)


def _route_kernel(*refs, with_prev):
    if with_prev:
        (o_ref, h_ref, wo_ref, g_ref, wqt_ref, keys_ref, pos_ref, act_p_ref, gate_p_ref,
         hn_out, xn_out, eid_out, gate_out, wts_p_out, qt_sc, v_sc, i_sc) = refs
        wts_p_out[...] = _slot_weight_words(act_p_ref[...], gate_p_ref[...])
    else:
        (o_ref, h_ref, wo_ref, g_ref, wqt_ref, keys_ref, pos_ref,
         hn_out, xn_out, eid_out, gate_out, qt_sc, v_sc, i_sc) = refs
    tb = h_ref.shape[0]
    hnew = h_ref[...] + jnp.dot(o_ref[...], wo_ref[...], preferred_element_type=jnp.float32)
    hn_out[...] = hnew
    xn = _rms(hnew, g_ref[...])
    xn_out[...] = _pack_pairs(xn)
    qt_sc[...] = lax.dot_general(wqt_ref[...], xn.astype(jnp.bfloat16), (((1,), (1,)), ((), ())),
                                 preferred_element_type=jnp.float32).astype(jnp.bfloat16)
    key_ids = lax.broadcasted_iota(jnp.int32, (PEER_NKEYS, tb), 0)

    def group(g, carry):
        row0 = pl.multiple_of(g * PEER_HALF, PEER_HALF)
        st = jnp.dot(keys_ref[g], qt_sc[pl.ds(row0, PEER_HALF), :], preferred_element_type=jnp.float32)
        tv, ti = _top_rows(st, key_ids, PEER_TOPK, PEER_TOPK)
        out0 = pl.multiple_of(g * PEER_TOPK, PEER_TOPK)
        v_sc[pl.ds(out0, PEER_TOPK), :] = tv
        i_sc[pl.ds(out0, PEER_TOPK), :] = ti
        return carry

    lax.fori_loop(0, 2 * PEER_HEADS, group, 0)

    pos = pos_ref[...]

    def head(hd, carry):
        base = pl.multiple_of(hd * 2 * PEER_TOPK, 2 * PEER_TOPK)
        v1 = v_sc[pl.ds(base, PEER_TOPK), :]
        i1 = i_sc[pl.ds(base, PEER_TOPK), :]
        v2 = v_sc[pl.ds(base + PEER_TOPK, PEER_TOPK), :]
        i2 = i_sc[pl.ds(base + PEER_TOPK, PEER_TOPK), :]
        cv, ce = [], []
        for a, lo, hi in _ROW_SLABS:
            cv.append(v1[a:a + 1, :] + v2[lo:hi, :])
            ce.append(i1[a:a + 1, :] * PEER_NKEYS + i2[lo:hi, :])
        a_lo, a_hi, b = _COL_SLAB
        cv.append(v1[a_lo:a_hi, :] + v2[b:b + 1, :])
        ce.append(i1[a_lo:a_hi, :] * PEER_NKEYS + i2[b:b + 1, :])
        cv = jnp.concatenate(cv, axis=0)
        ce = jnp.concatenate(ce, axis=0)
        tv, tp = _top_rows(cv, jnp.broadcast_to(pos, cv.shape), PEER_TOPK, PEER_TOPK)
        te = jnp.zeros((PEER_TOPK, tb), jnp.int32)
        orow = lax.broadcasted_iota(jnp.int32, (PEER_TOPK, tb), 0)
        for r in range(PEER_TOPK):
            e_r = jnp.sum(jnp.where(pos == tp[r:r + 1, :], ce, 0), axis=0, keepdims=True)
            te = jnp.where(orow == r, e_r, te)
        ex = jnp.exp(tv - tv[0:1, :])
        gates = ex / jnp.sum(ex, axis=0, keepdims=True)
        out0 = pl.multiple_of(hd * PEER_TOPK, PEER_TOPK)
        v_sc[pl.ds(out0, PEER_TOPK), :] = gates
        i_sc[pl.ds(out0, PEER_TOPK), :] = te
        return carry

    lax.fori_loop(0, PEER_HEADS, head, 0)

    eid_out[...] = jnp.transpose(i_sc[:PEER_SLOTS, :].astype(jnp.float32)).astype(jnp.int32)
    gate_out[...] = jnp.transpose(v_sc[:PEER_SLOTS, :])


def _pair_positions():
    pos = [a * PEER_TOPK + b for a, lo, hi in _ROW_SLABS for b in range(lo, hi)]
    a_lo, a_hi, b = _COL_SLAB
    pos += [a * PEER_TOPK + b for a in range(a_lo, a_hi)]
    return jnp.array(pos, jnp.int32).reshape(_PAIR_ROWS, 1)


def _route(o, h, h_row0, w_out, g, wqt, keys, prev=None):
    n = o.shape[0]
    tb = TOKEN_BLOCK
    full = lambda a: pl.BlockSpec(a.shape, lambda i: (0,) * a.ndim)
    g = g.reshape(1, -1)
    pos = _pair_positions()
    row = pl.BlockSpec((tb, D_MODEL), lambda i: (i, 0))
    slot = pl.BlockSpec((tb, PEER_SLOTS), lambda i: (i, 0))
    h_row = pl.BlockSpec((tb, D_MODEL), lambda i: (i + h_row0 // tb, 0))
    f32 = jnp.float32
    args = [o, h, w_out, g, wqt, keys, pos]
    in_specs = [row, h_row, full(w_out), full(g), full(wqt), full(keys), full(pos)]
    out_specs = [row, pl.BlockSpec((tb, ROW_WORDS), lambda i: (i, 0)), slot, slot]
    out_shape = [jax.ShapeDtypeStruct((n, D_MODEL), f32), jax.ShapeDtypeStruct((n, ROW_WORDS), jnp.uint32),
                 jax.ShapeDtypeStruct((n, PEER_SLOTS), jnp.int32), jax.ShapeDtypeStruct((n, PEER_SLOTS), f32)]
    if prev is not None:
        args += list(prev)
        in_specs += [slot, slot]
        out_specs.append(slot)
        out_shape.append(jax.ShapeDtypeStruct((n, PEER_SLOTS), jnp.uint32))
    return pl.pallas_call(
        functools.partial(_route_kernel, with_prev=prev is not None),
        grid=(n // tb,),
        in_specs=in_specs,
        out_specs=out_specs,
        out_shape=out_shape,
        scratch_shapes=[pltpu.VMEM((2 * PEER_HEADS * PEER_HALF, tb), jnp.bfloat16),
                        pltpu.VMEM((2 * PEER_HEADS * PEER_TOPK, tb), f32),
                        pltpu.VMEM((2 * PEER_HEADS * PEER_TOPK, tb), jnp.int32)],
        compiler_params=_params("parallel"),
        name="peer_route",
    )(*args)


def _sc_mesh():
    return plsc.VectorSubcoreMesh(core_axis_name="c", subcore_axis_name="s")


def _sc_params():
    return pltpu.CompilerParams(needs_layout_passes=False)


def _worker_id():
    return lax.axis_index("s") * SC_CORES + lax.axis_index("c")


def _gather_rows(tab_hbm, idx_ref, dst_ref, sem):
    return pltpu.make_async_copy(tab_hbm.at[idx_ref], dst_ref, sem)


def _unpack16(word):
    lo = lax.bitcast_convert_type(word << 16, jnp.float32)
    hi = lax.bitcast_convert_type(word & jnp.uint32(0xFFFF0000), jnp.float32)
    return lo, hi


def _as_bf16(word):
    return plsc.bitcast(word, jnp.bfloat16)


def _sc_dots(tab, idx, x):
    t = x.shape[0]
    workers = SC_CORES * SC_SUBCORES
    tok_w = t // workers
    per_worker = tok_w * PEER_SLOTS
    w = GATHER_WINDOW
    slots = DOTS_SLOTS
    n_win = per_worker // w
    win_per_tok = PEER_SLOTS // w
    n_chunks = ROW_WORDS // SC_LANES
    f32 = jnp.float32

    @functools.partial(
        pl.kernel, out_type=jax.ShapeDtypeStruct((t * PEER_SLOTS,), f32), mesh=_sc_mesh(),
        scratch_types=[pltpu.VMEM((per_worker,), jnp.int32),
                       pltpu.VMEM((tok_w, ROW_WORDS), jnp.uint32),
                       pltpu.VMEM((slots, w, ROW_WORDS), jnp.uint32),
                       pltpu.VMEM((per_worker,), f32),
                       pltpu.VMEM((w * SC_LANES,), f32),
                       pltpu.SemaphoreType.DMA((slots,))],
        compiler_params=_sc_params(), name="peer_dots")
    def dots(tab_hbm, i_hbm, x_hbm, act_hbm, idx_v, x_v, rows, act_v, part_v, sem):
        wid = _worker_id()
        base = pl.multiple_of(wid * per_worker, per_worker)
        tok0 = pl.multiple_of(wid * tok_w, tok_w)
        pltpu.sync_copy(i_hbm.at[pl.ds(base, per_worker)], idx_v)
        pltpu.sync_copy(x_hbm.at[pl.ds(tok0, tok_w)], x_v)

        def gather(win, slot):
            ix = idx_v.at[pl.ds(pl.multiple_of(win * w, w), w)]
            return _gather_rows(tab_hbm, ix, rows.at[slot], sem.at[slot])

        for s in range(slots - 1):
            gather(s, s).start()
        lane = lax.iota(jnp.int32, SC_LANES)

        @pl.loop(0, n_win)
        def _(win):
            slot = lax.rem(win, slots)
            nxt = win + slots - 1

            @pl.when(nxt < n_win)
            def _():
                gather(nxt, lax.rem(nxt, slots)).start()

            gather(win, slot).wait()
            tok = win // win_per_tok

            @plsc.parallel_loop(0, w // DOT_ROWS)
            def _(g):
                r0 = g * DOT_ROWS
                acc_lo = [jnp.zeros((SC_LANES,), f32) for _ in range(DOT_ROWS)]
                acc_hi = [jnp.zeros((SC_LANES,), f32) for _ in range(DOT_ROWS)]
                for c in range(0, n_chunks, 2):
                    xa = _as_bf16(x_v[tok, pl.ds(c * SC_LANES, SC_LANES)])
                    xb = _as_bf16(x_v[tok, pl.ds((c + 1) * SC_LANES, SC_LANES)])
                    for r in range(DOT_ROWS):
                        ra = _as_bf16(rows[slot, r0 + r, pl.ds(c * SC_LANES, SC_LANES)])
                        rb = _as_bf16(rows[slot, r0 + r, pl.ds((c + 1) * SC_LANES, SC_LANES)])
                        lo, hi = _unpack16(plsc.bitcast(ra * xa + rb * xb, jnp.uint32))
                        acc_lo[r] = acc_lo[r] + lo
                        acc_hi[r] = acc_hi[r] + hi
                for r in range(DOT_ROWS):
                    part_v[pl.ds(pl.multiple_of((r0 + r) * SC_LANES, SC_LANES), SC_LANES)] = acc_lo[r] + acc_hi[r]

            for blk in range(w // SC_LANES):
                res = jnp.zeros((SC_LANES,), f32)
                for l in range(SC_LANES):
                    res = res + plsc.load_gather(part_v, [lane * SC_LANES + (blk * SC_LANES * SC_LANES + l)])
                act_v[pl.ds(pl.multiple_of(win * w + blk * SC_LANES, SC_LANES), SC_LANES)] = res

        pltpu.sync_copy(act_v, act_hbm.at[pl.ds(base, per_worker)])

    return dots(tab, idx, x)


def _sc_axpy(tab, idx, wts):
    p = idx.shape[0]
    t = p // PEER_SLOTS
    workers = SC_CORES * SC_SUBCORES
    tok_w = t // workers
    per_worker = tok_w * PEER_SLOTS
    w = GATHER_WINDOW
    slots = AXPY_SLOTS
    n_win = per_worker // w
    win_per_tok = PEER_SLOTS // w
    passes = ROW_WORDS // (SC_LANES * AXPY_CHUNKS)
    f32 = jnp.float32

    @functools.partial(
        pl.kernel, out_type=jax.ShapeDtypeStruct((t, D_MODEL), f32), mesh=_sc_mesh(),
        scratch_types=[pltpu.VMEM((per_worker,), jnp.int32),
                       pltpu.VMEM((per_worker,), jnp.uint32),
                       pltpu.VMEM((slots, w, ROW_WORDS), jnp.uint32),
                       pltpu.VMEM((2, D_MODEL), f32),
                       pltpu.SemaphoreType.DMA((slots,)),
                       pltpu.SemaphoreType.DMA((2,))],
        compiler_params=_sc_params(), name="peer_axpy")
    def axpy(tab_hbm, i_hbm, w_hbm, y_hbm, idx_v, w_v, rows, y_v, sem, sem_y):
        wid = _worker_id()
        base = pl.multiple_of(wid * per_worker, per_worker)
        tok0 = wid * tok_w
        pltpu.sync_copy(i_hbm.at[pl.ds(base, per_worker)], idx_v)
        pltpu.sync_copy(w_hbm.at[pl.ds(base, per_worker)], w_v)

        def gather(win, slot):
            ix = idx_v.at[pl.ds(pl.multiple_of(win * w, w), w)]
            return _gather_rows(tab_hbm, ix, rows.at[slot], sem.at[slot])

        def y_write(tok, buf):
            return pltpu.make_async_copy(y_v.at[buf], y_hbm.at[tok0 + tok], sem_y.at[buf])

        for s in range(slots - 1):
            gather(s, s).start()

        @pl.loop(0, n_win)
        def _(win):
            slot = lax.rem(win, slots)
            nxt = win + slots - 1

            @pl.when(nxt < n_win)
            def _():
                gather(nxt, lax.rem(nxt, slots)).start()

            gather(win, slot).wait()
            tok = win // win_per_tok
            part = lax.rem(win, win_per_tok)
            buf = lax.rem(tok, 2)

            @pl.when(part == 0)
            def _():
                @pl.when(tok >= 2)
                def _():
                    y_write(tok - 2, buf).wait()

                for c in range(D_MODEL // SC_LANES):
                    y_v[buf, pl.ds(c * SC_LANES, SC_LANES)] = jnp.zeros((SC_LANES,), f32)

            for ps in range(passes):

                def group(g, accs):
                    accs = list(accs)
                    row0 = pl.multiple_of(g * SC_LANES, SC_LANES)
                    w_grp = w_v[pl.ds(pl.multiple_of(win * w + row0, SC_LANES), SC_LANES)]
                    for k in range(0, SC_LANES, 2):
                        wa = _as_bf16(jnp.take_along_axis(w_grp, jnp.full((SC_LANES,), k, jnp.int32), axis=0))
                        wb = _as_bf16(jnp.take_along_axis(w_grp, jnp.full((SC_LANES,), k + 1, jnp.int32), axis=0))
                        for c in range(AXPY_CHUNKS):
                            col = (ps * AXPY_CHUNKS + c) * SC_LANES
                            ra = _as_bf16(rows[slot, row0 + k, pl.ds(col, SC_LANES)])
                            rb = _as_bf16(rows[slot, row0 + k + 1, pl.ds(col, SC_LANES)])
                            lo, hi = _unpack16(plsc.bitcast(ra * wa + rb * wb, jnp.uint32))
                            accs[2 * c] = accs[2 * c] + lo
                            accs[2 * c + 1] = accs[2 * c + 1] + hi
                    return tuple(accs)

                zero = tuple(jnp.zeros((SC_LANES,), f32) for _ in range(2 * AXPY_CHUNKS))
                accs = lax.fori_loop(0, w // SC_LANES, group, zero)
                for c in range(AXPY_CHUNKS):
                    col = (ps * AXPY_CHUNKS + c) * SC_LANES
                    y_v[buf, pl.ds(col, SC_LANES)] = y_v[buf, pl.ds(col, SC_LANES)] + accs[2 * c]
                    y_v[buf, pl.ds(ROW_WORDS + col, SC_LANES)] = (
                        y_v[buf, pl.ds(ROW_WORDS + col, SC_LANES)] + accs[2 * c + 1])

            @pl.when(part == win_per_tok - 1)
            def _():
                y_write(tok, buf).start()

        for tok in (tok_w - 2, tok_w - 1):
            y_write(tok, tok % 2).wait()

    return axpy(tab, idx, wts)


def _slot_weight_kernel(act_ref, gate_ref, o_ref):
    o_ref[...] = _slot_weight_words(act_ref[...], gate_ref[...])


def _slot_weights(act, gates):
    n = gates.shape[0]
    tb = 1024
    blk = pl.BlockSpec((tb, PEER_SLOTS), lambda i: (i, 0))
    return pl.pallas_call(
        _slot_weight_kernel, grid=(n // tb,), in_specs=[blk, blk], out_specs=blk,
        out_shape=jax.ShapeDtypeStruct((n, PEER_SLOTS), jnp.uint32),
        compiler_params=_params("parallel"), name="peer_slot_weights",
    )(act, gates)


def _residual_kernel(h_ref, y_ref, g_ref, o_ref, *, final_norm):
    out = h_ref[...] + y_ref[...]
    o_ref[...] = _rms(out, g_ref[...]) if final_norm else out


def _residual(h, y, g_final, final_norm):
    n = h.shape[0]
    tb = 512
    blk = pl.BlockSpec((tb, D_MODEL), lambda i: (i, 0))
    g_final = g_final.reshape(1, -1)
    return pl.pallas_call(
        functools.partial(_residual_kernel, final_norm=final_norm),
        grid=(n // tb,), in_specs=[blk, blk, pl.BlockSpec((1, D_MODEL), lambda i: (0, 0))], out_specs=blk,
        out_shape=jax.ShapeDtypeStruct((n, D_MODEL), jnp.float32),
        compiler_params=_params("parallel"), name="peer_residual",
    )(h, y, g_final)


def _launch_slices(n):
    tc = min(GATHER_TOKENS, n)
    return [slice(c * tc, (c + 1) * tc) for c in range(n // tc)]


def _peer_dots(xp, eid, tab_u):
    n = xp.shape[0]
    act = jnp.concatenate([_sc_dots(tab_u, eid[tok].reshape(-1), xp[tok]) for tok in _launch_slices(n)])
    return act.reshape(n, PEER_SLOTS)


def _peer_finish(wts, eid, h, tab_v, g_final, final_norm):
    n = h.shape[0]
    y = jnp.concatenate([_sc_axpy(tab_v, eid[tok].reshape(-1), wts[tok].reshape(-1))
                         for tok in _launch_slices(n)])
    return _residual(h, y, g_final, final_norm)


_N_SUB = HG_CHUNK // HG_SUB
_OFF_PAIRS = [(i, j) for i in range(_N_SUB) for j in range(i)]


def _cum_matrix():
    t = jnp.arange(HG_CHUNK)[:, None]
    r = jnp.arange(HG_CHUNK)[None, :]
    sub = t // HG_SUB
    incl = r <= t
    before = r < sub * HG_SUB
    end = r < (sub + 1) * HG_SUB
    return jnp.concatenate([incl, before, end], axis=0).astype(jnp.float32)


def _hgrn_kernel(h_ref, g_ref, w_ref, lb_ref, on_ref, cum_ref, st_in_ref, o_ref, st_out_ref,
                 z_sc, st_sc, lb_sc):
    @pl.when(pl.program_id(0) == 0)
    def _():
        st_sc[...] = st_in_ref[...]

    tb = h_ref.shape[0]
    hn = _rms(h_ref[...], g_ref[...]).astype(jnp.bfloat16)
    z_sc[...] = jnp.dot(hn, w_ref[...], preferred_element_type=jnp.float32)
    lbr = lb_ref[...]
    mx = jnp.max(lbr, axis=0, keepdims=True)
    ex = jnp.exp(lbr - mx)
    prob = ex / jnp.sum(ex, axis=0, keepdims=True)
    lb_sc[...] = jnp.broadcast_to((prob[0:1, :] + prob[1:2, :]) - prob[0:1, :], lb_sc.shape)
    wf = HG_HEADS * HG_DK
    sub_row = lax.broadcasted_iota(jnp.int32, (HG_SUB, HG_DK), 0)
    lane64 = lax.broadcasted_iota(jnp.int32, (HG_SUB, HG_CHUNK), 1)

    mxu = jnp.bfloat16
    nt = (((1,), (1,)), ((), ()))

    def head(hd, carry):
        c0 = pl.multiple_of(hd * HG_DK, HG_DK)
        lb = lb_sc[0:1, pl.ds(c0, HG_DK)]
        on = on_ref[0:1, pl.ds(c0, HG_DK)]
        st = st_sc[hd]
        for ch in range(tb // HG_CHUNK):
            rows = slice(ch * HG_CHUNK, (ch + 1) * HG_CHUNK)
            qp = z_sc[rows, pl.ds(c0, HG_DK)]
            fp = z_sc[rows, pl.ds(pl.multiple_of(wf + c0, HG_DK), HG_DK)]
            v = z_sc[rows, pl.ds(pl.multiple_of(2 * wf + c0, HG_DK), HG_DK)]
            gp = z_sc[rows, pl.ds(pl.multiple_of(2 * wf + HG_HEADS * HG_DV + c0, HG_DK), HG_DK)]
            f = lb + (1.0 - lb) * jax.nn.sigmoid(fp)
            lf = jnp.log(f)
            k = 1.0 - f
            q = qp * jax.nn.sigmoid(qp)
            cums = jnp.dot(cum_ref[...], lf, precision=lax.Precision.HIGHEST,
                           preferred_element_type=jnp.float32)
            b = cums[:HG_CHUNK]
            b_start = cums[HG_CHUNK:2 * HG_CHUNK]
            b_end = cums[2 * HG_CHUNK:]
            q_hat = q * jnp.exp(b - b_start)
            k_hat = k * jnp.exp(b_end - b)
            o_inter = lax.dot_general((q_hat * jnp.exp(b_start)).astype(mxu), st.astype(mxu), nt,
                                      preferred_element_type=jnp.float32)
            stacked = []
            for (i, j) in _OFF_PAIRS:
                d_ij = jnp.exp(b_start[i * HG_SUB:i * HG_SUB + 1, :] - b_end[j * HG_SUB:j * HG_SUB + 1, :])
                stacked.append(q_hat[i * HG_SUB:(i + 1) * HG_SUB, :] * d_ij)
            stacked = jnp.concatenate(stacked, axis=0).astype(mxu)
            off = lax.dot_general(stacked, k_hat.astype(mxu), nt,
                                  preferred_element_type=jnp.float32)
            a_rows = []
            for i in range(_N_SUB):
                blk = jnp.zeros((HG_SUB, HG_CHUNK), jnp.float32)
                for p, (pi, pj) in enumerate(_OFF_PAIRS):
                    if pi == i:
                        in_j = (lane64 >= pj * HG_SUB) & (lane64 < (pj + 1) * HG_SUB)
                        blk = jnp.where(in_j, off[p * HG_SUB:(p + 1) * HG_SUB, :], blk)
                b_blk = b[i * HG_SUB:(i + 1) * HG_SUB, :]
                q_blk = q[i * HG_SUB:(i + 1) * HG_SUB, :]
                for s in range(HG_SUB):
                    n = i * HG_SUB + s
                    e = jnp.exp(jnp.where(sub_row >= s, b_blk - b[n:n + 1, :], MASKED_LOG_DECAY))
                    col = jnp.sum(q_blk * k[n:n + 1, :] * e, axis=1, keepdims=True)
                    blk = jnp.where(lane64 == n, col, blk)
                a_rows.append(blk)
            a = jnp.concatenate(a_rows, axis=0).astype(mxu)
            o = o_inter + jnp.dot(a, v.astype(mxu), preferred_element_type=jnp.float32)
            b_last = b[HG_CHUNK - 1:HG_CHUNK, :]
            k_til = (k_hat * jnp.exp(b_last - b_end)).astype(mxu)
            upd = lax.dot_general(v.astype(mxu), k_til, (((0,), (0,)), ((), ())),
                                  preferred_element_type=jnp.float32)
            st = st * jnp.exp(b_last) + upd
            o = o * lax.rsqrt(jnp.mean(o * o, axis=-1, keepdims=True) + NORM_EPS)
            o = o * on * (gp * jax.nn.sigmoid(gp))
            o_ref[rows, pl.ds(c0, HG_DK)] = o.astype(o_ref.dtype)
        st_sc[hd] = st
        return carry

    lax.fori_loop(0, HG_HEADS, head, 0)

    @pl.when(pl.program_id(0) == pl.num_programs(0) - 1)
    def _():
        st_out_ref[...] = st_sc[...]


def _hgrn(h, state, g, w, lb_raw, out_norm):
    n = h.shape[0]
    tb = min(HG_BLOCK, n)
    g = g.reshape(1, -1)
    out_norm = out_norm.reshape(1, -1)
    cum = _cum_matrix()
    full = lambda a: pl.BlockSpec(a.shape, lambda i: (0,) * a.ndim)
    f32 = jnp.float32
    return pl.pallas_call(
        _hgrn_kernel,
        grid=(n // tb,),
        in_specs=[pl.BlockSpec((tb, D_MODEL), lambda i: (i, 0)), full(g), full(w),
                  full(lb_raw), full(out_norm), full(cum), full(state)],
        out_specs=[pl.BlockSpec((tb, HG_HEADS * HG_DV), lambda i: (i, 0)), full(state)],
        out_shape=[jax.ShapeDtypeStruct((n, HG_HEADS * HG_DV), jnp.bfloat16),
                   jax.ShapeDtypeStruct(state.shape, f32)],
        scratch_shapes=[pltpu.VMEM((tb, w.shape[1]), f32),
                        pltpu.VMEM((HG_HEADS, HG_DV, HG_DK), f32),
                        pltpu.VMEM((8, HG_HEADS * HG_DK), f32)],
        compiler_params=_params("arbitrary"),
        name="hgrn2",
    )(h, g, w, lb_raw, out_norm, cum, state)


def kernel(x, positions, ln_mix, ln_ffn, ln_final, mla_w_in, mla_q_norm, mla_w_q_up, mla_kv_norm,
           mla_w_kv_up, mla_w_out, hg_w_in, hg_lb, hg_out_norm, hg_w_out, peer_w_q, peer_sub_keys,
           peer_u, peer_v):
    batch, seq, d = x.shape
    n = batch * seq
    bf16 = jnp.bfloat16
    h = x.reshape(n, d)

    def route_weights(i):
        keys = peer_sub_keys[i].reshape(2 * PEER_HEADS, PEER_NKEYS, PEER_HALF).astype(bf16)
        return peer_w_q[i].T.astype(bf16), keys

    tables = [(_pack_table(peer_u[i]), _pack_table(peer_v[i])) for i in range(2)]
    cos_t, sin_t = _rope_tables(positions)
    win, wq, wkv = _mla_weights(mla_w_in[0], mla_w_q_up[0], mla_w_kv_up[0])
    q, k, v = _mla_proj(h, ln_mix[0], cos_t, sin_t, win, mla_q_norm[0], wq, mla_kv_norm[0], wkv)
    wqt0, keys0 = route_weights(0)
    wqt1, keys1 = route_weights(1)
    mla_wo = mla_w_out[0].astype(bf16)
    hg_wi = hg_w_in[0].astype(bf16)
    hg_wo = hg_w_out[0].astype(bf16)

    chunk = min(PIPE_TOKENS, seq)
    blk = min(ATTN_BLOCK, chunk)
    n_chunks = seq // chunk
    states = [jnp.zeros((HG_HEADS, HG_DV, HG_DK), jnp.float32) for _ in range(batch)]
    outs = {}

    open_chunk = [None, None]
    routed0 = []

    def route_and_dot(layer, b, j, o, h_src, row0):
        w_o, wqt, keys = ((mla_wo, wqt0, keys0), (hg_wo, wqt1, keys1))[layer]
        prev = open_chunk[layer]
        res = _route(o, h_src, row0, w_o, ln_ffn[layer], wqt, keys,
                     prev=None if prev is None else (prev[2], prev[3]))
        hc, xp, eid, gates = res[:4]
        act = _peer_dots(xp, eid, tables[layer][0])
        open_chunk[layer] = (b, j, act, gates, eid, hc)
        if layer == 0:
            routed0.append(eid)
        if prev is not None:
            finish(layer, prev, res[4])

    def finish(layer, rec, wts):
        b, j, _, _, eid, hc = rec
        out = _peer_finish(wts, eid, hc, tables[layer][1], ln_final, layer == 1)
        if layer == 0:
            layer1(b, j, out)
        else:
            outs[(b, j)] = out

    def layer0(b, j):
        after = routed0[-4][0, 0] if len(routed0) >= 4 else None
        o = _attention(q, k, v, seq, blk, b, j * chunk // blk, (j + 1) * chunk // blk, after)
        route_and_dot(0, b, j, o, h, b * seq + j * chunk)

    def layer1(b, j, hc):
        o, states[b] = _hgrn(hc, states[b], ln_mix[1], hg_wi, hg_lb, hg_out_norm[0])
        route_and_dot(1, b, j, o, hc, 0)

    for j in range(n_chunks):
        for b in range(batch):
            layer0(b, j)
    for layer in (0, 1):
        rec = open_chunk[layer]
        finish(layer, rec, _slot_weights(rec[2], rec[3]))
    out = jnp.concatenate([outs[(b, j)] for b in range(batch) for j in range(n_chunks)], axis=0)
    return out.reshape(batch, seq, d)
```

```python
import functools

import jax
import jax.numpy as jnp
from jax import lax
from jax.experimental import pallas as pl
from jax.experimental.pallas import tpu as pltpu
from jax.experimental.pallas import tpu_sc as plsc

D_MODEL = 1024
NORM_EPS = 1e-6
MLA_HEADS = 8
MLA_Q_LORA = 384
MLA_KV_LORA = 256
MLA_NOPE = 128
MLA_ROPE = 64
MLA_V = 128
ROPE_THETA = 10000.0
HG_HEADS = 8
HG_DK = 128
HG_DV = 128
PEER_HEADS = 8
PEER_NKEYS = 128
PEER_HALF = 128
PEER_TOPK = 16
PEER_SLOTS = PEER_HEADS * PEER_TOPK

LANES = 128
SC_CORES = 2
SC_SUBCORES = 16
VMEM_LIMIT = 48 * 1024 * 1024

ROW_WORDS = D_MODEL // 2
TOKEN_BLOCK = 256
ATTN_BLOCK = 1024
HG_BLOCK = 256
HG_CHUNK = 64
HG_SUB = 16
PIPE_TOKENS = 1024
GATHER_WINDOW = 32
DOTS_SLOTS = 5
AXPY_SLOTS = 7
GATHER_TOKENS = 1024
SC_LANES = 16
DOT_ROWS = 8
AXPY_CHUNKS = 8
NEG_INF = float("-inf")
MASKED_LOG_DECAY = -1e30


def _rms(x, g):
    return x * lax.rsqrt(jnp.mean(x * x, axis=-1, keepdims=True) + NORM_EPS) * g


def _params(*sem):
    return pltpu.CompilerParams(dimension_semantics=sem, vmem_limit_bytes=VMEM_LIMIT)


def _bf16_bits(x):
    return pltpu.bitcast(x.astype(jnp.bfloat16).astype(jnp.float32), jnp.uint32)


def _pack_pairs(t):
    return (_bf16_bits(t[:, :ROW_WORDS]) >> 16) | _bf16_bits(t[:, ROW_WORDS:])


def _pack_kernel(t_ref, o_ref):
    o_ref[...] = _pack_pairs(t_ref[...])


def _pack_table(tab):
    e, d = tab.shape
    rows = 512
    return pl.pallas_call(
        _pack_kernel,
        grid=(e // rows,),
        in_specs=[pl.BlockSpec((rows, d), lambda i: (i, 0))],
        out_specs=pl.BlockSpec((rows, d // 2), lambda i: (i, 0)),
        out_shape=jax.ShapeDtypeStruct((e, d // 2), jnp.uint32),
        compiler_params=_params("parallel"),
        name="pack_table",
    )(tab)


def _rope_kernel(pos_ref, inv_ref, sign_ref, c_ref, s_ref):
    ang = pos_ref[...].astype(jnp.float32) * inv_ref[...]
    c_ref[...] = jnp.cos(ang)
    s_ref[...] = jnp.sin(ang) * sign_ref[...]


def _rope_tables(positions):
    n = positions.size
    lane = jnp.arange(LANES)
    inv = ROPE_THETA ** (-(2 * (lane % (MLA_ROPE // 2))).astype(jnp.float32) / MLA_ROPE)
    sign = jnp.where((lane % MLA_ROPE) < MLA_ROPE // 2, -1.0, 1.0).astype(jnp.float32)
    tb = 1024
    out = jax.ShapeDtypeStruct((n, LANES), jnp.float32)
    return pl.pallas_call(
        _rope_kernel,
        grid=(n // tb,),
        in_specs=[pl.BlockSpec((tb, 1), lambda i: (i, 0)),
                  pl.BlockSpec((1, LANES), lambda i: (0, 0)),
                  pl.BlockSpec((1, LANES), lambda i: (0, 0))],
        out_specs=[pl.BlockSpec((tb, LANES), lambda i: (i, 0))] * 2,
        out_shape=[out, out],
        compiler_params=_params("parallel"),
        name="rope_tables",
    )(positions.reshape(n, 1), inv.reshape(1, LANES), sign.reshape(1, LANES))


def _mla_proj_kernel(x_ref, g_ref, c_ref, s_ref, win_ref, qn_ref, wq_ref, kvn_ref, wkv_ref,
                     q_out, k_out, v_out):
    hn = _rms(x_ref[...], g_ref[...]).astype(jnp.bfloat16)
    z = jnp.dot(hn, win_ref[...], preferred_element_type=jnp.float32)
    c = c_ref[...]
    s = s_ref[...]
    o_kv = MLA_Q_LORA
    o_kr = MLA_Q_LORA + MLA_KV_LORA
    k_rope = (z[:, o_kr:o_kr + LANES] * c + z[:, o_kr + LANES:o_kr + 2 * LANES] * s).astype(jnp.bfloat16)
    cq = _rms(z[:, :MLA_Q_LORA], qn_ref[...]).astype(jnp.bfloat16)
    q = jnp.dot(cq, wq_ref[...], preferred_element_type=jnp.float32)
    ckv = _rms(z[:, o_kv:o_kr], kvn_ref[...]).astype(jnp.bfloat16)
    kv = jnp.dot(ckv, wkv_ref[...], preferred_element_type=jnp.float32)
    scale = (MLA_NOPE + MLA_ROPE) ** -0.5
    hw = MLA_HEADS * LANES
    for h in range(MLA_HEADS):
        sl = slice(h * LANES, (h + 1) * LANES)
        q_out[h, :, :LANES] = (q[:, sl] * scale).astype(jnp.bfloat16)
        qr = q[:, hw + h * LANES:hw + (h + 1) * LANES] * c + q[:, 2 * hw + h * LANES:2 * hw + (h + 1) * LANES] * s
        q_out[h, :, LANES:] = (qr * scale).astype(jnp.bfloat16)
        k_out[h, :, :LANES] = kv[:, sl].astype(jnp.bfloat16)
        k_out[h, :, LANES:] = k_rope
        v_out[h] = kv[:, hw + h * LANES:hw + (h + 1) * LANES].astype(jnp.bfloat16)


def _swap_halves(w):
    half = w.shape[-1] // 2
    return jnp.concatenate([w[..., half:], w[..., :half]], axis=-1)


def _pad_lanes(w):
    return jnp.pad(w, [(0, 0)] * (w.ndim - 1) + [(0, LANES - w.shape[-1])])


def _mla_weights(w_in, w_q_up, w_kv_up):
    o_kr = MLA_Q_LORA + MLA_KV_LORA
    w_kr = w_in[:, o_kr:]
    win = jnp.concatenate([w_in[:, :o_kr], _pad_lanes(w_kr), _pad_lanes(_swap_halves(w_kr))], axis=1)
    wq = w_q_up.reshape(MLA_Q_LORA, MLA_HEADS, MLA_NOPE + MLA_ROPE)
    wq_n = wq[:, :, :MLA_NOPE].reshape(MLA_Q_LORA, -1)
    wq_r = _pad_lanes(wq[:, :, MLA_NOPE:]).reshape(MLA_Q_LORA, -1)
    wq_rs = _pad_lanes(_swap_halves(wq[:, :, MLA_NOPE:])).reshape(MLA_Q_LORA, -1)
    wqp = jnp.concatenate([wq_n, wq_r, wq_rs], axis=1)
    wkv = w_kv_up.reshape(MLA_KV_LORA, MLA_HEADS, MLA_NOPE + MLA_V)
    wkvp = jnp.concatenate([wkv[:, :, :MLA_NOPE].reshape(MLA_KV_LORA, -1),
                            wkv[:, :, MLA_NOPE:].reshape(MLA_KV_LORA, -1)], axis=1)
    return win.astype(jnp.bfloat16), wqp.astype(jnp.bfloat16), wkvp.astype(jnp.bfloat16)


def _mla_proj(h, g, cos_t, sin_t, win, q_norm, wq, kv_norm, wkv):
    n = h.shape[0]
    tb = TOKEN_BLOCK
    full = lambda a: pl.BlockSpec(a.shape, lambda i: (0,) * a.ndim)
    g = g.reshape(1, -1)
    q_norm = q_norm.reshape(1, -1)
    kv_norm = kv_norm.reshape(1, -1)
    qk_t = jax.ShapeDtypeStruct((MLA_HEADS, n, 2 * LANES), jnp.bfloat16)
    v_t = jax.ShapeDtypeStruct((MLA_HEADS, n, MLA_V), jnp.bfloat16)
    return pl.pallas_call(
        _mla_proj_kernel,
        grid=(n // tb,),
        in_specs=[pl.BlockSpec((tb, D_MODEL), lambda i: (i, 0)), full(g),
                  pl.BlockSpec((tb, LANES), lambda i: (i, 0)), pl.BlockSpec((tb, LANES), lambda i: (i, 0)),
                  full(win), full(q_norm), full(wq), full(kv_norm), full(wkv)],
        out_specs=[pl.BlockSpec((MLA_HEADS, tb, 2 * LANES), lambda i: (0, i, 0)),
                   pl.BlockSpec((MLA_HEADS, tb, 2 * LANES), lambda i: (0, i, 0)),
                   pl.BlockSpec((MLA_HEADS, tb, MLA_V), lambda i: (0, i, 0))],
        out_shape=[qk_t, qk_t, v_t],
        compiler_params=_params("parallel"),
        name="mla_proj",
    )(h, g, cos_t, sin_t, win, q_norm, wq, kv_norm, wkv)


def _attn_kernel(qi_ref, ki_ref, q_ref, k_ref, v_ref, o_ref, m_sc, l_sc, acc_sc):
    step = pl.program_id(1)
    qi = qi_ref[step]
    ki = ki_ref[step]

    @pl.when(ki == 0)
    def _():
        m_sc[...] = jnp.full(m_sc.shape, NEG_INF, jnp.float32)
        l_sc[...] = jnp.zeros(l_sc.shape, jnp.float32)
        acc_sc[...] = jnp.zeros(acc_sc.shape, jnp.float32)

    def update(masked):
        sc = lax.dot_general(q_ref[...], k_ref[...], (((1,), (1,)), ((), ())),
                             preferred_element_type=jnp.float32)
        if masked:
            row = lax.broadcasted_iota(jnp.int32, sc.shape, 0)
            col = lax.broadcasted_iota(jnp.int32, sc.shape, 1)
            sc = jnp.where(col <= row, sc, NEG_INF)
        m_prev = m_sc[...]
        m_next = jnp.maximum(m_prev, jnp.max(sc, axis=1, keepdims=True))
        p = jnp.exp(sc - m_next[:, :1])
        alpha = jnp.exp(m_prev - m_next)
        l_sc[...] = alpha * l_sc[...] + jnp.sum(p, axis=1, keepdims=True)
        acc_sc[...] = alpha * acc_sc[...] + jnp.dot(p.astype(jnp.bfloat16), v_ref[...],
                                                    preferred_element_type=jnp.float32)
        m_sc[...] = m_next

    @pl.when(ki < qi)
    def _():
        update(False)

    @pl.when(ki == qi)
    def _():
        update(True)
        o_ref[...] = (acc_sc[...] / l_sc[...]).astype(o_ref.dtype)


def _attention(q, k, v, seq, blk, b, q_lo, q_hi, after=None):
    nb = seq // blk
    qi = jnp.array([i for i in range(q_lo, q_hi) for _ in range(i + 1)], jnp.int32)
    ki = jnp.array([j for i in range(q_lo, q_hi) for j in range(i + 1)], jnp.int32)
    if after is not None:
        qi = qi + jnp.minimum(after, 0)
    grid_spec = pltpu.PrefetchScalarGridSpec(
        num_scalar_prefetch=2,
        grid=(MLA_HEADS, qi.shape[0]),
        in_specs=[pl.BlockSpec((None, blk, 2 * LANES), lambda h, s, qi, ki: (h, b * nb + qi[s], 0)),
                  pl.BlockSpec((None, blk, 2 * LANES), lambda h, s, qi, ki: (h, b * nb + ki[s], 0)),
                  pl.BlockSpec((None, blk, MLA_V), lambda h, s, qi, ki: (h, b * nb + ki[s], 0))],
        out_specs=pl.BlockSpec((blk, MLA_V), lambda h, s, qi, ki: (qi[s] - q_lo, h)),
        scratch_shapes=[pltpu.VMEM((blk, LANES), jnp.float32), pltpu.VMEM((blk, LANES), jnp.float32),
                        pltpu.VMEM((blk, MLA_V), jnp.float32)],
    )
    return pl.pallas_call(
        _attn_kernel,
        grid_spec=grid_spec,
        out_shape=jax.ShapeDtypeStruct(((q_hi - q_lo) * blk, MLA_HEADS * MLA_V), jnp.bfloat16),
        compiler_params=_params("parallel", "arbitrary"),
        name="mla_attention",
    )(qi, ki, q, k, v)


def _col_reduce(x, op, reduce_fn):
    slabs = [x[i:i + 8] for i in range(0, x.shape[0], 8)]
    while len(slabs) > 1:
        nxt = [op(slabs[i], slabs[i + 1]) for i in range(0, len(slabs) - 1, 2)]
        if len(slabs) % 2:
            nxt.append(slabs[-1])
        slabs = nxt
    return reduce_fn(slabs[0], axis=0, keepdims=True)


def _top_rows(vals, ids, count, out_rows):
    t = vals.shape[1]
    big = jnp.int32(2 ** 30)
    orow = lax.broadcasted_iota(jnp.int32, (out_rows, t), 0)

    def body(r, carry):
        cur, ov, oi = carry
        m = _col_reduce(cur, jnp.maximum, jnp.max)
        pick = _col_reduce(jnp.where(cur == m, ids, big), jnp.minimum, jnp.min)
        cur = jnp.where(ids == pick, NEG_INF, cur)
        ov = jnp.where(orow == r, m, ov)
        oi = jnp.where(orow == r, pick, oi)
        return cur, ov, oi

    init = (vals, jnp.zeros((out_rows, t), jnp.float32), jnp.zeros((out_rows, t), jnp.int32))
    _, ov, oi = lax.fori_loop(0, count, body, init)
    return ov, oi


_ROW_SLABS = [(0, 0, 16), (1, 0, 8)] + [(a, 0, 8) for a in range(2, 8)]
_COL_SLAB = (8, 16, 0)
_PAIR_ROWS = sum(hi - lo for _, lo, hi in _ROW_SLABS) + (_COL_SLAB[1] - _COL_SLAB[0])


def _slot_weight_words(act, gate):
    hi = _bf16_bits(gate * (0.5 * act * (1.0 + lax.erf(act * (2.0 ** -0.5)))))
    return hi | (hi >> 16)


def _route_kernel(*refs, with_prev):
    if with_prev:
        (o_ref, h_ref, wo_ref, g_ref, wqt_ref, keys_ref, pos_ref, act_p_ref, gate_p_ref,
         hn_out, xn_out, eid_out, gate_out, wts_p_out, qt_sc, v_sc, i_sc) = refs
        wts_p_out[...] = _slot_weight_words(act_p_ref[...], gate_p_ref[...])
    else:
        (o_ref, h_ref, wo_ref, g_ref, wqt_ref, keys_ref, pos_ref,
         hn_out, xn_out, eid_out, gate_out, qt_sc, v_sc, i_sc) = refs
    tb = h_ref.shape[0]
    hnew = h_ref[...] + jnp.dot(o_ref[...], wo_ref[...], preferred_element_type=jnp.float32)
    hn_out[...] = hnew
    xn = _rms(hnew, g_ref[...])
    xn_out[...] = _pack_pairs(xn)
    qt_sc[...] = lax.dot_general(wqt_ref[...], xn.astype(jnp.bfloat16), (((1,), (1,)), ((), ())),
                                 preferred_element_type=jnp.float32).astype(jnp.bfloat16)
    key_ids = lax.broadcasted_iota(jnp.int32, (PEER_NKEYS, tb), 0)

    def group(g, carry):
        row0 = pl.multiple_of(g * PEER_HALF, PEER_HALF)
        st = jnp.dot(keys_ref[g], qt_sc[pl.ds(row0, PEER_HALF), :], preferred_element_type=jnp.float32)
        tv, ti = _top_rows(st, key_ids, PEER_TOPK, PEER_TOPK)
        out0 = pl.multiple_of(g * PEER_TOPK, PEER_TOPK)
        v_sc[pl.ds(out0, PEER_TOPK), :] = tv
        i_sc[pl.ds(out0, PEER_TOPK), :] = ti
        return carry

    lax.fori_loop(0, 2 * PEER_HEADS, group, 0)

    pos = pos_ref[...]

    def head(hd, carry):
        base = pl.multiple_of(hd * 2 * PEER_TOPK, 2 * PEER_TOPK)
        v1 = v_sc[pl.ds(base, PEER_TOPK), :]
        i1 = i_sc[pl.ds(base, PEER_TOPK), :]
        v2 = v_sc[pl.ds(base + PEER_TOPK, PEER_TOPK), :]
        i2 = i_sc[pl.ds(base + PEER_TOPK, PEER_TOPK), :]
        cv, ce = [], []
        for a, lo, hi in _ROW_SLABS:
            cv.append(v1[a:a + 1, :] + v2[lo:hi, :])
            ce.append(i1[a:a + 1, :] * PEER_NKEYS + i2[lo:hi, :])
        a_lo, a_hi, b = _COL_SLAB
        cv.append(v1[a_lo:a_hi, :] + v2[b:b + 1, :])
        ce.append(i1[a_lo:a_hi, :] * PEER_NKEYS + i2[b:b + 1, :])
        cv = jnp.concatenate(cv, axis=0)
        ce = jnp.concatenate(ce, axis=0)
        tv, tp = _top_rows(cv, jnp.broadcast_to(pos, cv.shape), PEER_TOPK, PEER_TOPK)
        te = jnp.zeros((PEER_TOPK, tb), jnp.int32)
        orow = lax.broadcasted_iota(jnp.int32, (PEER_TOPK, tb), 0)
        for r in range(PEER_TOPK):
            e_r = jnp.sum(jnp.where(pos == tp[r:r + 1, :], ce, 0), axis=0, keepdims=True)
            te = jnp.where(orow == r, e_r, te)
        ex = jnp.exp(tv - tv[0:1, :])
        gates = ex / jnp.sum(ex, axis=0, keepdims=True)
        out0 = pl.multiple_of(hd * PEER_TOPK, PEER_TOPK)
        v_sc[pl.ds(out0, PEER_TOPK), :] = gates
        i_sc[pl.ds(out0, PEER_TOPK), :] = te
        return carry

    lax.fori_loop(0, PEER_HEADS, head, 0)

    eid_out[...] = jnp.transpose(i_sc[:PEER_SLOTS, :].astype(jnp.float32)).astype(jnp.int32)
    gate_out[...] = jnp.transpose(v_sc[:PEER_SLOTS, :])


def _pair_positions():
    pos = [a * PEER_TOPK + b for a, lo, hi in _ROW_SLABS for b in range(lo, hi)]
    a_lo, a_hi, b = _COL_SLAB
    pos += [a * PEER_TOPK + b for a in range(a_lo, a_hi)]
    return jnp.array(pos, jnp.int32).reshape(_PAIR_ROWS, 1)


def _route(o, h, h_row0, w_out, g, wqt, keys, prev=None):
    n = o.shape[0]
    tb = TOKEN_BLOCK
    full = lambda a: pl.BlockSpec(a.shape, lambda i: (0,) * a.ndim)
    g = g.reshape(1, -1)
    pos = _pair_positions()
    row = pl.BlockSpec((tb, D_MODEL), lambda i: (i, 0))
    slot = pl.BlockSpec((tb, PEER_SLOTS), lambda i: (i, 0))
    h_row = pl.BlockSpec((tb, D_MODEL), lambda i: (i + h_row0 // tb, 0))
    f32 = jnp.float32
    args = [o, h, w_out, g, wqt, keys, pos]
    in_specs = [row, h_row, full(w_out), full(g), full(wqt), full(keys), full(pos)]
    out_specs = [row, pl.BlockSpec((tb, ROW_WORDS), lambda i: (i, 0)), slot, slot]
    out_shape = [jax.ShapeDtypeStruct((n, D_MODEL), f32), jax.ShapeDtypeStruct((n, ROW_WORDS), jnp.uint32),
                 jax.ShapeDtypeStruct((n, PEER_SLOTS), jnp.int32), jax.ShapeDtypeStruct((n, PEER_SLOTS), f32)]
    if prev is not None:
        args += list(prev)
        in_specs += [slot, slot]
        out_specs.append(slot)
        out_shape.append(jax.ShapeDtypeStruct((n, PEER_SLOTS), jnp.uint32))
    return pl.pallas_call(
        functools.partial(_route_kernel, with_prev=prev is not None),
        grid=(n // tb,),
        in_specs=in_specs,
        out_specs=out_specs,
        out_shape=out_shape,
        scratch_shapes=[pltpu.VMEM((2 * PEER_HEADS * PEER_HALF, tb), jnp.bfloat16),
                        pltpu.VMEM((2 * PEER_HEADS * PEER_TOPK, tb), f32),
                        pltpu.VMEM((2 * PEER_HEADS * PEER_TOPK, tb), jnp.int32)],
        compiler_params=_params("parallel"),
        name="peer_route",
    )(*args)


def _sc_mesh():
    return plsc.VectorSubcoreMesh(core_axis_name="c", subcore_axis_name="s")


def _sc_params():
    return pltpu.CompilerParams(needs_layout_passes=False)


def _worker_id():
    return lax.axis_index("s") * SC_CORES + lax.axis_index("c")


def _gather_rows(tab_hbm, idx_ref, dst_ref, sem):
    return pltpu.make_async_copy(tab_hbm.at[idx_ref], dst_ref, sem)


def _unpack16(word):
    lo = lax.bitcast_convert_type(word << 16, jnp.float32)
    hi = lax.bitcast_convert_type(word & jnp.uint32(0xFFFF0000), jnp.float32)
    return lo, hi


def _as_bf16(word):
    return plsc.bitcast(word, jnp.bfloat16)


def _sc_dots(tab, idx, x):
    t = x.shape[0]
    workers = SC_CORES * SC_SUBCORES
    tok_w = t // workers
    per_worker = tok_w * PEER_SLOTS
    w = GATHER_WINDOW
    slots = DOTS_SLOTS
    n_win = per_worker // w
    win_per_tok = PEER_SLOTS // w
    n_chunks = ROW_WORDS // SC_LANES
    f32 = jnp.float32

    @functools.partial(
        pl.kernel, out_type=jax.ShapeDtypeStruct((t * PEER_SLOTS,), f32), mesh=_sc_mesh(),
        scratch_types=[pltpu.VMEM((per_worker,), jnp.int32),
                       pltpu.VMEM((tok_w, ROW_WORDS), jnp.uint32),
                       pltpu.VMEM((slots, w, ROW_WORDS), jnp.uint32),
                       pltpu.VMEM((per_worker,), f32),
                       pltpu.VMEM((w * SC_LANES,), f32),
                       pltpu.SemaphoreType.DMA((slots,))],
        compiler_params=_sc_params(), name="peer_dots")
    def dots(tab_hbm, i_hbm, x_hbm, act_hbm, idx_v, x_v, rows, act_v, part_v, sem):
        wid = _worker_id()
        base = pl.multiple_of(wid * per_worker, per_worker)
        tok0 = pl.multiple_of(wid * tok_w, tok_w)
        pltpu.sync_copy(i_hbm.at[pl.ds(base, per_worker)], idx_v)
        pltpu.sync_copy(x_hbm.at[pl.ds(tok0, tok_w)], x_v)

        def gather(win, slot):
            ix = idx_v.at[pl.ds(pl.multiple_of(win * w, w), w)]
            return _gather_rows(tab_hbm, ix, rows.at[slot], sem.at[slot])

        for s in range(slots - 1):
            gather(s, s).start()
        lane = lax.iota(jnp.int32, SC_LANES)

        @pl.loop(0, n_win)
        def _(win):
            slot = lax.rem(win, slots)
            nxt = win + slots - 1

            @pl.when(nxt < n_win)
            def _():
                gather(nxt, lax.rem(nxt, slots)).start()

            gather(win, slot).wait()
            tok = win // win_per_tok

            @plsc.parallel_loop(0, w // DOT_ROWS)
            def _(g):
                r0 = g * DOT_ROWS
                acc_lo = [jnp.zeros((SC_LANES,), f32) for _ in range(DOT_ROWS)]
                acc_hi = [jnp.zeros((SC_LANES,), f32) for _ in range(DOT_ROWS)]
                for c in range(0, n_chunks, 2):
                    xa = _as_bf16(x_v[tok, pl.ds(c * SC_LANES, SC_LANES)])
                    xb = _as_bf16(x_v[tok, pl.ds((c + 1) * SC_LANES, SC_LANES)])
                    for r in range(DOT_ROWS):
                        ra = _as_bf16(rows[slot, r0 + r, pl.ds(c * SC_LANES, SC_LANES)])
                        rb = _as_bf16(rows[slot, r0 + r, pl.ds((c + 1) * SC_LANES, SC_LANES)])
                        lo, hi = _unpack16(plsc.bitcast(ra * xa + rb * xb, jnp.uint32))
                        acc_lo[r] = acc_lo[r] + lo
                        acc_hi[r] = acc_hi[r] + hi
                for r in range(DOT_ROWS):
                    part_v[pl.ds(pl.multiple_of((r0 + r) * SC_LANES, SC_LANES), SC_LANES)] = acc_lo[r] + acc_hi[r]

            for blk in range(w // SC_LANES):
                res = jnp.zeros((SC_LANES,), f32)
                for l in range(SC_LANES):
                    res = res + plsc.load_gather(part_v, [lane * SC_LANES + (blk * SC_LANES * SC_LANES + l)])
                act_v[pl.ds(pl.multiple_of(win * w + blk * SC_LANES, SC_LANES), SC_LANES)] = res

        pltpu.sync_copy(act_v, act_hbm.at[pl.ds(base, per_worker)])

    return dots(tab, idx, x)


def _sc_axpy(tab, idx, wts):
    p = idx.shape[0]
    t = p // PEER_SLOTS
    workers = SC_CORES * SC_SUBCORES
    tok_w = t // workers
    per_worker = tok_w * PEER_SLOTS
    w = GATHER_WINDOW
    slots = AXPY_SLOTS
    n_win = per_worker // w
    win_per_tok = PEER_SLOTS // w
    passes = ROW_WORDS // (SC_LANES * AXPY_CHUNKS)
    f32 = jnp.float32

    @functools.partial(
        pl.kernel, out_type=jax.ShapeDtypeStruct((t, D_MODEL), f32), mesh=_sc_mesh(),
        scratch_types=[pltpu.VMEM((per_worker,), jnp.int32),
                       pltpu.VMEM((per_worker,), jnp.uint32),
                       pltpu.VMEM((slots, w, ROW_WORDS), jnp.uint32),
                       pltpu.VMEM((2, D_MODEL), f32),
                       pltpu.SemaphoreType.DMA((slots,)),
                       pltpu.SemaphoreType.DMA((2,))],
        compiler_params=_sc_params(), name="peer_axpy")
    def axpy(tab_hbm, i_hbm, w_hbm, y_hbm, idx_v, w_v, rows, y_v, sem, sem_y):
        wid = _worker_id()
        base = pl.multiple_of(wid * per_worker, per_worker)
        tok0 = wid * tok_w
        pltpu.sync_copy(i_hbm.at[pl.ds(base, per_worker)], idx_v)
        pltpu.sync_copy(w_hbm.at[pl.ds(base, per_worker)], w_v)

        def gather(win, slot):
            ix = idx_v.at[pl.ds(pl.multiple_of(win * w, w), w)]
            return _gather_rows(tab_hbm, ix, rows.at[slot], sem.at[slot])

        def y_write(tok, buf):
            return pltpu.make_async_copy(y_v.at[buf], y_hbm.at[tok0 + tok], sem_y.at[buf])

        for s in range(slots - 1):
            gather(s, s).start()

        @pl.loop(0, n_win)
        def _(win):
            slot = lax.rem(win, slots)
            nxt = win + slots - 1

            @pl.when(nxt < n_win)
            def _():
                gather(nxt, lax.rem(nxt, slots)).start()

            gather(win, slot).wait()
            tok = win // win_per_tok
            part = lax.rem(win, win_per_tok)
            buf = lax.rem(tok, 2)

            @pl.when(part == 0)
            def _():
                @pl.when(tok >= 2)
                def _():
                    y_write(tok - 2, buf).wait()

                for c in range(D_MODEL // SC_LANES):
                    y_v[buf, pl.ds(c * SC_LANES, SC_LANES)] = jnp.zeros((SC_LANES,), f32)

            for ps in range(passes):

                def group(g, accs):
                    accs = list(accs)
                    row0 = pl.multiple_of(g * SC_LANES, SC_LANES)
                    w_grp = w_v[pl.ds(pl.multiple_of(win * w + row0, SC_LANES), SC_LANES)]
                    for k in range(0, SC_LANES, 2):
                        wa = _as_bf16(jnp.take_along_axis(w_grp, jnp.full((SC_LANES,), k, jnp.int32), axis=0))
                        wb = _as_bf16(jnp.take_along_axis(w_grp, jnp.full((SC_LANES,), k + 1, jnp.int32), axis=0))
                        for c in range(AXPY_CHUNKS):
                            col = (ps * AXPY_CHUNKS + c) * SC_LANES
                            ra = _as_bf16(rows[slot, row0 + k, pl.ds(col, SC_LANES)])
                            rb = _as_bf16(rows[slot, row0 + k + 1, pl.ds(col, SC_LANES)])
                            lo, hi = _unpack16(plsc.bitcast(ra * wa + rb * wb, jnp.uint32))
                            accs[2 * c] = accs[2 * c] + lo
                            accs[2 * c + 1] = accs[2 * c + 1] + hi
                    return tuple(accs)

                zero = tuple(jnp.zeros((SC_LANES,), f32) for _ in range(2 * AXPY_CHUNKS))
                accs = lax.fori_loop(0, w // SC_LANES, group, zero)
                for c in range(AXPY_CHUNKS):
                    col = (ps * AXPY_CHUNKS + c) * SC_LANES
                    y_v[buf, pl.ds(col, SC_LANES)] = y_v[buf, pl.ds(col, SC_LANES)] + accs[2 * c]
                    y_v[buf, pl.ds(ROW_WORDS + col, SC_LANES)] = (
                        y_v[buf, pl.ds(ROW_WORDS + col, SC_LANES)] + accs[2 * c + 1])

            @pl.when(part == win_per_tok - 1)
            def _():
                y_write(tok, buf).start()

        for tok in (tok_w - 2, tok_w - 1):
            y_write(tok, tok % 2).wait()

    return axpy(tab, idx, wts)


def _slot_weight_kernel(act_ref, gate_ref, o_ref):
    o_ref[...] = _slot_weight_words(act_ref[...], gate_ref[...])


def _slot_weights(act, gates):
    n = gates.shape[0]
    tb = 1024
    blk = pl.BlockSpec((tb, PEER_SLOTS), lambda i: (i, 0))
    return pl.pallas_call(
        _slot_weight_kernel, grid=(n // tb,), in_specs=[blk, blk], out_specs=blk,
        out_shape=jax.ShapeDtypeStruct((n, PEER_SLOTS), jnp.uint32),
        compiler_params=_params("parallel"), name="peer_slot_weights",
    )(act, gates)


def _residual_kernel(h_ref, y_ref, g_ref, o_ref, *, final_norm):
    out = h_ref[...] + y_ref[...]
    o_ref[...] = _rms(out, g_ref[...]) if final_norm else out


def _residual(h, y, g_final, final_norm):
    n = h.shape[0]
    tb = 512
    blk = pl.BlockSpec((tb, D_MODEL), lambda i: (i, 0))
    g_final = g_final.reshape(1, -1)
    return pl.pallas_call(
        functools.partial(_residual_kernel, final_norm=final_norm),
        grid=(n // tb,), in_specs=[blk, blk, pl.BlockSpec((1, D_MODEL), lambda i: (0, 0))], out_specs=blk,
        out_shape=jax.ShapeDtypeStruct((n, D_MODEL), jnp.float32),
        compiler_params=_params("parallel"), name="peer_residual",
    )(h, y, g_final)


def _launch_slices(n):
    tc = min(GATHER_TOKENS, n)
    return [slice(c * tc, (c + 1) * tc) for c in range(n // tc)]


def _peer_dots(xp, eid, tab_u):
    n = xp.shape[0]
    act = jnp.concatenate([_sc_dots(tab_u, eid[tok].reshape(-1), xp[tok]) for tok in _launch_slices(n)])
    return act.reshape(n, PEER_SLOTS)


def _peer_finish(wts, eid, h, tab_v, g_final, final_norm):
    n = h.shape[0]
    y = jnp.concatenate([_sc_axpy(tab_v, eid[tok].reshape(-1), wts[tok].reshape(-1))
                         for tok in _launch_slices(n)])
    return _residual(h, y, g_final, final_norm)


_N_SUB = HG_CHUNK // HG_SUB
_OFF_PAIRS = [(i, j) for i in range(_N_SUB) for j in range(i)]


def _cum_matrix():
    t = jnp.arange(HG_CHUNK)[:, None]
    r = jnp.arange(HG_CHUNK)[None, :]
    sub = t // HG_SUB
    incl = r <= t
    before = r < sub * HG_SUB
    end = r < (sub + 1) * HG_SUB
    return jnp.concatenate([incl, before, end], axis=0).astype(jnp.float32)


def _hgrn_kernel(h_ref, g_ref, w_ref, lb_ref, on_ref, cum_ref, st_in_ref, o_ref, st_out_ref,
                 z_sc, st_sc, lb_sc):
    @pl.when(pl.program_id(0) == 0)
    def _():
        st_sc[...] = st_in_ref[...]

    tb = h_ref.shape[0]
    hn = _rms(h_ref[...], g_ref[...]).astype(jnp.bfloat16)
    z_sc[...] = jnp.dot(hn, w_ref[...], preferred_element_type=jnp.float32)
    lbr = lb_ref[...]
    mx = jnp.max(lbr, axis=0, keepdims=True)
    ex = jnp.exp(lbr - mx)
    prob = ex / jnp.sum(ex, axis=0, keepdims=True)
    lb_sc[...] = jnp.broadcast_to((prob[0:1, :] + prob[1:2, :]) - prob[0:1, :], lb_sc.shape)
    wf = HG_HEADS * HG_DK
    sub_row = lax.broadcasted_iota(jnp.int32, (HG_SUB, HG_DK), 0)
    lane64 = lax.broadcasted_iota(jnp.int32, (HG_SUB, HG_CHUNK), 1)

    mxu = jnp.bfloat16
    nt = (((1,), (1,)), ((), ()))

    def head(hd, carry):
        c0 = pl.multiple_of(hd * HG_DK, HG_DK)
        lb = lb_sc[0:1, pl.ds(c0, HG_DK)]
        on = on_ref[0:1, pl.ds(c0, HG_DK)]
        st = st_sc[hd]
        for ch in range(tb // HG_CHUNK):
            rows = slice(ch * HG_CHUNK, (ch + 1) * HG_CHUNK)
            qp = z_sc[rows, pl.ds(c0, HG_DK)]
            fp = z_sc[rows, pl.ds(pl.multiple_of(wf + c0, HG_DK), HG_DK)]
            v = z_sc[rows, pl.ds(pl.multiple_of(2 * wf + c0, HG_DK), HG_DK)]
            gp = z_sc[rows, pl.ds(pl.multiple_of(2 * wf + HG_HEADS * HG_DV + c0, HG_DK), HG_DK)]
            f = lb + (1.0 - lb) * jax.nn.sigmoid(fp)
            lf = jnp.log(f)
            k = 1.0 - f
            q = qp * jax.nn.sigmoid(qp)
            cums = jnp.dot(cum_ref[...], lf, precision=lax.Precision.HIGHEST,
                           preferred_element_type=jnp.float32)
            b = cums[:HG_CHUNK]
            b_start = cums[HG_CHUNK:2 * HG_CHUNK]
            b_end = cums[2 * HG_CHUNK:]
            q_hat = q * jnp.exp(b - b_start)
            k_hat = k * jnp.exp(b_end - b)
            o_inter = lax.dot_general((q_hat * jnp.exp(b_start)).astype(mxu), st.astype(mxu), nt,
                                      preferred_element_type=jnp.float32)
            stacked = []
            for (i, j) in _OFF_PAIRS:
                d_ij = jnp.exp(b_start[i * HG_SUB:i * HG_SUB + 1, :] - b_end[j * HG_SUB:j * HG_SUB + 1, :])
                stacked.append(q_hat[i * HG_SUB:(i + 1) * HG_SUB, :] * d_ij)
            stacked = jnp.concatenate(stacked, axis=0).astype(mxu)
            off = lax.dot_general(stacked, k_hat.astype(mxu), nt,
                                  preferred_element_type=jnp.float32)
            a_rows = []
            for i in range(_N_SUB):
                blk = jnp.zeros((HG_SUB, HG_CHUNK), jnp.float32)
                for p, (pi, pj) in enumerate(_OFF_PAIRS):
                    if pi == i:
                        in_j = (lane64 >= pj * HG_SUB) & (lane64 < (pj + 1) * HG_SUB)
                        blk = jnp.where(in_j, off[p * HG_SUB:(p + 1) * HG_SUB, :], blk)
                b_blk = b[i * HG_SUB:(i + 1) * HG_SUB, :]
                q_blk = q[i * HG_SUB:(i + 1) * HG_SUB, :]
                for s in range(HG_SUB):
                    n = i * HG_SUB + s
                    e = jnp.exp(jnp.where(sub_row >= s, b_blk - b[n:n + 1, :], MASKED_LOG_DECAY))
                    col = jnp.sum(q_blk * k[n:n + 1, :] * e, axis=1, keepdims=True)
                    blk = jnp.where(lane64 == n, col, blk)
                a_rows.append(blk)
            a = jnp.concatenate(a_rows, axis=0).astype(mxu)
            o = o_inter + jnp.dot(a, v.astype(mxu), preferred_element_type=jnp.float32)
            b_last = b[HG_CHUNK - 1:HG_CHUNK, :]
            k_til = (k_hat * jnp.exp(b_last - b_end)).astype(mxu)
            upd = lax.dot_general(v.astype(mxu), k_til, (((0,), (0,)), ((), ())),
                                  preferred_element_type=jnp.float32)
            st = st * jnp.exp(b_last) + upd
            o = o * lax.rsqrt(jnp.mean(o * o, axis=-1, keepdims=True) + NORM_EPS)
            o = o * on * (gp * jax.nn.sigmoid(gp))
            o_ref[rows, pl.ds(c0, HG_DK)] = o.astype(o_ref.dtype)
        st_sc[hd] = st
        return carry

    def head_pair(i, carry):
        head(2 * i, carry)
        return head(2 * i + 1, carry)

    lax.fori_loop(0, HG_HEADS // 2, head_pair, 0)

    @pl.when(pl.program_id(0) == pl.num_programs(0) - 1)
    def _():
        st_out_ref[...] = st_sc[...]


def _hgrn(h, state, g, w, lb_raw, out_norm):
    n = h.shape[0]
    tb = min(HG_BLOCK, n)
    g = g.reshape(1, -1)
    out_norm = out_norm.reshape(1, -1)
    cum = _cum_matrix()
    full = lambda a: pl.BlockSpec(a.shape, lambda i: (0,) * a.ndim)
    f32 = jnp.float32
    return pl.pallas_call(
        _hgrn_kernel,
        grid=(n // tb,),
        in_specs=[pl.BlockSpec((tb, D_MODEL), lambda i: (i, 0)), full(g), full(w),
                  full(lb_raw), full(out_norm), full(cum), full(state)],
        out_specs=[pl.BlockSpec((tb, HG_HEADS * HG_DV), lambda i: (i, 0)), full(state)],
        out_shape=[jax.ShapeDtypeStruct((n, HG_HEADS * HG_DV), jnp.bfloat16),
                   jax.ShapeDtypeStruct(state.shape, f32)],
        scratch_shapes=[pltpu.VMEM((tb, w.shape[1]), f32),
                        pltpu.VMEM((HG_HEADS, HG_DV, HG_DK), f32),
                        pltpu.VMEM((8, HG_HEADS * HG_DK), f32)],
        compiler_params=_params("arbitrary"),
        name="hgrn2",
    )(h, g, w, lb_raw, out_norm, cum, state)


def kernel(x, positions, ln_mix, ln_ffn, ln_final, mla_w_in, mla_q_norm, mla_w_q_up, mla_kv_norm,
           mla_w_kv_up, mla_w_out, hg_w_in, hg_lb, hg_out_norm, hg_w_out, peer_w_q, peer_sub_keys,
           peer_u, peer_v):
    batch, seq, d = x.shape
    n = batch * seq
    bf16 = jnp.bfloat16
    h = x.reshape(n, d)

    def route_weights(i):
        keys = peer_sub_keys[i].reshape(2 * PEER_HEADS, PEER_NKEYS, PEER_HALF).astype(bf16)
        return peer_w_q[i].T.astype(bf16), keys

    tables = [(_pack_table(peer_u[i]), _pack_table(peer_v[i])) for i in range(2)]
    cos_t, sin_t = _rope_tables(positions)
    win, wq, wkv = _mla_weights(mla_w_in[0], mla_w_q_up[0], mla_w_kv_up[0])
    q, k, v = _mla_proj(h, ln_mix[0], cos_t, sin_t, win, mla_q_norm[0], wq, mla_kv_norm[0], wkv)
    wqt0, keys0 = route_weights(0)
    wqt1, keys1 = route_weights(1)
    mla_wo = mla_w_out[0].astype(bf16)
    hg_wi = hg_w_in[0].astype(bf16)
    hg_wo = hg_w_out[0].astype(bf16)

    chunk = min(PIPE_TOKENS, seq)
    blk = min(ATTN_BLOCK, chunk)
    n_chunks = seq // chunk
    states = [jnp.zeros((HG_HEADS, HG_DV, HG_DK), jnp.float32) for _ in range(batch)]
    outs = {}

    open_chunk = [None, None]
    routed0 = []

    def route_and_dot(layer, b, j, o, h_src, row0):
        w_o, wqt, keys = ((mla_wo, wqt0, keys0), (hg_wo, wqt1, keys1))[layer]
        prev = open_chunk[layer]
        res = _route(o, h_src, row0, w_o, ln_ffn[layer], wqt, keys,
                     prev=None if prev is None else (prev[2], prev[3]))
        hc, xp, eid, gates = res[:4]
        act = _peer_dots(xp, eid, tables[layer][0])
        open_chunk[layer] = (b, j, act, gates, eid, hc)
        if layer == 0:
            routed0.append(eid)
        if prev is not None:
            finish(layer, prev, res[4])

    def finish(layer, rec, wts):
        b, j, _, _, eid, hc = rec
        out = _peer_finish(wts, eid, hc, tables[layer][1], ln_final, layer == 1)
        if layer == 0:
            layer1(b, j, out)
        else:
            outs[(b, j)] = out

    def layer0(b, j):
        after = routed0[-2][0, 0] if len(routed0) >= 2 else None
        o = _attention(q, k, v, seq, blk, b, j * chunk // blk, (j + 1) * chunk // blk, after)
        route_and_dot(0, b, j, o, h, b * seq + j * chunk)

    def layer1(b, j, hc):
        o, states[b] = _hgrn(hc, states[b], ln_mix[1], hg_wi, hg_lb, hg_out_norm[0])
        route_and_dot(1, b, j, o, hc, 0)

    for j in range(n_chunks):
        for b in range(batch):
            layer0(b, j)
    for layer in (0, 1):
        rec = open_chunk[layer]
        finish(layer, rec, _slot_weights(rec[2], rec[3]))
    out = jnp.concatenate([outs[(b, j)] for b in range(batch) for j in range(n_chunks)], axis=0)
    return out.reshape(batch, seq, d)
```
